```python
import math
import jax, jax.numpy as jnp
from jax import lax
import numpy as np

D_MODEL = 2048
BATCH = 4
SEQ = 2048
DEPTH = 1
DEC_BATCH = 128
DEC_SEQ = 8
PAST_LEN = 2048
PAGE_SIZE = 128

H_D = 8
DH_HALF = 64
DK_D = 2 * DH_HALF
DV_D = 2 * DH_HALF
DIFF_W = H_D * DV_D
H_M = 4
DH_M = 256
MLSTM_W = H_M * DH_M
MIX_W = MLSTM_W + DIFF_W
D_FF = -(-8 * D_MODEL // (3 * 256)) * 256
N_BUCKETS = 32
MAX_DIST = 128
Q_BLOCK = 128
M_CHUNK = 64
ALPHA = (2 * DEPTH) ** 0.25
BETA = (8 * DEPTH) ** -0.25
LN_EPS = 1e-5
RMS_EPS = 1e-6
SPLIT_SIZES = (H_D * DK_D, H_D * DK_D, H_D * DV_D, MLSTM_W, MLSTM_W, MLSTM_W, MLSTM_W, H_M, H_M)
PROJ_W = sum(SPLIT_SIZES)

kernel_name = 'hymba_mlstm_diffattn_deepnorm_step'


def split_cols(p):
    outs = []
    start = 0
    for size in SPLIT_SIZES:
        outs.append(p[..., start:start + size])
        start += size
    return outs


def layer_norm(x, g, b):
    xf = x.astype(jnp.float32)
    xc = xf - jnp.mean(xf, -1, keepdims=True)
    var = jnp.mean(xc * xc, -1, keepdims=True)
    y = xc * lax.rsqrt(var + LN_EPS) * g.astype(jnp.float32) + b.astype(jnp.float32)
    return y.astype(x.dtype)


def head_rms(h, w):
    hf = h.astype(jnp.float32)
    return hf * lax.rsqrt(jnp.mean(hf * hf, -1, keepdims=True) + RMS_EPS) * w.astype(jnp.float32)


def rel_bucket(q_pos, k_pos):
    n = jnp.maximum(q_pos[:, None] - k_pos[None, :], 0)
    max_exact = N_BUCKETS // 2
    nf = jnp.maximum(n, 1).astype(jnp.float32)
    large = max_exact + (jnp.log(nf / max_exact) / math.log(MAX_DIST / max_exact)
                         * (N_BUCKETS - max_exact)).astype(jnp.int32)
    large = jnp.minimum(large, N_BUCKETS - 1)
    return jnp.where(n < max_exact, n, large)


def diff_attend(q, k, v, q_pos, k_pos, rel_bias, lam):
    B, Lq = q.shape[:2]
    Lk = k.shape[1]
    q2 = q.reshape(B, Lq, H_D, 2, DH_HALF)
    k2 = k.reshape(B, Lk, H_D, 2, DH_HALF)
    s = jnp.einsum('bqhcd,bkhcd->cbhqk', q2, k2).astype(jnp.float32) * (DH_HALF ** -0.5)
    bias = jnp.transpose(rel_bias[rel_bucket(q_pos, k_pos)], (2, 0, 1)).astype(jnp.float32)
    causal = k_pos[None, :] <= q_pos[:, None]
    p = jax.nn.softmax(jnp.where(causal, s + bias, -jnp.inf), axis=-1)
    w = p[0] - lam * p[1]
    return jnp.einsum('bhqk,bkhd->bqhd', w.astype(v.dtype), v)


def diff_lambda(lq1, lk1, lq2, lk2, lam_init):
    e1 = jnp.exp(jnp.sum(lq1.astype(jnp.float32) * lk1.astype(jnp.float32)))
    e2 = jnp.exp(jnp.sum(lq2.astype(jnp.float32) * lk2.astype(jnp.float32)))
    return e1 - e2 + lam_init


def diff_post(o, norm_w, lam_init):
    y = head_rms(o, norm_w) * (1.0 - lam_init)
    return y.reshape(o.shape[0], o.shape[1], DIFF_W)


def mlstm_inputs(q, k, v, i_pre, f_pre, b_gates):
    B, L = q.shape[:2]
    shp = (B, L, H_M, DH_M)
    bg = b_gates.astype(jnp.float32)
    it = i_pre.astype(jnp.float32) + bg[:H_M]
    lf = jax.nn.log_sigmoid(f_pre.astype(jnp.float32) + bg[H_M:])
    return (q.reshape(shp).astype(jnp.float32),
            k.reshape(shp).astype(jnp.float32) * (DH_M ** -0.5),
            v.reshape(shp).astype(jnp.float32), it, lf)


def mlstm_chunk(carry, xs):
    C, n, m = carry
    q, k, v, it, lf = xs
    L = lf.shape[-1]
    b = jnp.cumsum(lf, axis=-1)
    causal = jnp.tril(jnp.ones((L, L), dtype=bool))
    log_d = jnp.where(causal, b[..., :, None] - b[..., None, :] + it[..., None, :], -jnp.inf)
    m_t = jnp.maximum(b + m[..., None], jnp.max(log_d, axis=-1))
    dmat = jnp.exp(log_d - m_t[..., None])
    inter = jnp.exp(b + m[..., None] - m_t)
    w = jnp.einsum('bhtd,bhsd->bhts', q, k) * dmat
    num = inter[..., None] * jnp.einsum('bhvk,bhtk->bhtv', C, q) + jnp.einsum('bhts,bhsv->bhtv', w, v)
    den = inter * jnp.einsum('bhk,bhtk->bht', n, q) + jnp.sum(w, -1)
    h = num / jnp.maximum(jnp.abs(den), jnp.exp(-m_t))[..., None]
    m_new = m_t[..., -1]
    g = jnp.exp(b[..., -1:] - b + it - m_new[..., None])
    decay = jnp.exp(b[..., -1] + m - m_new)
    C_new = decay[..., None, None] * C + jnp.einsum('bhsv,bhsk->bhvk', v * g[..., None], k)
    n_new = decay[..., None] * n + jnp.einsum('bhs,bhsk->bhk', g, k)
    return (C_new, n_new, m_new), h


def mlstm_prompt(q, k, v, it, lf):
    B, S = q.shape[:2]
    n_chunks = S // M_CHUNK

    def to_chunks(a):
        a = a.reshape((B, n_chunks, M_CHUNK) + a.shape[2:])
        return jnp.moveaxis(a, (1, 3), (0, 2))

    init = (jnp.zeros((B, H_M, DH_M, DH_M), jnp.float32),
            jnp.zeros((B, H_M, DH_M), jnp.float32),
            jnp.zeros((B, H_M), jnp.float32))
    state, h = lax.scan(mlstm_chunk, init, tuple(to_chunks(a) for a in (q, k, v, it, lf)))
    h = jnp.moveaxis(h, (0, 2), (1, 3)).reshape(B, S, H_M, DH_M)
    return h, state


def mlstm_sample(q, k, v, it, lf, C, n, m):
    xs = tuple(jnp.moveaxis(a, 2, 1) for a in (q, k, v, it, lf))
    carry = (C.astype(jnp.float32), n.astype(jnp.float32), m.astype(jnp.float32))
    state, h = mlstm_chunk(carry, xs)
    return jnp.moveaxis(h, 1, 2), state


def mlstm_post(h, o_pre, norm_w):
    B, L = h.shape[:2]
    o = jax.nn.sigmoid(o_pre.astype(jnp.float32)).reshape(B, L, H_M, DH_M)
    return (head_rms(h, norm_w.reshape(H_M, DH_M)) * o).reshape(B, L, MLSTM_W)


def mix_and_ffn(x, h_m, h_d, w_out, ln1_g, ln1_b, w_gate, w_up, w_down, ln2_g, ln2_b):
    mix = jnp.concatenate([h_m.astype(x.dtype), h_d.astype(x.dtype)], -1) @ w_out
    x = layer_norm(ALPHA * x + mix, ln1_g, ln1_b)
    ffn = (jax.nn.silu(x @ w_gate) * (x @ w_up)) @ w_down
    return layer_norm(ALPHA * x + ffn, ln2_g, ln2_b)


def setup_inputs(seed: int = 0) -> dict:
    key = jax.random.key(seed)
    ks = jax.random.split(key, 26)
    n_pages = PAST_LEN // PAGE_SIZE
    n_phys = (5 * DEC_BATCH * n_pages + 3) // 4

    def nrm(k, shape, scale):
        return jax.random.normal(k, shape, jnp.float32) * scale

    f_bias = jnp.linspace(3.0, 6.0, H_M, dtype=jnp.float32)[None, :] + nrm(ks[10], (DEPTH, H_M), 0.1)
    b_gates = jnp.concatenate([nrm(ks[9], (DEPTH, H_M), 0.1), f_bias], -1)
    page_table = jax.random.permutation(ks[7], n_phys)[:DEC_BATCH * n_pages]
    page_table = page_table.reshape(DEC_BATCH, n_pages).astype(jnp.int32)
    return {
        'x_prompt': nrm(ks[0], (BATCH, SEQ, D_MODEL), 1.0),
        'x_sample': nrm(ks[1], (DEC_BATCH, DEC_SEQ, D_MODEL), 1.0),
        'cache_k': nrm(ks[2], (DEPTH, n_phys, PAGE_SIZE, H_D, DK_D), 1.0),
        'cache_v': nrm(ks[3], (DEPTH, n_phys, PAGE_SIZE, H_D, DV_D), 1.0),
        'state_C': nrm(ks[4], (DEPTH, DEC_BATCH, H_M, DH_M, DH_M), 0.5),
        'state_n': nrm(ks[5], (DEPTH, DEC_BATCH, H_M, DH_M), 0.5),
        'state_m': nrm(ks[6], (DEPTH, DEC_BATCH, H_M), 1.0),
        'page_table': page_table,
        'rel_bias': nrm(ks[8], (N_BUCKETS, H_D), 0.2),
        'w_in': nrm(ks[11], (DEPTH, D_MODEL, PROJ_W), D_MODEL ** -0.5),
        'b_gates': b_gates,
        'lambda_q1': nrm(ks[12], (DEPTH, DH_HALF), 0.1),
        'lambda_k1': nrm(ks[13], (DEPTH, DH_HALF), 0.1),
        'lambda_q2': nrm(ks[14], (DEPTH, DH_HALF), 0.1),
        'lambda_k2': nrm(ks[15], (DEPTH, DH_HALF), 0.1),
        'diff_norm_w': 1.0 + nrm(ks[16], (DEPTH, DV_D), 0.02),
        'mlstm_norm_w': 1.0 + nrm(ks[17], (DEPTH, MLSTM_W), 0.02),
        'w_out': nrm(ks[18], (DEPTH, MIX_W, D_MODEL), BETA * MIX_W ** -0.5),
        'ln1_g': 1.0 + nrm(ks[19], (DEPTH, D_MODEL), 0.02),
        'ln1_b': nrm(ks[20], (DEPTH, D_MODEL), 0.02),
        'w_gate': nrm(ks[21], (DEPTH, D_MODEL, D_FF), D_MODEL ** -0.5),
        'w_up': nrm(ks[22], (DEPTH, D_MODEL, D_FF), D_MODEL ** -0.5),
        'w_down': nrm(ks[23], (DEPTH, D_FF, D_MODEL), BETA * D_FF ** -0.5),
        'ln2_g': 1.0 + nrm(ks[24], (DEPTH, D_MODEL), 0.02),
        'ln2_b': nrm(ks[25], (DEPTH, D_MODEL), 0.02),
    }


def reference(x_prompt, x_sample, cache_k, cache_v, state_C, state_n, state_m, page_table,
              rel_bias, w_in, b_gates, lambda_q1, lambda_k1, lambda_q2, lambda_k2,
              diff_norm_w, mlstm_norm_w, w_out, ln1_g, ln1_b, w_gate, w_up, w_down,
              ln2_g, ln2_b):
    B, S = x_prompt.shape[:2]
    Bs, Ls = x_sample.shape[:2]
    past = page_table.shape[1] * cache_k.shape[2]
    n_blk = S // Q_BLOCK
    pos_p = jnp.arange(S, dtype=jnp.int32)
    q_pos_s = past + jnp.arange(Ls, dtype=jnp.int32)
    k_pos_s = jnp.arange(past + Ls, dtype=jnp.int32)
    x_p, x_s = x_prompt, x_sample
    kp_l, vp_l, Cp_l, np_l, mp_l = [], [], [], [], []
    ks_l, vs_l, Cs_l, ns_l, ms_l = [], [], [], [], []
    for l in range(DEPTH):
        lam_init = 0.8 - 0.6 * math.exp(-0.3 * l)
        lam = diff_lambda(lambda_q1[l], lambda_k1[l], lambda_q2[l], lambda_k2[l], lam_init)

        qd, kd, vd, qm, km, vm, om, im, fm = split_cols(x_p @ w_in[l])
        qd = qd.reshape(B, S, H_D, DK_D)
        kd = kd.reshape(B, S, H_D, DK_D)
        vd = vd.reshape(B, S, H_D, DV_D)
        q_blocks = jnp.moveaxis(qd.reshape(B, n_blk, Q_BLOCK, H_D, DK_D), 1, 0)
        o_blocks = lax.map(lambda a: diff_attend(a[0], kd, vd, a[1], pos_p, rel_bias, lam),
                           (q_blocks, pos_p.reshape(n_blk, Q_BLOCK)))
        o_d = jnp.moveaxis(o_blocks, 0, 1).reshape(B, S, H_D, DV_D)
        h_m, (C_p, n_p, m_p) = mlstm_prompt(*mlstm_inputs(qm, km, vm, im, fm, b_gates[l]))
        x_p_next = mix_and_ffn(x_p, mlstm_post(h_m, om, mlstm_norm_w[l]),
                               diff_post(o_d, diff_norm_w[l], lam_init), w_out[l],
                               ln1_g[l], ln1_b[l], w_gate[l], w_up[l], w_down[l], ln2_g[l], ln2_b[l])

        qd_s, kd_s, vd_s, qm_s, km_s, vm_s, om_s, im_s, fm_s = split_cols(x_s @ w_in[l])
        qd_s = qd_s.reshape(Bs, Ls, H_D, DK_D)
        kd_s = kd_s.reshape(Bs, Ls, H_D, DK_D)
        vd_s = vd_s.reshape(Bs, Ls, H_D, DV_D)
        k_past = cache_k[l, page_table].reshape(Bs, past, H_D, DK_D)
        v_past = cache_v[l, page_table].reshape(Bs, past, H_D, DV_D)
        k_all = jnp.concatenate([k_past.astype(kd_s.dtype), kd_s], 1)
        v_all = jnp.concatenate([v_past.astype(vd_s.dtype), vd_s], 1)
        o_s = diff_attend(qd_s, k_all, v_all, q_pos_s, k_pos_s, rel_bias, lam)
        h_ms, (C_s, n_s, m_s) = mlstm_sample(*mlstm_inputs(qm_s, km_s, vm_s, im_s, fm_s, b_gates[l]),
                                             state_C[l], state_n[l], state_m[l])
        x_s_next = mix_and_ffn(x_s, mlstm_post(h_ms, om_s, mlstm_norm_w[l]),
                               diff_post(o_s, diff_norm_w[l], lam_init), w_out[l],
                               ln1_g[l], ln1_b[l], w_gate[l], w_up[l], w_down[l], ln2_g[l], ln2_b[l])

        kp_l.append(kd); vp_l.append(vd); Cp_l.append(C_p); np_l.append(n_p); mp_l.append(m_p)
        ks_l.append(kd_s); vs_l.append(vd_s); Cs_l.append(C_s); ns_l.append(n_s); ms_l.append(m_s)
        x_p, x_s = x_p_next, x_s_next

    return (x_p, x_s,
            jnp.stack(kp_l, 0), jnp.stack(vp_l, 0), jnp.stack(Cp_l, 0), jnp.stack(np_l, 0), jnp.stack(mp_l, 0),
            jnp.stack(ks_l, 0), jnp.stack(vs_l, 0), jnp.stack(Cs_l, 0), jnp.stack(ns_l, 0), jnp.stack(ms_l, 0))
```

```python
import functools
import math

import numpy as np
import jax
import jax.numpy as jnp
from jax import lax
from jax.experimental import pallas as pl
from jax.experimental.pallas import tpu as pltpu

D_MODEL = 2048
DEPTH = 1
PAGE_SIZE = 128
H_D = 8
DH_HALF = 64
DK_D = 2 * DH_HALF
DV_D = 2 * DH_HALF
DIFF_W = H_D * DV_D
H_M = 4
DH_M = 256
MLSTM_W = H_M * DH_M
D_FF = -(-8 * D_MODEL // (3 * 256)) * 256
N_BUCKETS = 32
MAX_DIST = 128
ALPHA = (2 * DEPTH) ** 0.25
LN_EPS = 1e-5
RMS_EPS = 1e-6
N_GATES = 2 * H_M
GATE_PAD = 128
DIFF_COLS = 3 * DIFF_W
MLSTM_COLS = 4 * MLSTM_W
MASK_VALUE = -1e30

VMEM_LIMIT_BYTES = 56 * 1024 * 1024

BF16 = jnp.bfloat16
F32 = jnp.float32


def _dot(a, b):
    return jnp.dot(a, b, preferred_element_type=F32)


def _dot_nt(a, b):
    return lax.dot_general(a, b, (((1,), (1,)), ((), ())), preferred_element_type=F32)


def _dot_tn(a, b):
    return lax.dot_general(a, b, (((0,), (0,)), ((), ())), preferred_element_type=F32)


def _params(*semantics):
    return pltpu.CompilerParams(dimension_semantics=semantics, vmem_limit_bytes=VMEM_LIMIT_BYTES)


def _const_spec(shape):
    n = len(shape)
    return pl.BlockSpec(shape, lambda *_: (0,) * n)


def _resident_spec(shape):
    n = len(shape)
    return pl.BlockSpec(shape, lambda *_: (0,) * n, pipeline_mode=pl.Buffered(1))


def _bucket_np(dist):
    n = np.maximum(dist, 0)
    max_exact = N_BUCKETS // 2
    nf = np.maximum(n, 1).astype(np.float32)
    large = max_exact + (np.log(nf / np.float32(max_exact)) / np.float32(math.log(MAX_DIST / max_exact))
                         * np.float32(N_BUCKETS - max_exact)).astype(np.int32)
    large = np.minimum(large, N_BUCKETS - 1)
    out = np.where(n < max_exact, n, large).astype(np.int32)
    return np.where(dist < 0, -1, out).astype(np.int32)


def _bias_kernel(rb_ref, pb_ref, sb_ref, pbias_ref, sbias_ref):
    h = pl.program_id(0)

    def lookup(bk):
        acc = jnp.full(bk.shape, MASK_VALUE, F32)
        for b in range(N_BUCKETS):
            acc = jnp.where(bk == b, rb_ref[b, h], acc)
        return acc

    for t in range(pb_ref.shape[0]):
        pbias_ref[0, t] = lookup(pb_ref[t])
    sbias_ref[0] = lookup(sb_ref[...])


def _bias_tables(rel_bias, blk, past, dec_seq):
    r = np.arange(blk)[:, None]
    c = np.arange(blk)[None, :]
    pb = np.stack([_bucket_np(r - c), _bucket_np(blk + r - c)], 0)
    far = N_BUCKETS - 1
    assert (_bucket_np(np.arange(MAX_DIST, 4 * past + 4 * blk)) == far).all()
    assert blk >= MAX_DIST and PAGE_SIZE >= MAX_DIST
    qpos = past + np.arange(dec_seq)[:, None]
    last = _bucket_np(qpos - (past - PAGE_SIZE + np.arange(PAGE_SIZE))[None, :])
    new = _bucket_np(qpos - (past + np.arange(PAGE_SIZE))[None, :])
    new[:, dec_seq:] = -1
    sb = np.concatenate([np.full((dec_seq, PAGE_SIZE), far, np.int32), last, new], 1)
    sb = np.concatenate([sb, sb], 0)
    return pl.pallas_call(
        _bias_kernel,
        grid=(H_D,),
        in_specs=[pl.BlockSpec(memory_space=pltpu.SMEM),
                  _const_spec(pb.shape), _const_spec(sb.shape)],
        out_specs=[pl.BlockSpec((1,) + pb.shape, lambda h: (h, 0, 0, 0)),
                   pl.BlockSpec((1,) + sb.shape, lambda h: (h, 0, 0))],
        out_shape=[jax.ShapeDtypeStruct((H_D,) + pb.shape, F32),
                   jax.ShapeDtypeStruct((H_D,) + sb.shape, F32)],
        compiler_params=_params("arbitrary"),
        name="bias_tables",
    )(rel_bias, jnp.asarray(pb), jnp.asarray(sb))


Q_SCALE = DH_HALF ** -0.5
K_SCALE = DH_M ** -0.5


def _in_proj_prompt_kernel(x_ref, wd_ref, wm_ref, wg_ref,
                           q_ref, k32_ref, k16_ref, v32_ref, v16_ref,
                           qm_ref, km_ref, vm_ref, om_ref, g_ref):
    x = x_ref[...].astype(BF16)
    r = _dot(x, wd_ref[:, 0:DIFF_W]) * Q_SCALE
    for h in range(H_D):
        q_ref[h] = r[:, h * DK_D:(h + 1) * DK_D].astype(BF16)
    r = _dot(x, wd_ref[:, DIFF_W:2 * DIFF_W])
    for h in range(H_D):
        k32_ref[:, h, :] = r[:, h * DK_D:(h + 1) * DK_D]
        k16_ref[h] = r[:, h * DK_D:(h + 1) * DK_D].astype(BF16)
    r = _dot(x, wd_ref[:, 2 * DIFF_W:3 * DIFF_W])
    for h in range(H_D):
        v32_ref[:, h, :] = r[:, h * DV_D:(h + 1) * DV_D]
        v16_ref[h] = r[:, h * DV_D:(h + 1) * DV_D].astype(BF16)
    qm_ref[...] = _dot(x, wm_ref[:, 0:MLSTM_W]).astype(BF16)
    km_ref[...] = (_dot(x, wm_ref[:, MLSTM_W:2 * MLSTM_W]) * K_SCALE).astype(BF16)
    vm_ref[...] = _dot(x, wm_ref[:, 2 * MLSTM_W:3 * MLSTM_W]).astype(BF16)
    om_ref[...] = _dot(x, wm_ref[:, 3 * MLSTM_W:4 * MLSTM_W]).astype(BF16)
    g_ref[...] = _dot(x, wg_ref[...])


def _in_proj_sample_kernel(x_ref, wd_ref, wm_ref, wg_ref,
                           q_ref, k_ref, v_ref, qm_ref, km_ref, vm_ref, om_ref, g_ref):
    x = x_ref[...].astype(BF16)
    q_ref[...] = _dot(x, wd_ref[:, 0:DIFF_W]) * Q_SCALE
    k = _dot(x, wd_ref[:, DIFF_W:2 * DIFF_W])
    v = _dot(x, wd_ref[:, 2 * DIFF_W:3 * DIFF_W])
    for h in range(H_D):
        k_ref[:, h, :] = k[:, h * DK_D:(h + 1) * DK_D]
        v_ref[:, h, :] = v[:, h * DV_D:(h + 1) * DV_D]
    qm_ref[...] = _dot(x, wm_ref[:, 0:MLSTM_W])
    km_ref[...] = _dot(x, wm_ref[:, MLSTM_W:2 * MLSTM_W]) * K_SCALE
    vm_ref[...] = _dot(x, wm_ref[:, 2 * MLSTM_W:3 * MLSTM_W])
    om_ref[...] = _dot(x, wm_ref[:, 3 * MLSTM_W:4 * MLSTM_W])
    g_ref[...] = _dot(x, wg_ref[...])


def _in_proj(x, wd, wm, wg, tm, prompt):
    t = x.shape[0]
    assert t % tm == 0
    row = lambda w: pl.BlockSpec((tm, w), lambda i: (i, 0))
    heads = pl.BlockSpec((H_D, tm, DK_D), lambda i: (0, i, 0))
    cache = pl.BlockSpec((tm, H_D, DK_D), lambda i: (i, 0, 0))
    sds = jax.ShapeDtypeStruct
    if prompt:
        body = _in_proj_prompt_kernel
        out_specs = [heads, cache, heads, cache, heads] + [row(MLSTM_W)] * 4 + [row(GATE_PAD)]
        out_shape = [sds((H_D, t, DK_D), BF16), sds((t, H_D, DK_D), F32), sds((H_D, t, DK_D), BF16),
                     sds((t, H_D, DV_D), F32), sds((H_D, t, DV_D), BF16)]
        out_shape += [sds((t, MLSTM_W), BF16)] * 4 + [sds((t, GATE_PAD), F32)]
    else:
        body = _in_proj_sample_kernel
        out_specs = [row(DIFF_W), cache, cache] + [row(MLSTM_W)] * 4 + [row(GATE_PAD)]
        out_shape = ([sds((t, DIFF_W), F32)] + [sds((t, H_D, DK_D), F32)] * 2 + [sds((t, MLSTM_W), F32)] * 4
                     + [sds((t, GATE_PAD), F32)])
    return pl.pallas_call(
        body,
        grid=(t // tm,),
        in_specs=[row(D_MODEL), _resident_spec(wd.shape), _resident_spec(wm.shape), _resident_spec(wg.shape)],
        out_specs=out_specs,
        out_shape=out_shape,
        compiler_params=_params("parallel"),
        name="in_proj_prompt" if prompt else "in_proj_sample",
    )(x, wd, wm, wg)


def _lambda_value(lam_ref, lam_init):
    lp = lam_ref[...]
    e1 = jnp.exp(jnp.sum(lp[0:1] * lp[1:2], axis=1, keepdims=True))
    e2 = jnp.exp(jnp.sum(lp[2:3] * lp[3:4], axis=1, keepdims=True))
    return e1 - e2 + lam_init


def _head_rms(o, w):
    return o * lax.rsqrt(jnp.mean(o * o, axis=-1, keepdims=True) + RMS_EPS) * w


def _layer_norm(x, g, b):
    xc = x - jnp.mean(x, axis=-1, keepdims=True)
    var = jnp.mean(xc * xc, axis=-1, keepdims=True)
    return xc * lax.rsqrt(var + LN_EPS) * g + b


def _attn_prompt_kernel(lam_init, blk, q_ref, k_ref, v_ref, pbias_ref, far_ref, lam_ref, nw_ref, o_ref):
    i = pl.program_id(1)
    lam = _lambda_value(lam_ref, lam_init)
    lane = lax.broadcasted_iota(jnp.int32, (blk, DK_D), 1)
    first_map = lane < DH_HALF

    def block(q2, h, j, bias):
        start = pl.multiple_of(j * blk, blk)
        kj = k_ref[h, pl.ds(start, blk), :]
        vj = v_ref[h, pl.ds(start, blk), :]
        return _dot_nt(q2, kj) + bias, vj

    def head(h, carry):
        qh = q_ref[h]
        zero = jnp.zeros_like(qh)
        q2 = jnp.concatenate([jnp.where(first_map, qh, zero), jnp.where(first_map, zero, qh)], axis=0)

        bd = pbias_ref[h, 0]
        s, vj = block(q2, h, i, jnp.concatenate([bd, bd], axis=0))
        m = jnp.max(s, axis=1, keepdims=True)
        p = jnp.exp(s - m)
        l = jnp.sum(p, axis=1, keepdims=True)
        acc = _dot(p.astype(BF16), vj)

        def update(state, s, vj):
            m, l, acc = state
            m_new = jnp.maximum(m, jnp.max(s, axis=1, keepdims=True))
            a = jnp.exp(m - m_new)
            p = jnp.exp(s - m_new)
            return (m_new, a * l + jnp.sum(p, axis=1, keepdims=True), a * acc + _dot(p.astype(BF16), vj))

        def sub_diagonal(state):
            bs = pbias_ref[h, 1]
            s, vj = block(q2, h, i - 1, jnp.concatenate([bs, bs], axis=0))
            return update(state, s, vj)

        state = lax.cond(i >= 1, sub_diagonal, lambda st: st, (m, l, acc))

        def far_block(j, state):
            s, vj = block(q2, h, j, far_ref[h])
            return update(state, s, vj)

        m, l, acc = lax.fori_loop(0, jnp.maximum(i - 1, 0), far_block, state)
        o = acc / l
        o = o[:blk] - lam * o[blk:]
        o_ref[h] = (_head_rms(o, nw_ref[...]) * (1.0 - lam_init)).astype(o_ref.dtype)
        return carry

    lax.fori_loop(0, H_D, head, 0)


def _attn_prompt(q, k, v, pbias, far, lam_params, norm_w, batch, seq, blk, lam_init):
    nq = seq // blk
    return pl.pallas_call(
        functools.partial(_attn_prompt_kernel, lam_init, blk),
        grid=(batch, nq),
        in_specs=[pl.BlockSpec((H_D, blk, DK_D), lambda b, i: (0, b * nq + i, 0)),
                  pl.BlockSpec((H_D, seq, DK_D), lambda b, i: (0, b, 0)),
                  pl.BlockSpec((H_D, seq, DV_D), lambda b, i: (0, b, 0)),
                  _const_spec(pbias.shape),
                  pl.BlockSpec(memory_space=pltpu.SMEM),
                  _const_spec(lam_params.shape),
                  _const_spec(norm_w.shape)],
        out_specs=pl.BlockSpec((H_D, blk, DV_D), lambda b, i: (0, b * nq + i, 0)),
        out_shape=jax.ShapeDtypeStruct((H_D, batch * seq, DV_D), BF16),
        compiler_params=_params("parallel", "parallel"),
        name="attn_prompt",
    )(q, k, v, pbias, far, lam_params, norm_w)


def _attn_sample_kernel(lam_init, n_pages, ls, pt_ref, q_ref, kn_ref, vn_ref, sbias_ref, lam_ref, nw_ref, *rest):
    del pt_ref
    k_pages = rest[:n_pages]
    v_pages = rest[n_pages:2 * n_pages]
    o_ref = rest[2 * n_pages]
    rows = 2 * ls * H_D
    lam = _lambda_value(lam_ref, lam_init)

    q = q_ref[...]
    qt = jnp.concatenate([q] * (2 * H_D), axis=0)
    rowi = lax.broadcasted_iota(jnp.int32, (rows, DIFF_W), 0)
    coli = lax.broadcasted_iota(jnp.int32, (rows, DIFF_W), 1)
    q_bd = jnp.where(coli // DH_HALF == rowi // ls, qt, 0.0).astype(BF16)

    pad = jnp.zeros((PAGE_SIZE - ls, DIFF_W), F32)
    def heads_to_lanes(ref, *lead):
        return jnp.concatenate([ref[(*lead, slice(None), h, slice(None))] for h in range(H_D)], axis=1)

    def page(ref):
        return heads_to_lanes(ref, 0, 0).astype(BF16)

    k_new = jnp.concatenate([heads_to_lanes(kn_ref), pad], axis=0).astype(BF16)
    v_new = jnp.concatenate([heads_to_lanes(vn_ref), pad], axis=0).astype(BF16)

    sb = sbias_ref[...]
    bias_far, bias_last, bias_new = sb[:, 0:128], sb[:, 128:256], sb[:, 256:384]
    scores = []
    for p in range(n_pages):
        kp = page(k_pages[p])
        scores.append(_dot_nt(q_bd, kp) + (bias_last if p == n_pages - 1 else bias_far))
    scores.append(_dot_nt(q_bd, k_new) + bias_new)

    m = scores[0]
    for s in scores[1:]:
        m = jnp.maximum(m, s)
    m = jnp.max(m, axis=1, keepdims=True)
    probs = [jnp.exp(s - m) for s in scores]
    l = probs[0]
    for p in probs[1:]:
        l = l + p
    l = jnp.sum(l, axis=1, keepdims=True)
    second_map = (lax.broadcasted_iota(jnp.int32, (rows, 1), 0) // ls) % 2 == 1
    fac = jnp.where(second_map, -lam, 1.0) / l

    acc = jnp.zeros((H_D * ls, DIFF_W), F32)
    for p in range(n_pages + 1):
        p3 = (probs[p] * fac).reshape(H_D, 2 * ls, PAGE_SIZE)
        w = (p3[:, :ls] + p3[:, ls:]).reshape(H_D * ls, PAGE_SIZE).astype(BF16)
        vp = v_new if p == n_pages else page(v_pages[p])
        acc = acc + _dot(w, vp)
    for h in range(H_D):
        o = acc[h * ls:(h + 1) * ls, h * DV_D:(h + 1) * DV_D]
        o_ref[:, h * DV_D:(h + 1) * DV_D] = _head_rms(o, nw_ref[...]) * (1.0 - lam_init)


def _attn_sample(page_table, q, k_new, v_new, cache_k, cache_v, sbias, lam_params, norm_w, ls, lam_init):
    bs, n_pages = page_table.shape
    seq_spec = pl.BlockSpec((ls, DIFF_W), lambda b, pt: (b, 0))
    new_spec = pl.BlockSpec((ls, H_D, DK_D), lambda b, pt: (b, 0, 0))

    def page_spec(p):
        return pl.BlockSpec((1, 1, PAGE_SIZE, H_D, DK_D), lambda b, pt: (0, pt[b, p], 0, 0, 0))

    grid_spec = pltpu.PrefetchScalarGridSpec(
        num_scalar_prefetch=1,
        grid=(bs,),
        in_specs=[seq_spec, new_spec, new_spec,
                  pl.BlockSpec(sbias.shape, lambda b, pt: (0, 0)),
                  pl.BlockSpec(lam_params.shape, lambda b, pt: (0, 0)),
                  pl.BlockSpec(norm_w.shape, lambda b, pt: (0, 0))]
        + [page_spec(p) for p in range(n_pages)] * 2,
        out_specs=seq_spec,
    )
    return pl.pallas_call(
        functools.partial(_attn_sample_kernel, lam_init, n_pages, ls),
        grid_spec=grid_spec,
        out_shape=jax.ShapeDtypeStruct((bs * ls, DIFF_W), F32),
        compiler_params=_params("parallel"),
        name="attn_sample",
    )(page_table, q, k_new, v_new, sbias, lam_params, norm_w,
      *([cache_k] * n_pages), *([cache_v] * n_pages))


def _mlstm_chunk(q, k, v, o_pre, i_pre, f_pre, c_prev, n_prev, m_prev, norm_w):
    L = q.shape[0]
    t_idx = lax.broadcasted_iota(jnp.int32, (L, L), 0)
    s_idx = lax.broadcasted_iota(jnp.int32, (L, L), 1)
    causal = s_idx <= t_idx
    eye = s_idx == t_idx

    def to_row(col):
        return jnp.sum(jnp.where(eye, col, 0.0), axis=0, keepdims=True)

    it_col = i_pre
    lf_col = jax.nn.log_sigmoid(f_pre)
    it_row = to_row(it_col)
    b_row = jnp.sum(jnp.where(t_idx <= s_idx, lf_col, 0.0), axis=0, keepdims=True)
    b_col = jnp.sum(jnp.where(causal, to_row(lf_col), 0.0), axis=1, keepdims=True)

    log_d = jnp.where(causal, b_col - b_row + it_row, -jnp.inf)
    m_t = jnp.maximum(b_col + m_prev, jnp.max(log_d, axis=1, keepdims=True))
    dmat = jnp.exp(log_d - m_t)
    inter = jnp.exp(b_col + m_prev - m_t)

    w = _dot_nt(q, k) * dmat
    kf = k.astype(F32)
    num = inter * _dot_nt(q, c_prev.astype(BF16)) + _dot(w.astype(BF16), v)
    den = inter * jnp.sum(q.astype(F32) * n_prev, axis=1, keepdims=True) + jnp.sum(w, axis=1, keepdims=True)
    h = num / jnp.maximum(jnp.abs(den), jnp.exp(-m_t))
    h = _head_rms(h, norm_w) * jax.nn.sigmoid(o_pre.astype(F32))

    m_new = m_t[L - 1:L]
    b_last = b_col[L - 1:L]
    g = jnp.exp(b_last - b_col + it_col - m_new)
    decay = jnp.exp(b_last + m_prev - m_new)
    gk = g * kf
    c_new = decay * c_prev + _dot_tn(v, gk.astype(BF16))
    n_new = decay * n_prev + jnp.sum(gk, axis=0, keepdims=True)
    return h, c_new, n_new, m_new


def _mlstm_prompt_kernel(bg_ref, q_ref, k_ref, v_ref, o_ref, g_ref, nw_ref, h_ref, c_ref, n_ref, m_ref):
    @pl.when(pl.program_id(1) == 0)
    def _():
        c_ref[...] = jnp.zeros_like(c_ref)
        n_ref[...] = jnp.zeros_like(n_ref)
        m_ref[...] = jnp.zeros_like(m_ref)

    for h in range(H_M):
        cols = slice(h * DH_M, (h + 1) * DH_M)
        out, c_new, n_new, m_new = _mlstm_chunk(
            q_ref[:, cols], k_ref[:, cols], v_ref[:, cols], o_ref[:, cols],
            g_ref[:, h:h + 1] + bg_ref[h], g_ref[:, H_M + h:H_M + h + 1] + bg_ref[H_M + h],
            c_ref[0, h], n_ref[0, h:h + 1, :], m_ref[0, h:h + 1, 0:1], nw_ref[:, cols])
        h_ref[:, cols] = out.astype(h_ref.dtype)
        c_ref[0, h] = c_new
        n_ref[0, h:h + 1, :] = n_new
        m_ref[0, h:h + 1, :] = jnp.broadcast_to(m_new, (1, m_ref.shape[2]))


def _mlstm_prompt(b_gates, qm, km, vm, om, gates, norm_w, batch, seq, chunk):
    nc = seq // chunk
    tok = lambda w: pl.BlockSpec((chunk, w), lambda b, c: (b * nc + c, 0))
    sds = jax.ShapeDtypeStruct
    return pl.pallas_call(
        _mlstm_prompt_kernel,
        grid=(batch, nc),
        in_specs=[pl.BlockSpec(memory_space=pltpu.SMEM),
                  tok(MLSTM_W), tok(MLSTM_W), tok(MLSTM_W), tok(MLSTM_W), tok(GATE_PAD),
                  _const_spec(norm_w.shape)],
        out_specs=[tok(MLSTM_W),
                   pl.BlockSpec((1, H_M, DH_M, DH_M), lambda b, c: (b, 0, 0, 0)),
                   pl.BlockSpec((1, H_M, DH_M), lambda b, c: (b, 0, 0)),
                   pl.BlockSpec((1, H_M, 128), lambda b, c: (b, 0, 0))],
        out_shape=[sds((batch * seq, MLSTM_W), BF16), sds((batch, H_M, DH_M, DH_M), F32),
                   sds((batch, H_M, DH_M), F32), sds((batch, H_M, 128), F32)],
        compiler_params=_params("parallel", "arbitrary"),
        name="mlstm_prompt",
    )(b_gates, qm, km, vm, om, gates, norm_w)


def _mlstm_sample_kernel(ls, group, bg_ref, q_ref, k_ref, v_ref, o_ref, g_ref, nw_ref, c_in, n_in, m_in,
                         h_ref, c_ref, n_ref, m_ref):
    def seq(s, carry):
        rows = pl.ds(pl.multiple_of(s * ls, ls), ls)
        for h in range(H_M):
            cols = slice(h * DH_M, (h + 1) * DH_M)
            out, c_new, n_new, m_new = _mlstm_chunk(
                q_ref[rows, cols].astype(BF16), k_ref[rows, cols].astype(BF16), v_ref[rows, cols].astype(BF16),
                o_ref[rows, cols],
                g_ref[rows, h:h + 1] + bg_ref[h], g_ref[rows, H_M + h:H_M + h + 1] + bg_ref[H_M + h],
                c_in[s, h], n_in[s, h:h + 1, :], m_in[s, h:h + 1, 0:1], nw_ref[:, cols])
            h_ref[rows, cols] = out
            c_ref[s, h] = c_new
            n_ref[s, h:h + 1, :] = n_new
            m_ref[s, h:h + 1, :] = jnp.broadcast_to(m_new, (1, m_ref.shape[2]))
        return carry

    lax.fori_loop(0, group, seq, 0)


def _mlstm_sample(b_gates, qm, km, vm, om, gates, norm_w, state_c, state_n, state_m, ls, group):
    bs = state_c.shape[0]
    assert bs % group == 0
    tok = lambda w: pl.BlockSpec((group * ls, w), lambda i: (i, 0))
    c_spec = pl.BlockSpec((group, H_M, DH_M, DH_M), lambda i: (i, 0, 0, 0))
    n_spec = pl.BlockSpec((group, H_M, DH_M), lambda i: (i, 0, 0))
    m_spec = pl.BlockSpec((group, H_M, 128), lambda i: (i, 0, 0))
    sds = jax.ShapeDtypeStruct
    return pl.pallas_call(
        functools.partial(_mlstm_sample_kernel, ls, group),
        grid=(bs // group,),
        in_specs=[pl.BlockSpec(memory_space=pltpu.SMEM),
                  tok(MLSTM_W), tok(MLSTM_W), tok(MLSTM_W), tok(MLSTM_W), tok(GATE_PAD),
                  _const_spec(norm_w.shape), c_spec, n_spec, m_spec],
        out_specs=[tok(MLSTM_W), c_spec, n_spec, m_spec],
        out_shape=[sds((bs * ls, MLSTM_W), F32), sds(state_c.shape, F32),
                   sds(state_n.shape, F32), sds((bs, H_M, 128), F32)],
        compiler_params=_params("parallel"),
        name="mlstm_sample",
    )(b_gates, qm, km, vm, om, gates, norm_w, state_c, state_n, state_m)


def _mix_kernel(head_major, x_ref, hm_ref, hd_ref, w_ref, g_ref, b_ref, y_ref):
    if head_major:
        hd = jnp.concatenate([hd_ref[h] for h in range(H_D)], axis=-1)
    else:
        hd = hd_ref[...]
    mix = _dot(hm_ref[...].astype(BF16), w_ref[0:MLSTM_W, :]) + _dot(hd.astype(BF16), w_ref[MLSTM_W:, :])
    y_ref[...] = _layer_norm(ALPHA * x_ref[...] + mix, g_ref[...], b_ref[...])


def _mix(x, hm, hd, w_out, ln_g, ln_b, tm, head_major):
    t = x.shape[0]
    assert t % tm == 0
    row = lambda w: pl.BlockSpec((tm, w), lambda i: (i, 0))
    hd_spec = pl.BlockSpec((H_D, tm, DV_D), lambda i: (0, i, 0)) if head_major else row(DIFF_W)
    return pl.pallas_call(
        functools.partial(_mix_kernel, head_major),
        grid=(t // tm,),
        in_specs=[row(D_MODEL), row(MLSTM_W), hd_spec, _resident_spec(w_out.shape),
                  _const_spec(ln_g.shape), _const_spec(ln_b.shape)],
        out_specs=row(D_MODEL),
        out_shape=jax.ShapeDtypeStruct((t, D_MODEL), F32),
        compiler_params=_params("parallel"),
        name="mix_ln",
    )(x, hm, hd, w_out, ln_g, ln_b)


def _ffn_kernel(x_ref, wg_ref, wu_ref, wd_ref, g_ref, b_ref, y_ref, xb_ref, acc_ref):
    j = pl.program_id(1)

    @pl.when(j == 0)
    def _():
        xb_ref[...] = x_ref[...].astype(BF16)
        acc_ref[...] = jnp.zeros_like(acc_ref)

    xb = xb_ref[...]
    gate = _dot(xb, wg_ref[...])
    up = _dot(xb, wu_ref[...])
    acc_ref[...] += _dot((jax.nn.silu(gate) * up).astype(BF16), wd_ref[...])

    @pl.when(j == pl.num_programs(1) - 1)
    def _():
        y_ref[...] = _layer_norm(ALPHA * x_ref[...] + acc_ref[...], g_ref[...], b_ref[...])


def _ffn(x, w_gate, w_up, w_down, ln_g, ln_b, tm, tf):
    t = x.shape[0]
    assert t % tm == 0 and D_FF % tf == 0
    row = pl.BlockSpec((tm, D_MODEL), lambda i, j: (i, 0))
    return pl.pallas_call(
        _ffn_kernel,
        grid=(t // tm, D_FF // tf),
        in_specs=[row,
                  pl.BlockSpec((D_MODEL, tf), lambda i, j: (0, j)),
                  pl.BlockSpec((D_MODEL, tf), lambda i, j: (0, j)),
                  pl.BlockSpec((tf, D_MODEL), lambda i, j: (j, 0)),
                  pl.BlockSpec(ln_g.shape, lambda i, j: (0, 0)),
                  pl.BlockSpec(ln_b.shape, lambda i, j: (0, 0))],
        out_specs=row,
        out_shape=jax.ShapeDtypeStruct((t, D_MODEL), F32),
        scratch_shapes=[pltpu.VMEM((tm, D_MODEL), BF16), pltpu.VMEM((tm, D_MODEL), F32)],
        compiler_params=_params("parallel", "arbitrary"),
        name="ffn_ln",
    )(x, w_gate, w_up, w_down, ln_g, ln_b)


def _tiles(seq):
    return dict(
        proj_tm=256,
        attn_blk=min(256, seq),
        mlstm_chunk=min(256, seq),
        sample_group=4,
        mix_tm=256,
        ffn_tm=512,
        ffn_tf=512,
    )


def kernel(x_prompt, x_sample, cache_k, cache_v, state_C, state_n, state_m, page_table, rel_bias, w_in, b_gates, lambda_q1, lambda_k1, lambda_q2, lambda_k2, diff_norm_w, mlstm_norm_w, w_out, ln1_g, ln1_b, w_gate, w_up, w_down, ln2_g, ln2_b):
    B, S, _ = x_prompt.shape
    Bs, Ls, _ = x_sample.shape
    n_pages = page_table.shape[1]
    past = n_pages * cache_k.shape[2]
    assert w_in.shape[0] == DEPTH == 1 and cache_k.shape[2] == PAGE_SIZE
    tl = _tiles(S)
    l = 0
    lam_init = 0.8 - 0.6 * math.exp(-0.3 * l)

    w = w_in[l]
    wd = w[:, :DIFF_COLS].astype(BF16)
    wm = w[:, DIFF_COLS:DIFF_COLS + MLSTM_COLS].astype(BF16)
    wg = jnp.pad(w[:, DIFF_COLS + MLSTM_COLS:], ((0, 0), (0, GATE_PAD - N_GATES))).astype(BF16)
    w_o = w_out[l].astype(BF16)
    w_g, w_u, w_d = w_gate[l].astype(BF16), w_up[l].astype(BF16), w_down[l].astype(BF16)
    lam_params = jnp.stack([lambda_q1[l], lambda_k1[l], lambda_q2[l], lambda_k2[l]], 0)
    dnw = diff_norm_w[l].reshape(1, DV_D)
    mnw = mlstm_norm_w[l].reshape(1, MLSTM_W)
    g1, b1 = ln1_g[l].reshape(1, D_MODEL), ln1_b[l].reshape(1, D_MODEL)
    g2, b2 = ln2_g[l].reshape(1, D_MODEL), ln2_b[l].reshape(1, D_MODEL)
    bg = b_gates[l]

    pbias, sbias = _bias_tables(rel_bias, tl["attn_blk"], past, Ls)
    sbias = sbias.reshape(H_D * 2 * Ls, 3 * 128)
    far = rel_bias[N_BUCKETS - 1]

    xp = x_prompt.reshape(B * S, D_MODEL)
    q16, k32, k16, v32, v16, qm, km, vm, om, gates = _in_proj(xp, wd, wm, wg, tl["proj_tm"], True)
    hd = _attn_prompt(q16, k16, v16, pbias, far, lam_params, dnw, B, S, tl["attn_blk"], lam_init)
    hm, c_p, n_p, m_p = _mlstm_prompt(bg, qm, km, vm, om, gates, mnw, B, S, tl["mlstm_chunk"])
    x1 = _mix(xp, hm, hd, w_o, g1, b1, tl["mix_tm"], True)
    y_p = _ffn(x1, w_g, w_u, w_d, g2, b2, tl["ffn_tm"], tl["ffn_tf"])

    xs = x_sample.reshape(Bs * Ls, D_MODEL)
    qs, ks, vs, qms, kms, vms, oms, gates_s = _in_proj(xs, wd, wm, wg, tl["proj_tm"], False)
    hd_s = _attn_sample(page_table, qs, ks, vs, cache_k, cache_v, sbias, lam_params, dnw, Ls, lam_init)
    m_in = jnp.broadcast_to(state_m[l][:, :, None], (Bs, H_M, 128))
    hm_s, c_s, n_s, m_s = _mlstm_sample(bg, qms, kms, vms, oms, gates_s, mnw,
                                        state_C[l], state_n[l], m_in, Ls, tl["sample_group"])
    x1s = _mix(xs, hm_s, hd_s, w_o, g1, b1, tl["mix_tm"], False)
    y_s = _ffn(x1s, w_g, w_u, w_d, g2, b2, tl["ffn_tm"], tl["ffn_tf"])

    return (y_p.reshape(B, S, D_MODEL), y_s.reshape(Bs, Ls, D_MODEL),
            k32.reshape(1, B, S, H_D, DK_D), v32.reshape(1, B, S, H_D, DV_D),
            c_p[None], n_p[None], m_p[None, :, :, 0],
            ks.reshape(1, Bs, Ls, H_D, DK_D), vs.reshape(1, Bs, Ls, H_D, DV_D),
            c_s[None], n_s[None], m_s[None, :, :, 0])
```

```python
import functools
import math

import numpy as np
import jax
import jax.numpy as jnp
from jax import lax
from jax.experimental import pallas as pl
from jax.experimental.pallas import tpu as pltpu

D_MODEL = 2048
DEPTH = 1
PAGE_SIZE = 128
H_D = 8
DH_HALF = 64
DK_D = 2 * DH_HALF
DV_D = 2 * DH_HALF
DIFF_W = H_D * DV_D
H_M = 4
DH_M = 256
MLSTM_W = H_M * DH_M
D_FF = -(-8 * D_MODEL // (3 * 256)) * 256
N_BUCKETS = 32
MAX_DIST = 128
ALPHA = (2 * DEPTH) ** 0.25
LN_EPS = 1e-5
RMS_EPS = 1e-6
N_GATES = 2 * H_M
GATE_PAD = 128
DIFF_COLS = 3 * DIFF_W
MLSTM_COLS = 4 * MLSTM_W
MASK_VALUE = -1e30
LOG2E = math.log2(math.e)

VMEM_LIMIT_BYTES = 56 * 1024 * 1024

BF16 = jnp.bfloat16
F32 = jnp.float32


def _dot(a, b):
    return jnp.dot(a, b, preferred_element_type=F32)


def _dot_nt(a, b):
    return lax.dot_general(a, b, (((1,), (1,)), ((), ())), preferred_element_type=F32)


def _dot_tn(a, b):
    return lax.dot_general(a, b, (((0,), (0,)), ((), ())), preferred_element_type=F32)


def _params(*semantics):
    return pltpu.CompilerParams(dimension_semantics=semantics, vmem_limit_bytes=VMEM_LIMIT_BYTES)


def _const_spec(shape):
    n = len(shape)
    return pl.BlockSpec(shape, lambda *_: (0,) * n)


def _resident_spec(shape):
    n = len(shape)
    return pl.BlockSpec(shape, lambda *_: (0,) * n, pipeline_mode=pl.Buffered(1))


def _bucket_np(dist):
    n = np.maximum(dist, 0)
    max_exact = N_BUCKETS // 2
    nf = np.maximum(n, 1).astype(np.float32)
    large = max_exact + (np.log(nf / np.float32(max_exact)) / np.float32(math.log(MAX_DIST / max_exact))
                         * np.float32(N_BUCKETS - max_exact)).astype(np.int32)
    large = np.minimum(large, N_BUCKETS - 1)
    out = np.where(n < max_exact, n, large).astype(np.int32)
    return np.where(dist < 0, -1, out).astype(np.int32)


def _bias_kernel(rb_ref, pb_ref, sb_ref, pbias_ref, sbias_ref):
    h = pl.program_id(0)

    def lookup(bk, shift, scale):
        acc = jnp.full(bk.shape, MASK_VALUE, F32)
        for b in range(N_BUCKETS):
            acc = jnp.where(bk == b, (rb_ref[b, h] - shift) * scale, acc)
        return acc

    for t in range(pb_ref.shape[0]):
        pbias_ref[0, t] = lookup(pb_ref[t], rb_ref[N_BUCKETS - 1, h], LOG2E)
    sbias_ref[0] = lookup(sb_ref[...], 0.0, 1.0)


def _bias_tables(rel_bias, blk, past, dec_seq):
    k = np.arange(blk)[:, None]
    q = np.arange(blk)[None, :]
    pb = np.stack([_bucket_np(blk + q - k), _bucket_np(q - k)], 0)
    far = N_BUCKETS - 1
    assert (_bucket_np(np.arange(MAX_DIST, 4 * past + 4 * blk)) == far).all()
    assert blk >= MAX_DIST and PAGE_SIZE >= MAX_DIST
    qpos = past + np.arange(dec_seq)[:, None]
    last = _bucket_np(qpos - (past - PAGE_SIZE + np.arange(PAGE_SIZE))[None, :])
    new = _bucket_np(qpos - (past + np.arange(PAGE_SIZE))[None, :])
    new[:, dec_seq:] = -1
    sb = np.concatenate([np.full((dec_seq, PAGE_SIZE), far, np.int32), last, new], 1)
    sb = np.concatenate([sb, sb], 0)
    return pl.pallas_call(
        _bias_kernel,
        grid=(H_D,),
        in_specs=[pl.BlockSpec(memory_space=pltpu.SMEM),
                  _const_spec(pb.shape), _const_spec(sb.shape)],
        out_specs=[pl.BlockSpec((1,) + pb.shape, lambda h: (h, 0, 0, 0)),
                   pl.BlockSpec((1,) + sb.shape, lambda h: (h, 0, 0))],
        out_shape=[jax.ShapeDtypeStruct((H_D,) + pb.shape, F32),
                   jax.ShapeDtypeStruct((H_D,) + sb.shape, F32)],
        compiler_params=_params("arbitrary"),
        name="bias_tables",
    )(rel_bias, jnp.asarray(pb), jnp.asarray(sb))


Q_SCALE = DH_HALF ** -0.5
K_SCALE = DH_M ** -0.5


def _head_rows(ref, h, n_tokens):
    return ref[pl.ds(h, n_tokens, stride=H_D), :]


def _store_head_rows(ref, x):
    for h in range(H_D):
        ref[pl.ds(h, x.shape[0], stride=H_D), :] = x[:, h * DK_D:(h + 1) * DK_D]


def _load_head_rows(ref, n_tokens):
    return jnp.concatenate([_head_rows(ref, h, n_tokens) for h in range(H_D)], axis=1)


def _in_proj_prompt_kernel(x_ref, wd_ref, wm_ref, wg_ref,
                           qt_ref, k32_ref, k16_ref, v32_ref, vt_ref,
                           qm_ref, km_ref, vm_ref, om_ref, g_ref):
    x = x_ref[...].astype(BF16)
    r = _dot(x, wd_ref[:, 0:DIFF_W]) * (Q_SCALE * LOG2E)
    for h in range(H_D):
        qt_ref[h] = r[:, h * DK_D:(h + 1) * DK_D].T.astype(BF16)
    r = _dot(x, wd_ref[:, DIFF_W:2 * DIFF_W])
    _store_head_rows(k32_ref, r)
    for h in range(H_D):
        k16_ref[h] = r[:, h * DK_D:(h + 1) * DK_D].astype(BF16)
    r = _dot(x, wd_ref[:, 2 * DIFF_W:3 * DIFF_W])
    _store_head_rows(v32_ref, r)
    for h in range(H_D):
        vt_ref[h] = r[:, h * DV_D:(h + 1) * DV_D].T.astype(BF16)
    qm_ref[...] = _dot(x, wm_ref[:, 0:MLSTM_W]).astype(BF16)
    km_ref[...] = (_dot(x, wm_ref[:, MLSTM_W:2 * MLSTM_W]) * K_SCALE).astype(BF16)
    vm_ref[...] = _dot(x, wm_ref[:, 2 * MLSTM_W:3 * MLSTM_W]).astype(BF16)
    om_ref[...] = _dot(x, wm_ref[:, 3 * MLSTM_W:4 * MLSTM_W]).astype(BF16)
    g_ref[...] = _dot(x, wg_ref[...])


def _in_proj_sample_kernel(x_ref, wd_ref, wm_ref, wg_ref,
                           q_ref, k_ref, v_ref, qm_ref, km_ref, vm_ref, om_ref, g_ref):
    x = x_ref[...].astype(BF16)
    q_ref[...] = _dot(x, wd_ref[:, 0:DIFF_W]) * Q_SCALE
    _store_head_rows(k_ref, _dot(x, wd_ref[:, DIFF_W:2 * DIFF_W]))
    _store_head_rows(v_ref, _dot(x, wd_ref[:, 2 * DIFF_W:3 * DIFF_W]))
    qm_ref[...] = _dot(x, wm_ref[:, 0:MLSTM_W])
    km_ref[...] = _dot(x, wm_ref[:, MLSTM_W:2 * MLSTM_W]) * K_SCALE
    vm_ref[...] = _dot(x, wm_ref[:, 2 * MLSTM_W:3 * MLSTM_W])
    om_ref[...] = _dot(x, wm_ref[:, 3 * MLSTM_W:4 * MLSTM_W])
    g_ref[...] = _dot(x, wg_ref[...])


def _in_proj(x, wd, wm, wg, tm, prompt):
    t = x.shape[0]
    assert t % tm == 0
    row = lambda w: pl.BlockSpec((tm, w), lambda i: (i, 0))
    heads = pl.BlockSpec((H_D, tm, DK_D), lambda i: (0, i, 0))
    cache = pl.BlockSpec((tm * H_D, DK_D), lambda i: (i, 0))
    sds = jax.ShapeDtypeStruct
    if prompt:
        body = _in_proj_prompt_kernel
        heads_t = pl.BlockSpec((H_D, DK_D, tm), lambda i: (0, 0, i))
        out_specs = [heads_t, cache, heads, cache, heads_t] + [row(MLSTM_W)] * 4 + [row(GATE_PAD)]
        out_shape = [sds((H_D, DK_D, t), BF16), sds((t * H_D, DK_D), F32), sds((H_D, t, DK_D), BF16),
                     sds((t * H_D, DV_D), F32), sds((H_D, DV_D, t), BF16)]
        out_shape += [sds((t, MLSTM_W), BF16)] * 4 + [sds((t, GATE_PAD), F32)]
    else:
        body = _in_proj_sample_kernel
        out_specs = [row(DIFF_W), cache, cache] + [row(MLSTM_W)] * 4 + [row(GATE_PAD)]
        out_shape = ([sds((t, DIFF_W), F32)] + [sds((t * H_D, DK_D), F32)] * 2 + [sds((t, MLSTM_W), F32)] * 4
                     + [sds((t, GATE_PAD), F32)])
    return pl.pallas_call(
        body,
        grid=(t // tm,),
        in_specs=[row(D_MODEL), _resident_spec(wd.shape), _resident_spec(wm.shape), _resident_spec(wg.shape)],
        out_specs=out_specs,
        out_shape=out_shape,
        compiler_params=_params("parallel"),
        name="in_proj_prompt" if prompt else "in_proj_sample",
    )(x, wd, wm, wg)


def _lambda_value(lam_ref, lam_init):
    lp = lam_ref[...]
    e1 = jnp.exp(jnp.sum(lp[0:1] * lp[1:2], axis=1, keepdims=True))
    e2 = jnp.exp(jnp.sum(lp[2:3] * lp[3:4], axis=1, keepdims=True))
    return e1 - e2 + lam_init


def _head_rms(o, w):
    return o * lax.rsqrt(jnp.mean(o * o, axis=-1, keepdims=True) + RMS_EPS) * w


def _layer_norm(x, g, b):
    xc = x - jnp.mean(x, axis=-1, keepdims=True)
    var = jnp.mean(xc * xc, axis=-1, keepdims=True)
    return xc * lax.rsqrt(var + LN_EPS) * g + b


def _attn_prompt_kernel(lam_init, tq, nq, group, qt_ref, k_ref, vt_ref, pbias_ref, lam_ref, nw_ref, o_ref):
    i = pl.program_id(1)
    lam = _lambda_value(lam_ref, lam_init)
    chan = lax.broadcasted_iota(jnp.int32, (DK_D, tq), 0)
    first_map = chan < DH_HALF

    def fold8(op, s):
        return op(s.reshape(s.shape[0] // 8, 8, s.shape[1]), axis=0)

    def process(ii):
        n = (ii + 1) * tq
        n_far = max(n - 2 * tq, 0)

        def head(h):
            qt = qt_ref[h]
            zero = jnp.zeros_like(qt)
            q2t = jnp.concatenate([jnp.where(first_map, qt, zero), jnp.where(first_map, zero, qt)], axis=1)
            parts = []
            if n_far:
                parts.append(_dot(k_ref[h, 0:n_far, :], q2t))
            if ii >= 1:
                bs = pbias_ref[h, 0]
                parts.append(_dot(k_ref[h, n - 2 * tq:n - tq, :], q2t) + jnp.concatenate([bs, bs], axis=1))
            bd = pbias_ref[h, 1]
            parts.append(_dot(k_ref[h, n - tq:n, :], q2t) + jnp.concatenate([bd, bd], axis=1))
            m8 = fold8(jnp.max, parts[0])
            for s in parts[1:]:
                m8 = jnp.maximum(m8, fold8(jnp.max, s))
            m = jnp.max(m8, axis=0, keepdims=True)
            probs = [jnp.exp2(s - m) for s in parts]
            l8 = fold8(jnp.sum, probs[0])
            for p in probs[1:]:
                l8 = l8 + fold8(jnp.sum, p)
            l = jnp.sum(l8, axis=0, keepdims=True)
            p_all = jnp.concatenate([p.astype(BF16) for p in probs], axis=0)
            o = _dot(vt_ref[h, :, 0:n], p_all) / l
            o = o[:, :tq] - lam * o[:, tq:]
            o = o * lax.rsqrt(jnp.mean(o * o, axis=0, keepdims=True) + RMS_EPS) * (nw_ref[...] * (1.0 - lam_init))
            o_ref[h] = o.T.astype(o_ref.dtype)

        def head_group(g, carry):
            for u in range(group):
                head(g * group + u)
            return carry

        lax.fori_loop(0, H_D // group, head_group, 0)

    for ii in range(nq):
        pl.when(i == ii)(functools.partial(process, ii))


def _attn_prompt(qt, k, vt, pbias, lam_params, norm_w, batch, seq, blk, group, lam_init):
    nq = seq // blk
    assert H_D % group == 0
    return pl.pallas_call(
        functools.partial(_attn_prompt_kernel, lam_init, blk, nq, group),
        grid=(batch, nq),
        in_specs=[pl.BlockSpec((H_D, DK_D, blk), lambda b, i: (0, 0, b * nq + i)),
                  pl.BlockSpec((H_D, seq, DK_D), lambda b, i: (0, b, 0)),
                  pl.BlockSpec((H_D, DV_D, seq), lambda b, i: (0, 0, b)),
                  _const_spec(pbias.shape),
                  _const_spec(lam_params.shape),
                  _const_spec(norm_w.shape)],
        out_specs=pl.BlockSpec((H_D, blk, DV_D), lambda b, i: (0, b * nq + i, 0)),
        out_shape=jax.ShapeDtypeStruct((H_D, batch * seq, DV_D), BF16),
        compiler_params=_params("parallel", "parallel"),
        name="attn_prompt",
    )(qt, k, vt, pbias, lam_params, norm_w)


def _attn_sample_kernel(lam_init, n_pages, ls, pt_ref, q_ref, kn_ref, vn_ref, sbias_ref, lam_ref, nw_ref, *rest):
    del pt_ref
    k_pages = rest[:n_pages]
    v_pages = rest[n_pages:2 * n_pages]
    o_ref = rest[2 * n_pages]
    rows = 2 * ls * H_D
    lam = _lambda_value(lam_ref, lam_init)

    q = q_ref[...]
    qt = jnp.concatenate([q] * (2 * H_D), axis=0)
    rowi = lax.broadcasted_iota(jnp.int32, (rows, DIFF_W), 0)
    coli = lax.broadcasted_iota(jnp.int32, (rows, DIFF_W), 1)
    q_bd = jnp.where(coli // DH_HALF == rowi // ls, qt, 0.0).astype(BF16)

    pad = jnp.zeros((PAGE_SIZE - ls, DIFF_W), F32)

    def page(ref):
        return _load_head_rows(ref, PAGE_SIZE).astype(BF16)

    k_new = jnp.concatenate([_load_head_rows(kn_ref, ls), pad], axis=0).astype(BF16)
    v_new = jnp.concatenate([_load_head_rows(vn_ref, ls), pad], axis=0).astype(BF16)

    sb = sbias_ref[...]
    bias_far, bias_last, bias_new = sb[:, 0:128], sb[:, 128:256], sb[:, 256:384]
    scores = []
    for p in range(n_pages):
        kp = page(k_pages[p])
        scores.append(_dot_nt(q_bd, kp) + (bias_last if p == n_pages - 1 else bias_far))
    scores.append(_dot_nt(q_bd, k_new) + bias_new)

    m = scores[0]
    for s in scores[1:]:
        m = jnp.maximum(m, s)
    m = jnp.max(m, axis=1, keepdims=True)
    probs = [jnp.exp(s - m) for s in scores]
    l = probs[0]
    for p in probs[1:]:
        l = l + p
    l = jnp.sum(l, axis=1, keepdims=True)
    second_map = (lax.broadcasted_iota(jnp.int32, (rows, 1), 0) // ls) % 2 == 1
    fac = jnp.where(second_map, -lam, 1.0) / l

    acc = jnp.zeros((H_D * ls, DIFF_W), F32)
    for p in range(n_pages + 1):
        p3 = (probs[p] * fac).reshape(H_D, 2 * ls, PAGE_SIZE)
        w = (p3[:, :ls] + p3[:, ls:]).reshape(H_D * ls, PAGE_SIZE).astype(BF16)
        vp = v_new if p == n_pages else page(v_pages[p])
        acc = acc + _dot(w, vp)
    for h in range(H_D):
        o = acc[h * ls:(h + 1) * ls, h * DV_D:(h + 1) * DV_D]
        o_ref[:, h * DV_D:(h + 1) * DV_D] = _head_rms(o, nw_ref[...]) * (1.0 - lam_init)


def _attn_sample(page_table, q, k_new, v_new, cache_k, cache_v, sbias, lam_params, norm_w, ls, lam_init):
    bs, n_pages = page_table.shape
    seq_spec = pl.BlockSpec((ls, DIFF_W), lambda b, pt: (b, 0))
    new_spec = pl.BlockSpec((ls * H_D, DK_D), lambda b, pt: (b, 0))
    rows_per_page = PAGE_SIZE * H_D
    cache_k = cache_k.reshape(-1, DK_D)
    cache_v = cache_v.reshape(-1, DV_D)

    def page_spec(p):
        return pl.BlockSpec((rows_per_page, DK_D), lambda b, pt: (pt[b, p], 0))

    grid_spec = pltpu.PrefetchScalarGridSpec(
        num_scalar_prefetch=1,
        grid=(bs,),
        in_specs=[seq_spec, new_spec, new_spec,
                  pl.BlockSpec(sbias.shape, lambda b, pt: (0, 0)),
                  pl.BlockSpec(lam_params.shape, lambda b, pt: (0, 0)),
                  pl.BlockSpec(norm_w.shape, lambda b, pt: (0, 0))]
        + [page_spec(p) for p in range(n_pages)] * 2,
        out_specs=seq_spec,
    )
    return pl.pallas_call(
        functools.partial(_attn_sample_kernel, lam_init, n_pages, ls),
        grid_spec=grid_spec,
        out_shape=jax.ShapeDtypeStruct((bs * ls, DIFF_W), F32),
        compiler_params=_params("parallel"),
        name="attn_sample",
    )(page_table, q, k_new, v_new, sbias, lam_params, norm_w,
      *([cache_k] * n_pages), *([cache_v] * n_pages))


def _mlstm_chunk(q, k, v, o_pre, i_pre, f_pre, c_prev, n_prev, m_prev, norm_w):
    L = q.shape[0]
    t_idx = lax.broadcasted_iota(jnp.int32, (L, L), 0)
    s_idx = lax.broadcasted_iota(jnp.int32, (L, L), 1)
    causal = s_idx <= t_idx
    eye = s_idx == t_idx

    def to_row(col):
        return jnp.sum(jnp.where(eye, col, 0.0), axis=0, keepdims=True)

    it_col = i_pre
    lf_col = jax.nn.log_sigmoid(f_pre)
    it_row = to_row(it_col)
    b_row = jnp.sum(jnp.where(t_idx <= s_idx, lf_col, 0.0), axis=0, keepdims=True)
    b_col = jnp.sum(jnp.where(causal, to_row(lf_col), 0.0), axis=1, keepdims=True)

    log_d = jnp.where(causal, b_col - b_row + it_row, -jnp.inf)
    m_t = jnp.maximum(b_col + m_prev, jnp.max(log_d, axis=1, keepdims=True))
    dmat = jnp.exp(log_d - m_t)
    inter = jnp.exp(b_col + m_prev - m_t)

    w = _dot_nt(q, k) * dmat
    kf = k.astype(F32)
    num = inter * _dot_nt(q, c_prev.astype(BF16)) + _dot(w.astype(BF16), v)
    den = inter * jnp.sum(q.astype(F32) * n_prev, axis=1, keepdims=True) + jnp.sum(w, axis=1, keepdims=True)
    h = num / jnp.maximum(jnp.abs(den), jnp.exp(-m_t))
    h = _head_rms(h, norm_w) * jax.nn.sigmoid(o_pre.astype(F32))

    m_new = m_t[L - 1:L]
    b_last = b_col[L - 1:L]
    g = jnp.exp(b_last - b_col + it_col - m_new)
    decay = jnp.exp(b_last + m_prev - m_new)
    gk = g * kf
    c_new = decay * c_prev + _dot_tn(v, gk.astype(BF16))
    n_new = decay * n_prev + jnp.sum(gk, axis=0, keepdims=True)
    return h, c_new, n_new, m_new


def _mlstm_prompt_kernel(bg_ref, q_ref, k_ref, v_ref, o_ref, g_ref, nw_ref, h_ref, c_ref, n_ref, m_ref):
    @pl.when(pl.program_id(1) == 0)
    def _():
        c_ref[...] = jnp.zeros_like(c_ref)
        n_ref[...] = jnp.zeros_like(n_ref)
        m_ref[...] = jnp.zeros_like(m_ref)

    for h in range(H_M):
        cols = slice(h * DH_M, (h + 1) * DH_M)
        out, c_new, n_new, m_new = _mlstm_chunk(
            q_ref[:, cols], k_ref[:, cols], v_ref[:, cols], o_ref[:, cols],
            g_ref[:, h:h + 1] + bg_ref[h], g_ref[:, H_M + h:H_M + h + 1] + bg_ref[H_M + h],
            c_ref[0, h], n_ref[0, h:h + 1, :], m_ref[0, h:h + 1, 0:1], nw_ref[:, cols])
        h_ref[:, cols] = out.astype(h_ref.dtype)
        c_ref[0, h] = c_new
        n_ref[0, h:h + 1, :] = n_new
        m_ref[0, h:h + 1, :] = jnp.broadcast_to(m_new, (1, m_ref.shape[2]))


def _mlstm_prompt(b_gates, qm, km, vm, om, gates, norm_w, batch, seq, chunk):
    nc = seq // chunk
    tok = lambda w: pl.BlockSpec((chunk, w), lambda b, c: (b * nc + c, 0))
    sds = jax.ShapeDtypeStruct
    return pl.pallas_call(
        _mlstm_prompt_kernel,
        grid=(batch, nc),
        in_specs=[pl.BlockSpec(memory_space=pltpu.SMEM),
                  tok(MLSTM_W), tok(MLSTM_W), tok(MLSTM_W), tok(MLSTM_W), tok(GATE_PAD),
                  _const_spec(norm_w.shape)],
        out_specs=[tok(MLSTM_W),
                   pl.BlockSpec((1, H_M, DH_M, DH_M), lambda b, c: (b, 0, 0, 0)),
                   pl.BlockSpec((1, H_M, DH_M), lambda b, c: (b, 0, 0)),
                   pl.BlockSpec((1, H_M, 128), lambda b, c: (b, 0, 0))],
        out_shape=[sds((batch * seq, MLSTM_W), BF16), sds((batch, H_M, DH_M, DH_M), F32),
                   sds((batch, H_M, DH_M), F32), sds((batch, H_M, 128), F32)],
        compiler_params=_params("parallel", "arbitrary"),
        name="mlstm_prompt",
    )(b_gates, qm, km, vm, om, gates, norm_w)


def _mlstm_sample_kernel(ls, group, bg_ref, q_ref, k_ref, v_ref, o_ref, g_ref, nw_ref, c_in, n_in, m_in,
                         h_ref, c_ref, n_ref, m_ref):
    def seq(s, carry):
        rows = pl.ds(pl.multiple_of(s * ls, ls), ls)
        for h in range(H_M):
            cols = slice(h * DH_M, (h + 1) * DH_M)
            out, c_new, n_new, m_new = _mlstm_chunk(
                q_ref[rows, cols].astype(BF16), k_ref[rows, cols].astype(BF16), v_ref[rows, cols].astype(BF16),
                o_ref[rows, cols],
                g_ref[rows, h:h + 1] + bg_ref[h], g_ref[rows, H_M + h:H_M + h + 1] + bg_ref[H_M + h],
                c_in[s, h], n_in[s, h:h + 1, :], m_in[s, h:h + 1, 0:1], nw_ref[:, cols])
            h_ref[rows, cols] = out
            c_ref[s, h] = c_new
            n_ref[s, h:h + 1, :] = n_new
            m_ref[s, h:h + 1, :] = jnp.broadcast_to(m_new, (1, m_ref.shape[2]))
        return carry

    lax.fori_loop(0, group, seq, 0)


def _mlstm_sample(b_gates, qm, km, vm, om, gates, norm_w, state_c, state_n, state_m, ls, group):
    bs = state_c.shape[0]
    assert bs % group == 0
    tok = lambda w: pl.BlockSpec((group * ls, w), lambda i: (i, 0))
    c_spec = pl.BlockSpec((group, H_M, DH_M, DH_M), lambda i: (i, 0, 0, 0))
    n_spec = pl.BlockSpec((group, H_M, DH_M), lambda i: (i, 0, 0))
    m_spec = pl.BlockSpec((group, H_M, 128), lambda i: (i, 0, 0))
    sds = jax.ShapeDtypeStruct
    return pl.pallas_call(
        functools.partial(_mlstm_sample_kernel, ls, group),
        grid=(bs // group,),
        in_specs=[pl.BlockSpec(memory_space=pltpu.SMEM),
                  tok(MLSTM_W), tok(MLSTM_W), tok(MLSTM_W), tok(MLSTM_W), tok(GATE_PAD),
                  _const_spec(norm_w.shape), c_spec, n_spec, m_spec],
        out_specs=[tok(MLSTM_W), c_spec, n_spec, m_spec],
        out_shape=[sds((bs * ls, MLSTM_W), F32), sds(state_c.shape, F32),
                   sds(state_n.shape, F32), sds((bs, H_M, 128), F32)],
        compiler_params=_params("parallel"),
        name="mlstm_sample",
    )(b_gates, qm, km, vm, om, gates, norm_w, state_c, state_n, state_m)


def _mix_kernel(head_major, x_ref, hm_ref, hd_ref, w_ref, g_ref, b_ref, y_ref):
    if head_major:
        hd = jnp.concatenate([hd_ref[h] for h in range(H_D)], axis=-1)
    else:
        hd = hd_ref[...]
    mix = _dot(hm_ref[...].astype(BF16), w_ref[0:MLSTM_W, :]) + _dot(hd.astype(BF16), w_ref[MLSTM_W:, :])
    y_ref[...] = _layer_norm(ALPHA * x_ref[...] + mix, g_ref[...], b_ref[...])


def _mix(x, hm, hd, w_out, ln_g, ln_b, tm, head_major):
    t = x.shape[0]
    assert t % tm == 0
    row = lambda w: pl.BlockSpec((tm, w), lambda i: (i, 0))
    hd_spec = pl.BlockSpec((H_D, tm, DV_D), lambda i: (0, i, 0)) if head_major else row(DIFF_W)
    return pl.pallas_call(
        functools.partial(_mix_kernel, head_major),
        grid=(t // tm,),
        in_specs=[row(D_MODEL), row(MLSTM_W), hd_spec, _resident_spec(w_out.shape),
                  _const_spec(ln_g.shape), _const_spec(ln_b.shape)],
        out_specs=row(D_MODEL),
        out_shape=jax.ShapeDtypeStruct((t, D_MODEL), F32),
        compiler_params=_params("parallel"),
        name="mix_ln",
    )(x, hm, hd, w_out, ln_g, ln_b)


def _ffn_kernel(x_ref, wg_ref, wu_ref, wd_ref, g_ref, b_ref, y_ref, xb_ref, acc_ref):
    j = pl.program_id(1)

    @pl.when(j == 0)
    def _():
        xb_ref[...] = x_ref[...].astype(BF16)
        acc_ref[...] = jnp.zeros_like(acc_ref)

    xb = xb_ref[...]
    gate = _dot(xb, wg_ref[...])
    up = _dot(xb, wu_ref[...])
    acc_ref[...] += _dot((jax.nn.silu(gate) * up).astype(BF16), wd_ref[...])

    @pl.when(j == pl.num_programs(1) - 1)
    def _():
        y_ref[...] = _layer_norm(ALPHA * x_ref[...] + acc_ref[...], g_ref[...], b_ref[...])


def _ffn(x, w_gate, w_up, w_down, ln_g, ln_b, tm, tf):
    t = x.shape[0]
    assert t % tm == 0 and D_FF % tf == 0
    row = pl.BlockSpec((tm, D_MODEL), lambda i, j: (i, 0))
    return pl.pallas_call(
        _ffn_kernel,
        grid=(t // tm, D_FF // tf),
        in_specs=[row,
                  pl.BlockSpec((D_MODEL, tf), lambda i, j: (0, j)),
                  pl.BlockSpec((D_MODEL, tf), lambda i, j: (0, j)),
                  pl.BlockSpec((tf, D_MODEL), lambda i, j: (j, 0)),
                  pl.BlockSpec(ln_g.shape, lambda i, j: (0, 0)),
                  pl.BlockSpec(ln_b.shape, lambda i, j: (0, 0))],
        out_specs=row,
        out_shape=jax.ShapeDtypeStruct((t, D_MODEL), F32),
        scratch_shapes=[pltpu.VMEM((tm, D_MODEL), BF16), pltpu.VMEM((tm, D_MODEL), F32)],
        compiler_params=_params("parallel", "arbitrary"),
        name="ffn_ln",
    )(x, w_gate, w_up, w_down, ln_g, ln_b)


def _tiles(seq):
    return dict(
        proj_tm=256,
        attn_blk=min(256, seq),
        attn_group=2,
        mlstm_chunk=min(256, seq),
        sample_group=4,
        mix_tm=256,
        ffn_tm=512,
        ffn_tf=512,
    )


def kernel(x_prompt, x_sample, cache_k, cache_v, state_C, state_n, state_m, page_table, rel_bias, w_in, b_gates, lambda_q1, lambda_k1, lambda_q2, lambda_k2, diff_norm_w, mlstm_norm_w, w_out, ln1_g, ln1_b, w_gate, w_up, w_down, ln2_g, ln2_b):
    B, S, _ = x_prompt.shape
    Bs, Ls, _ = x_sample.shape
    n_pages = page_table.shape[1]
    past = n_pages * cache_k.shape[2]
    assert w_in.shape[0] == DEPTH == 1 and cache_k.shape[2] == PAGE_SIZE
    tl = _tiles(S)
    l = 0
    lam_init = 0.8 - 0.6 * math.exp(-0.3 * l)

    w = w_in[l]
    wd = w[:, :DIFF_COLS].astype(BF16)
    wm = w[:, DIFF_COLS:DIFF_COLS + MLSTM_COLS].astype(BF16)
    wg = jnp.pad(w[:, DIFF_COLS + MLSTM_COLS:], ((0, 0), (0, GATE_PAD - N_GATES))).astype(BF16)
    w_o = w_out[l].astype(BF16)
    w_g, w_u, w_d = w_gate[l].astype(BF16), w_up[l].astype(BF16), w_down[l].astype(BF16)
    lam_params = jnp.stack([lambda_q1[l], lambda_k1[l], lambda_q2[l], lambda_k2[l]], 0)
    dnw = diff_norm_w[l].reshape(1, DV_D)
    mnw = mlstm_norm_w[l].reshape(1, MLSTM_W)
    g1, b1 = ln1_g[l].reshape(1, D_MODEL), ln1_b[l].reshape(1, D_MODEL)
    g2, b2 = ln2_g[l].reshape(1, D_MODEL), ln2_b[l].reshape(1, D_MODEL)
    bg = b_gates[l]

    pbias, sbias = _bias_tables(rel_bias, tl["attn_blk"], past, Ls)
    sbias = sbias.reshape(H_D * 2 * Ls, 3 * 128)

    xp = x_prompt.reshape(B * S, D_MODEL)
    qt, k32, k16, v32, vt, qm, km, vm, om, gates = _in_proj(xp, wd, wm, wg, tl["proj_tm"], True)
    hd = _attn_prompt(qt, k16, vt, pbias, lam_params, dnw.reshape(DV_D, 1), B, S, tl["attn_blk"],
                      tl["attn_group"], lam_init)
    hm, c_p, n_p, m_p = _mlstm_prompt(bg, qm, km, vm, om, gates, mnw, B, S, tl["mlstm_chunk"])
    x1 = _mix(xp, hm, hd, w_o, g1, b1, tl["mix_tm"], True)
    y_p = _ffn(x1, w_g, w_u, w_d, g2, b2, tl["ffn_tm"], tl["ffn_tf"])

    xs = x_sample.reshape(Bs * Ls, D_MODEL)
    qs, ks, vs, qms, kms, vms, oms, gates_s = _in_proj(xs, wd, wm, wg, tl["proj_tm"], False)
    hd_s = _attn_sample(page_table, qs, ks, vs, cache_k, cache_v, sbias, lam_params, dnw, Ls, lam_init)
    m_in = jnp.broadcast_to(state_m[l][:, :, None], (Bs, H_M, 128))
    hm_s, c_s, n_s, m_s = _mlstm_sample(bg, qms, kms, vms, oms, gates_s, mnw,
                                        state_C[l], state_n[l], m_in, Ls, tl["sample_group"])
    x1s = _mix(xs, hm_s, hd_s, w_o, g1, b1, tl["mix_tm"], False)
    y_s = _ffn(x1s, w_g, w_u, w_d, g2, b2, tl["ffn_tm"], tl["ffn_tf"])

    return (y_p.reshape(B, S, D_MODEL), y_s.reshape(Bs, Ls, D_MODEL),
            k32.reshape(1, B, S, H_D, DK_D), v32.reshape(1, B, S, H_D, DV_D),
            c_p[None], n_p[None], m_p[None, :, :, 0],
            ks.reshape(1, Bs, Ls, H_D, DK_D), vs.reshape(1, Bs, Ls, H_D, DV_D),
            c_s[None], n_s[None], m_s[None, :, :, 0])
```

```python
import functools
import math

import numpy as np
import jax
import jax.numpy as jnp
from jax import lax
from jax.experimental import pallas as pl
from jax.experimental.pallas import tpu as pltpu

D_MODEL = 2048
DEPTH = 1
PAGE_SIZE = 128
H_D = 8
DH_HALF = 64
DK_D = 2 * DH_HALF
DV_D = 2 * DH_HALF
DIFF_W = H_D * DV_D
H_M = 4
DH_M = 256
MLSTM_W = H_M * DH_M
D_FF = -(-8 * D_MODEL // (3 * 256)) * 256
N_BUCKETS = 32
MAX_DIST = 128
ALPHA = (2 * DEPTH) ** 0.25
LN_EPS = 1e-5
RMS_EPS = 1e-6
N_GATES = 2 * H_M
GATE_PAD = 128
DIFF_COLS = 3 * DIFF_W
MLSTM_COLS = 4 * MLSTM_W
MASK_VALUE = -1e30
LOG2E = math.log2(math.e)

VMEM_LIMIT_BYTES = 56 * 1024 * 1024

BF16 = jnp.bfloat16
F32 = jnp.float32


def _dot(a, b):
    return jnp.dot(a, b, preferred_element_type=F32)


def _dot_nt(a, b):
    return lax.dot_general(a, b, (((1,), (1,)), ((), ())), preferred_element_type=F32)


def _dot_tn(a, b):
    return lax.dot_general(a, b, (((0,), (0,)), ((), ())), preferred_element_type=F32)


def _params(*semantics):
    return pltpu.CompilerParams(dimension_semantics=semantics, vmem_limit_bytes=VMEM_LIMIT_BYTES)


def _const_spec(shape):
    n = len(shape)
    return pl.BlockSpec(shape, lambda *_: (0,) * n)


def _resident_spec(shape):
    n = len(shape)
    return pl.BlockSpec(shape, lambda *_: (0,) * n, pipeline_mode=pl.Buffered(1))


def _bucket_np(dist):
    n = np.maximum(dist, 0)
    max_exact = N_BUCKETS // 2
    nf = np.maximum(n, 1).astype(np.float32)
    large = max_exact + (np.log(nf / np.float32(max_exact)) / np.float32(math.log(MAX_DIST / max_exact))
                         * np.float32(N_BUCKETS - max_exact)).astype(np.int32)
    large = np.minimum(large, N_BUCKETS - 1)
    out = np.where(n < max_exact, n, large).astype(np.int32)
    return np.where(dist < 0, -1, out).astype(np.int32)


def _bias_kernel(rb_ref, pb_ref, sb_ref, pbias_ref, sbias_ref):
    h = pl.program_id(0)

    def lookup(bk, shift, scale):
        acc = jnp.full(bk.shape, MASK_VALUE, F32)
        for b in range(N_BUCKETS):
            acc = jnp.where(bk == b, (rb_ref[b, h] - shift) * scale, acc)
        return acc

    for t in range(pb_ref.shape[0]):
        pbias_ref[0, t] = lookup(pb_ref[t], rb_ref[N_BUCKETS - 1, h], LOG2E)
    sbias_ref[0] = lookup(sb_ref[...], 0.0, 1.0)


def _bias_tables(rel_bias, blk, past, dec_seq):
    k = np.arange(blk)[:, None]
    q = np.arange(blk)[None, :]
    pb = np.stack([_bucket_np(blk + q - k), _bucket_np(q - k)], 0)
    far = N_BUCKETS - 1
    assert (_bucket_np(np.arange(MAX_DIST, 4 * past + 4 * blk)) == far).all()
    assert blk >= MAX_DIST and PAGE_SIZE >= MAX_DIST
    qpos = past + np.arange(dec_seq)[:, None]
    last = _bucket_np(qpos - (past - PAGE_SIZE + np.arange(PAGE_SIZE))[None, :])
    new = _bucket_np(qpos - (past + np.arange(PAGE_SIZE))[None, :])
    new[:, dec_seq:] = -1
    sb = np.concatenate([np.full((dec_seq, past - PAGE_SIZE), far, np.int32), last, new], 1)
    sb = np.concatenate([sb, sb], 0)
    return pl.pallas_call(
        _bias_kernel,
        grid=(H_D,),
        in_specs=[pl.BlockSpec(memory_space=pltpu.SMEM),
                  _const_spec(pb.shape), _const_spec(sb.shape)],
        out_specs=[pl.BlockSpec((1,) + pb.shape, lambda h: (h, 0, 0, 0)),
                   pl.BlockSpec((1,) + sb.shape, lambda h: (h, 0, 0))],
        out_shape=[jax.ShapeDtypeStruct((H_D,) + pb.shape, F32),
                   jax.ShapeDtypeStruct((H_D,) + sb.shape, F32)],
        compiler_params=_params("arbitrary"),
        name="bias_tables",
    )(rel_bias, jnp.asarray(pb), jnp.asarray(sb))


Q_SCALE = DH_HALF ** -0.5
K_SCALE = DH_M ** -0.5


def _head_rows(ref, h, n_tokens):
    return ref[pl.ds(h, n_tokens, stride=H_D), :]


def _store_head_rows(ref, x):
    for h in range(H_D):
        ref[pl.ds(h, x.shape[0], stride=H_D), :] = x[:, h * DK_D:(h + 1) * DK_D]


def _load_head_rows(ref, n_tokens):
    return jnp.concatenate([_head_rows(ref, h, n_tokens) for h in range(H_D)], axis=1)


def _split_w_in_kernel(w_ref, wd_ref, wm_ref):
    wd_ref[...] = w_ref[:, 0:DIFF_COLS].astype(BF16)
    wm_ref[...] = w_ref[:, DIFF_COLS:DIFF_COLS + MLSTM_COLS].astype(BF16)


def _split_w_in(w, rows):
    d, width = w.shape
    assert d % rows == 0
    return pl.pallas_call(
        _split_w_in_kernel,
        grid=(d // rows,),
        in_specs=[pl.BlockSpec((rows, width), lambda i: (i, 0))],
        out_specs=[pl.BlockSpec((rows, DIFF_COLS), lambda i: (i, 0)),
                   pl.BlockSpec((rows, MLSTM_COLS), lambda i: (i, 0))],
        out_shape=[jax.ShapeDtypeStruct((d, DIFF_COLS), BF16), jax.ShapeDtypeStruct((d, MLSTM_COLS), BF16)],
        compiler_params=_params("parallel"),
        name="split_w_in",
    )(w)


def _in_proj_prompt_kernel(x_ref, wd_ref, wm_ref, wg_ref,
                           qt_ref, k32_ref, k16_ref, v32_ref, vt_ref,
                           qm_ref, km_ref, vm_ref, om_ref, g_ref):
    x = x_ref[...].astype(BF16)
    r = _dot(x, wd_ref[:, 0:DIFF_W]) * (Q_SCALE * LOG2E)
    for h in range(H_D):
        qt_ref[h] = r[:, h * DK_D:(h + 1) * DK_D].T.astype(BF16)
    r = _dot(x, wd_ref[:, DIFF_W:2 * DIFF_W])
    _store_head_rows(k32_ref, r)
    for h in range(H_D):
        k16_ref[h] = r[:, h * DK_D:(h + 1) * DK_D].astype(BF16)
    r = _dot(x, wd_ref[:, 2 * DIFF_W:3 * DIFF_W])
    _store_head_rows(v32_ref, r)
    for h in range(H_D):
        vt_ref[h] = r[:, h * DV_D:(h + 1) * DV_D].T.astype(BF16)
    qm_ref[...] = _dot(x, wm_ref[:, 0:MLSTM_W]).astype(BF16)
    km_ref[...] = (_dot(x, wm_ref[:, MLSTM_W:2 * MLSTM_W]) * K_SCALE).astype(BF16)
    vm_ref[...] = _dot(x, wm_ref[:, 2 * MLSTM_W:3 * MLSTM_W]).astype(BF16)
    om_ref[...] = _dot(x, wm_ref[:, 3 * MLSTM_W:4 * MLSTM_W]).astype(BF16)
    g_ref[...] = _dot(x, wg_ref[...])


def _in_proj_sample_kernel(x_ref, wd_ref, wm_ref, wg_ref,
                           q_ref, k_ref, v_ref, qm_ref, km_ref, vm_ref, om_ref, g_ref):
    x = x_ref[...].astype(BF16)
    q_ref[...] = _dot(x, wd_ref[:, 0:DIFF_W]) * Q_SCALE
    _store_head_rows(k_ref, _dot(x, wd_ref[:, DIFF_W:2 * DIFF_W]))
    _store_head_rows(v_ref, _dot(x, wd_ref[:, 2 * DIFF_W:3 * DIFF_W]))
    qm_ref[...] = _dot(x, wm_ref[:, 0:MLSTM_W])
    km_ref[...] = _dot(x, wm_ref[:, MLSTM_W:2 * MLSTM_W]) * K_SCALE
    vm_ref[...] = _dot(x, wm_ref[:, 2 * MLSTM_W:3 * MLSTM_W])
    om_ref[...] = _dot(x, wm_ref[:, 3 * MLSTM_W:4 * MLSTM_W])
    g_ref[...] = _dot(x, wg_ref[...])


def _in_proj(x, wd, wm, wg, tm, prompt):
    t = x.shape[0]
    assert t % tm == 0
    row = lambda w: pl.BlockSpec((tm, w), lambda i: (i, 0))
    heads = pl.BlockSpec((H_D, tm, DK_D), lambda i: (0, i, 0))
    cache = pl.BlockSpec((tm * H_D, DK_D), lambda i: (i, 0))
    sds = jax.ShapeDtypeStruct
    if prompt:
        body = _in_proj_prompt_kernel
        heads_t = pl.BlockSpec((H_D, DK_D, tm), lambda i: (0, 0, i))
        out_specs = [heads_t, cache, heads, cache, heads_t] + [row(MLSTM_W)] * 4 + [row(GATE_PAD)]
        out_shape = [sds((H_D, DK_D, t), BF16), sds((t * H_D, DK_D), F32), sds((H_D, t, DK_D), BF16),
                     sds((t * H_D, DV_D), F32), sds((H_D, DV_D, t), BF16)]
        out_shape += [sds((t, MLSTM_W), BF16)] * 4 + [sds((t, GATE_PAD), F32)]
    else:
        body = _in_proj_sample_kernel
        out_specs = [row(DIFF_W), cache, cache] + [row(MLSTM_W)] * 4 + [row(GATE_PAD)]
        out_shape = ([sds((t, DIFF_W), F32)] + [sds((t * H_D, DK_D), F32)] * 2 + [sds((t, MLSTM_W), F32)] * 4
                     + [sds((t, GATE_PAD), F32)])
    return pl.pallas_call(
        body,
        grid=(t // tm,),
        in_specs=[row(D_MODEL), _resident_spec(wd.shape), _resident_spec(wm.shape), _resident_spec(wg.shape)],
        out_specs=out_specs,
        out_shape=out_shape,
        compiler_params=_params("parallel"),
        name="in_proj_prompt" if prompt else "in_proj_sample",
    )(x, wd, wm, wg)


def _lambda_value(lam_ref, lam_init):
    lp = lam_ref[...]
    e1 = jnp.exp(jnp.sum(lp[0:1] * lp[1:2], axis=1, keepdims=True))
    e2 = jnp.exp(jnp.sum(lp[2:3] * lp[3:4], axis=1, keepdims=True))
    return e1 - e2 + lam_init


def _head_rms(o, w):
    return o * lax.rsqrt(jnp.mean(o * o, axis=-1, keepdims=True) + RMS_EPS) * w


def _layer_norm(x, g, b):
    xc = x - jnp.mean(x, axis=-1, keepdims=True)
    var = jnp.mean(xc * xc, axis=-1, keepdims=True)
    return xc * lax.rsqrt(var + LN_EPS) * g + b


def _attn_prompt_kernel(lam_init, tq, nq, group, qt_ref, k_ref, vt_ref, pbias_ref, lam_ref, nw_ref, o_ref):
    i = pl.program_id(1)
    lam = _lambda_value(lam_ref, lam_init)
    chan = lax.broadcasted_iota(jnp.int32, (DK_D, tq), 0)
    first_map = chan < DH_HALF

    def fold8(op, s):
        return op(s.reshape(s.shape[0] // 8, 8, s.shape[1]), axis=0)

    def process(ii):
        n = (ii + 1) * tq
        n_far = max(n - 2 * tq, 0)

        def head(h):
            qt = qt_ref[h]
            zero = jnp.zeros_like(qt)
            q2t = jnp.concatenate([jnp.where(first_map, qt, zero), jnp.where(first_map, zero, qt)], axis=1)
            parts = []
            if n_far:
                parts.append(_dot(k_ref[h, 0:n_far, :], q2t))
            if ii >= 1:
                bs = pbias_ref[h, 0]
                parts.append(_dot(k_ref[h, n - 2 * tq:n - tq, :], q2t) + jnp.concatenate([bs, bs], axis=1))
            bd = pbias_ref[h, 1]
            parts.append(_dot(k_ref[h, n - tq:n, :], q2t) + jnp.concatenate([bd, bd], axis=1))
            m8 = fold8(jnp.max, parts[0])
            for s in parts[1:]:
                m8 = jnp.maximum(m8, fold8(jnp.max, s))
            m = jnp.max(m8, axis=0, keepdims=True)
            probs = [jnp.exp2(s - m) for s in parts]
            l8 = fold8(jnp.sum, probs[0])
            for p in probs[1:]:
                l8 = l8 + fold8(jnp.sum, p)
            l = jnp.sum(l8, axis=0, keepdims=True)
            p_all = jnp.concatenate([p.astype(BF16) for p in probs], axis=0)
            o = _dot(vt_ref[h, :, 0:n], p_all) / l
            o = o[:, :tq] - lam * o[:, tq:]
            o = o * lax.rsqrt(jnp.mean(o * o, axis=0, keepdims=True) + RMS_EPS) * (nw_ref[...] * (1.0 - lam_init))
            o_ref[h] = o.T.astype(o_ref.dtype)

        def head_group(g, carry):
            for u in range(group):
                head(g * group + u)
            return carry

        lax.fori_loop(0, H_D // group, head_group, 0)

    for ii in range(nq):
        pl.when(i == ii)(functools.partial(process, ii))


def _attn_prompt(qt, k, vt, pbias, lam_params, norm_w, batch, seq, blk, group, lam_init):
    nq = seq // blk
    assert H_D % group == 0
    return pl.pallas_call(
        functools.partial(_attn_prompt_kernel, lam_init, blk, nq, group),
        grid=(batch, nq),
        in_specs=[pl.BlockSpec((H_D, DK_D, blk), lambda b, i: (0, 0, b * nq + i)),
                  pl.BlockSpec((H_D, seq, DK_D), lambda b, i: (0, b, 0)),
                  pl.BlockSpec((H_D, DV_D, seq), lambda b, i: (0, 0, b)),
                  _const_spec(pbias.shape),
                  _const_spec(lam_params.shape),
                  _const_spec(norm_w.shape)],
        out_specs=pl.BlockSpec((H_D, blk, DV_D), lambda b, i: (0, b * nq + i, 0)),
        out_shape=jax.ShapeDtypeStruct((H_D, batch * seq, DV_D), BF16),
        compiler_params=_params("parallel", "parallel"),
        name="attn_prompt",
    )(qt, k, vt, pbias, lam_params, norm_w)


def _attn_sample_body(lam_init, ls, q_ref, kn_ref, vn_ref, sbias_ref, lam_ref, nw_ref, k_pages, v_pages, o_ref):
    n_pages = len(k_pages)
    rows = 2 * ls * H_D
    lam = _lambda_value(lam_ref, lam_init)

    q = q_ref[...]
    qt = jnp.concatenate([q] * (2 * H_D), axis=0)
    rowi = lax.broadcasted_iota(jnp.int32, (rows, DIFF_W), 0)
    coli = lax.broadcasted_iota(jnp.int32, (rows, DIFF_W), 1)
    q_bd = jnp.where(coli // DH_HALF == rowi // ls, qt, 0.0).astype(BF16)

    pad = jnp.zeros((PAGE_SIZE - ls, DIFF_W), F32)

    def page(ref):
        return _load_head_rows(ref, PAGE_SIZE).astype(BF16)

    k_new = jnp.concatenate([_load_head_rows(kn_ref, ls), pad], axis=0).astype(BF16)
    v_new = jnp.concatenate([_load_head_rows(vn_ref, ls), pad], axis=0).astype(BF16)

    k_all = jnp.concatenate([page(r) for r in k_pages] + [k_new], axis=0)
    n_keys = k_all.shape[0]
    s = _dot_nt(q_bd, k_all) + sbias_ref[...]
    m = jnp.max(s, axis=1, keepdims=True)
    p = jnp.exp(s - m)
    l = jnp.sum(p, axis=1, keepdims=True)
    second_map = (lax.broadcasted_iota(jnp.int32, (rows, 1), 0) // ls) % 2 == 1
    fac = jnp.where(second_map, -lam, 1.0) / l
    p3 = (p * fac).reshape(H_D, 2 * ls, n_keys)
    w = (p3[:, :ls] + p3[:, ls:]).reshape(H_D * ls, n_keys).astype(BF16)
    v_all = jnp.concatenate([page(r) for r in v_pages] + [v_new], axis=0)
    acc = _dot(w, v_all)
    for h in range(H_D):
        o = acc[h * ls:(h + 1) * ls, h * DV_D:(h + 1) * DV_D]
        o_ref[:, h * DV_D:(h + 1) * DV_D] = _head_rms(o, nw_ref[...]) * (1.0 - lam_init)


def _mlstm_chunk(q, k, v, o_pre, i_pre, f_pre, c_prev, n_prev, m_prev, norm_w):
    L = q.shape[0]
    t_idx = lax.broadcasted_iota(jnp.int32, (L, L), 0)
    s_idx = lax.broadcasted_iota(jnp.int32, (L, L), 1)
    causal = s_idx <= t_idx
    eye = s_idx == t_idx

    def to_row(col):
        return jnp.sum(jnp.where(eye, col, 0.0), axis=0, keepdims=True)

    it_col = i_pre
    lf_col = jax.nn.log_sigmoid(f_pre)
    it_row = to_row(it_col)
    b_row = jnp.sum(jnp.where(t_idx <= s_idx, lf_col, 0.0), axis=0, keepdims=True)
    b_col = jnp.sum(jnp.where(causal, to_row(lf_col), 0.0), axis=1, keepdims=True)

    log_d = jnp.where(causal, b_col - b_row + it_row, -jnp.inf)
    m_t = jnp.maximum(b_col + m_prev, jnp.max(log_d, axis=1, keepdims=True))
    dmat = jnp.exp(log_d - m_t)
    inter = jnp.exp(b_col + m_prev - m_t)

    w = _dot_nt(q, k) * dmat
    kf = k.astype(F32)
    num = inter * _dot_nt(q, c_prev.astype(BF16)) + _dot(w.astype(BF16), v)
    den = inter * jnp.sum(q.astype(F32) * n_prev, axis=1, keepdims=True) + jnp.sum(w, axis=1, keepdims=True)
    h = num / jnp.maximum(jnp.abs(den), jnp.exp(-m_t))
    h = _head_rms(h, norm_w) * jax.nn.sigmoid(o_pre.astype(F32))

    m_new = m_t[L - 1:L]
    b_last = b_col[L - 1:L]
    g = jnp.exp(b_last - b_col + it_col - m_new)
    decay = jnp.exp(b_last + m_prev - m_new)
    gk = g * kf
    c_new = decay * c_prev + _dot_tn(v, gk.astype(BF16))
    n_new = decay * n_prev + jnp.sum(gk, axis=0, keepdims=True)
    return h, c_new, n_new, m_new


def _mlstm_prompt_kernel(bg_ref, q_ref, k_ref, v_ref, o_ref, g_ref, nw_ref, h_ref, c_ref, n_ref, m_ref):
    @pl.when(pl.program_id(1) == 0)
    def _():
        c_ref[...] = jnp.zeros_like(c_ref)
        n_ref[...] = jnp.zeros_like(n_ref)
        m_ref[...] = jnp.zeros_like(m_ref)

    for h in range(H_M):
        cols = slice(h * DH_M, (h + 1) * DH_M)
        out, c_new, n_new, m_new = _mlstm_chunk(
            q_ref[:, cols], k_ref[:, cols], v_ref[:, cols], o_ref[:, cols],
            g_ref[:, h:h + 1] + bg_ref[h], g_ref[:, H_M + h:H_M + h + 1] + bg_ref[H_M + h],
            c_ref[0, h], n_ref[0, h:h + 1, :], m_ref[0, h:h + 1, 0:1], nw_ref[:, cols])
        h_ref[:, cols] = out.astype(h_ref.dtype)
        c_ref[0, h] = c_new
        n_ref[0, h:h + 1, :] = n_new
        m_ref[0, h:h + 1, :] = jnp.broadcast_to(m_new, (1, m_ref.shape[2]))


def _mlstm_prompt(b_gates, qm, km, vm, om, gates, norm_w, batch, seq, chunk):
    nc = seq // chunk
    tok = lambda w: pl.BlockSpec((chunk, w), lambda b, c: (b * nc + c, 0))
    sds = jax.ShapeDtypeStruct
    return pl.pallas_call(
        _mlstm_prompt_kernel,
        grid=(batch, nc),
        in_specs=[pl.BlockSpec(memory_space=pltpu.SMEM),
                  tok(MLSTM_W), tok(MLSTM_W), tok(MLSTM_W), tok(MLSTM_W), tok(GATE_PAD),
                  _const_spec(norm_w.shape)],
        out_specs=[tok(MLSTM_W),
                   pl.BlockSpec((1, H_M, DH_M, DH_M), lambda b, c: (b, 0, 0, 0)),
                   pl.BlockSpec((1, H_M, DH_M), lambda b, c: (b, 0, 0)),
                   pl.BlockSpec((1, H_M, 128), lambda b, c: (b, 0, 0))],
        out_shape=[sds((batch * seq, MLSTM_W), BF16), sds((batch, H_M, DH_M, DH_M), F32),
                   sds((batch, H_M, DH_M), F32), sds((batch, H_M, 128), F32)],
        compiler_params=_params("parallel", "arbitrary"),
        name="mlstm_prompt",
    )(b_gates, qm, km, vm, om, gates, norm_w)


def _mlstm_sample_kernel(ls, group, bg_ref, q_ref, k_ref, v_ref, o_ref, g_ref, nw_ref, c_in, n_in, m_in,
                         h_ref, c_ref, n_ref, m_ref):
    def seq(s, carry):
        rows = pl.ds(pl.multiple_of(s * ls, ls), ls)
        for h in range(H_M):
            cols = slice(h * DH_M, (h + 1) * DH_M)
            out, c_new, n_new, m_new = _mlstm_chunk(
                q_ref[rows, cols].astype(BF16), k_ref[rows, cols].astype(BF16), v_ref[rows, cols].astype(BF16),
                o_ref[rows, cols],
                g_ref[rows, h:h + 1] + bg_ref[h], g_ref[rows, H_M + h:H_M + h + 1] + bg_ref[H_M + h],
                c_in[s, h], n_in[s, h:h + 1, :], m_in[s, h:h + 1, 0:1], nw_ref[:, cols])
            h_ref[rows, cols] = out
            c_ref[s, h] = c_new
            n_ref[s, h:h + 1, :] = n_new
            m_ref[s, h:h + 1, :] = jnp.broadcast_to(m_new, (1, m_ref.shape[2]))
        return carry

    lax.fori_loop(0, group, seq, 0)


def _mlstm_sample(b_gates, qm, km, vm, om, gates, norm_w, state_c, state_n, state_m, ls, group):
    bs = state_c.shape[0]
    assert bs % group == 0
    tok = lambda w: pl.BlockSpec((group * ls, w), lambda i: (i, 0))
    c_spec = pl.BlockSpec((group, H_M, DH_M, DH_M), lambda i: (i, 0, 0, 0))
    n_spec = pl.BlockSpec((group, H_M, DH_M), lambda i: (i, 0, 0))
    m_spec = pl.BlockSpec((group, H_M, 128), lambda i: (i, 0, 0))
    sds = jax.ShapeDtypeStruct
    return pl.pallas_call(
        functools.partial(_mlstm_sample_kernel, ls, group),
        grid=(bs // group,),
        in_specs=[pl.BlockSpec(memory_space=pltpu.SMEM),
                  tok(MLSTM_W), tok(MLSTM_W), tok(MLSTM_W), tok(MLSTM_W), tok(GATE_PAD),
                  _const_spec(norm_w.shape), c_spec, n_spec, m_spec],
        out_specs=[tok(MLSTM_W), c_spec, n_spec, m_spec],
        out_shape=[sds((bs * ls, MLSTM_W), F32), sds(state_c.shape, F32),
                   sds(state_n.shape, F32), sds((bs, H_M, 128), F32)],
        compiler_params=_params("parallel"),
        name="mlstm_sample",
    )(b_gates, qm, km, vm, om, gates, norm_w, state_c, state_n, state_m)


def _mix_kernel(head_major, x_ref, hm_ref, hd_ref, w_ref, g_ref, b_ref, y_ref, yb_ref):
    if head_major:
        hd = jnp.concatenate([hd_ref[h] for h in range(H_D)], axis=-1)
    else:
        hd = hd_ref[...]
    mix = _dot(hm_ref[...].astype(BF16), w_ref[0:MLSTM_W, :]) + _dot(hd.astype(BF16), w_ref[MLSTM_W:, :])
    y = _layer_norm(ALPHA * x_ref[...] + mix, g_ref[...], b_ref[...])
    y_ref[...] = y
    yb_ref[...] = y.astype(BF16)


def _mix(x, hm, hd, w_out, ln_g, ln_b, tm, head_major):
    t = x.shape[0]
    assert t % tm == 0
    row = lambda w: pl.BlockSpec((tm, w), lambda i: (i, 0))
    hd_spec = pl.BlockSpec((H_D, tm, DV_D), lambda i: (0, i, 0)) if head_major else row(DIFF_W)
    return pl.pallas_call(
        functools.partial(_mix_kernel, head_major),
        grid=(t // tm,),
        in_specs=[row(D_MODEL), row(MLSTM_W), hd_spec, _resident_spec(w_out.shape),
                  _const_spec(ln_g.shape), _const_spec(ln_b.shape)],
        out_specs=[row(D_MODEL), row(D_MODEL)],
        out_shape=[jax.ShapeDtypeStruct((t, D_MODEL), F32), jax.ShapeDtypeStruct((t, D_MODEL), BF16)],
        compiler_params=_params("parallel"),
        name="mix_ln",
    )(x, hm, hd, w_out, ln_g, ln_b)


def _ffn_up_math(xb_ref, wg_ref, wu_ref, h_ref):
    xb = xb_ref[...]
    h_ref[...] = (jax.nn.silu(_dot(xb, wg_ref[...])) * _dot(xb, wu_ref[...])).astype(h_ref.dtype)


def _ffn_up_kernel(xb_ref, wg_ref, wu_ref, h_ref):
    _ffn_up_math(xb_ref, wg_ref, wu_ref, h_ref)


def _ffn_up(xb, w_gate, w_up, tm, tf):
    t = xb.shape[0]
    assert t % tm == 0 and D_FF % tf == 0
    return pl.pallas_call(
        _ffn_up_kernel,
        grid=(t // tm, D_FF // tf),
        in_specs=[pl.BlockSpec((tm, D_MODEL), lambda i, j: (i, 0)),
                  pl.BlockSpec((D_MODEL, tf), lambda i, j: (0, j)),
                  pl.BlockSpec((D_MODEL, tf), lambda i, j: (0, j))],
        out_specs=pl.BlockSpec((tm, tf), lambda i, j: (i, j)),
        out_shape=jax.ShapeDtypeStruct((t, D_FF), BF16),
        compiler_params=_params("parallel", "parallel"),
        name="ffn_up",
    )(xb, w_gate, w_up)


def _ffn_up_attn_kernel(lam_init, n_pages, ls, n_seq, pt_ref, xb_ref, wg_ref, wu_ref,
                        q_ref, kn_ref, vn_ref, sbias_ref, lam_ref, nw_ref, *rest):
    del pt_ref
    k_pages = rest[:n_pages]
    v_pages = rest[n_pages:2 * n_pages]
    h_ref, o_ref = rest[2 * n_pages:]
    _ffn_up_math(xb_ref, wg_ref, wu_ref, h_ref)
    step = pl.program_id(0) * pl.num_programs(1) + pl.program_id(1)

    @pl.when(step < n_seq)
    def _():
        _attn_sample_body(lam_init, ls, q_ref, kn_ref, vn_ref, sbias_ref, lam_ref, nw_ref, k_pages, v_pages, o_ref)


def _ffn_up_attn_sample(xb, w_gate, w_up, tm, tf, page_table, q, k_new, v_new, cache_k, cache_v, sbias,
                        lam_params, norm_w, ls, lam_init):
    t = xb.shape[0]
    bs, n_pages = page_table.shape
    n_i, n_j = t // tm, D_FF // tf
    assert t % tm == 0 and D_FF % tf == 0 and n_i * n_j >= bs
    rows_per_page = PAGE_SIZE * H_D
    cache_k = cache_k.reshape(-1, DK_D)
    cache_v = cache_v.reshape(-1, DV_D)

    def seq(i, j):
        return jnp.minimum(i * n_j + j, bs - 1)

    seq_spec = pl.BlockSpec((ls, DIFF_W), lambda i, j, pt: (seq(i, j), 0))
    new_spec = pl.BlockSpec((ls * H_D, DK_D), lambda i, j, pt: (seq(i, j), 0))

    def page_spec(p):
        return pl.BlockSpec((rows_per_page, DK_D), lambda i, j, pt: (pt[seq(i, j), p], 0))

    def const(shape):
        return pl.BlockSpec(shape, lambda i, j, pt: (0,) * len(shape))

    grid_spec = pltpu.PrefetchScalarGridSpec(
        num_scalar_prefetch=1,
        grid=(n_i, n_j),
        in_specs=[pl.BlockSpec((tm, D_MODEL), lambda i, j, pt: (i, 0)),
                  pl.BlockSpec((D_MODEL, tf), lambda i, j, pt: (0, j)),
                  pl.BlockSpec((D_MODEL, tf), lambda i, j, pt: (0, j)),
                  seq_spec, new_spec, new_spec,
                  const(sbias.shape), const(lam_params.shape), const(norm_w.shape)]
        + [page_spec(p) for p in range(n_pages)] * 2,
        out_specs=[pl.BlockSpec((tm, tf), lambda i, j, pt: (i, j)), seq_spec],
    )
    return pl.pallas_call(
        functools.partial(_ffn_up_attn_kernel, lam_init, n_pages, ls, bs),
        grid_spec=grid_spec,
        out_shape=[jax.ShapeDtypeStruct((t, D_FF), BF16), jax.ShapeDtypeStruct((bs * ls, DIFF_W), F32)],
        compiler_params=_params("arbitrary", "arbitrary"),
        name="ffn_up_attn_sample",
    )(page_table, xb, w_gate, w_up, q, k_new, v_new, sbias, lam_params, norm_w,
      *([cache_k] * n_pages), *([cache_v] * n_pages))


def _ffn_down_kernel(x_ref, h_ref, wd_ref, g_ref, b_ref, y_ref):
    y_ref[...] = _layer_norm(ALPHA * x_ref[...] + _dot(h_ref[...], wd_ref[...]), g_ref[...], b_ref[...])


def _ffn_down(x, h, w_down, ln_g, ln_b, tm):
    t = x.shape[0]
    assert t % tm == 0
    row = lambda w: pl.BlockSpec((tm, w), lambda i: (i, 0))
    return pl.pallas_call(
        _ffn_down_kernel,
        grid=(t // tm,),
        in_specs=[row(D_MODEL), row(D_FF), _resident_spec(w_down.shape),
                  _const_spec(ln_g.shape), _const_spec(ln_b.shape)],
        out_specs=row(D_MODEL),
        out_shape=jax.ShapeDtypeStruct((t, D_MODEL), F32),
        compiler_params=_params("parallel"),
        name="ffn_down_ln",
    )(x, h, w_down, ln_g, ln_b)


def _tiles(seq):
    return dict(
        cast_rows=256,
        proj_tm=256,
        attn_blk=min(256, seq),
        attn_group=2,
        mlstm_chunk=min(256, seq),
        sample_group=4,
        mix_tm=256,
        ffn_up_tm=1024,
        ffn_up_tf=256,
        ffn_down_tm=256,
    )


def kernel(x_prompt, x_sample, cache_k, cache_v, state_C, state_n, state_m, page_table, rel_bias, w_in, b_gates, lambda_q1, lambda_k1, lambda_q2, lambda_k2, diff_norm_w, mlstm_norm_w, w_out, ln1_g, ln1_b, w_gate, w_up, w_down, ln2_g, ln2_b):
    B, S, _ = x_prompt.shape
    Bs, Ls, _ = x_sample.shape
    n_pages = page_table.shape[1]
    past = n_pages * cache_k.shape[2]
    assert w_in.shape[0] == DEPTH == 1 and cache_k.shape[2] == PAGE_SIZE
    tl = _tiles(S)
    l = 0
    lam_init = 0.8 - 0.6 * math.exp(-0.3 * l)

    w = w_in[l]
    wd, wm = _split_w_in(w, tl["cast_rows"])
    wg = jnp.pad(w[:, DIFF_COLS + MLSTM_COLS:], ((0, 0), (0, GATE_PAD - N_GATES))).astype(BF16)
    w_o = w_out[l].astype(BF16)
    w_g, w_u, w_d = w_gate[l].astype(BF16), w_up[l].astype(BF16), w_down[l].astype(BF16)
    lam_params = jnp.stack([lambda_q1[l], lambda_k1[l], lambda_q2[l], lambda_k2[l]], 0)
    dnw = diff_norm_w[l].reshape(1, DV_D)
    mnw = mlstm_norm_w[l].reshape(1, MLSTM_W)
    g1, b1 = ln1_g[l].reshape(1, D_MODEL), ln1_b[l].reshape(1, D_MODEL)
    g2, b2 = ln2_g[l].reshape(1, D_MODEL), ln2_b[l].reshape(1, D_MODEL)
    bg = b_gates[l]

    pbias, sbias = _bias_tables(rel_bias, tl["attn_blk"], past, Ls)
    sbias = sbias.reshape(H_D * 2 * Ls, past + PAGE_SIZE)

    xp = x_prompt.reshape(B * S, D_MODEL)
    qt, k32, k16, v32, vt, qm, km, vm, om, gates = _in_proj(xp, wd, wm, wg, tl["proj_tm"], True)
    hd = _attn_prompt(qt, k16, vt, pbias, lam_params, dnw.reshape(DV_D, 1), B, S, tl["attn_blk"],
                      tl["attn_group"], lam_init)
    hm, c_p, n_p, m_p = _mlstm_prompt(bg, qm, km, vm, om, gates, mnw, B, S, tl["mlstm_chunk"])
    x1, x1b = _mix(xp, hm, hd, w_o, g1, b1, tl["mix_tm"], True)

    xs = x_sample.reshape(Bs * Ls, D_MODEL)
    qs, ks, vs, qms, kms, vms, oms, gates_s = _in_proj(xs, wd, wm, wg, tl["proj_tm"], False)
    up_p, hd_s = _ffn_up_attn_sample(x1b, w_g, w_u, min(tl["ffn_up_tm"], B * S), tl["ffn_up_tf"], page_table,
                                     qs, ks, vs, cache_k, cache_v, sbias, lam_params, dnw, Ls, lam_init)
    y_p = _ffn_down(x1, up_p, w_d, g2, b2, tl["ffn_down_tm"])
    m_in = jnp.broadcast_to(state_m[l][:, :, None], (Bs, H_M, 128))
    hm_s, c_s, n_s, m_s = _mlstm_sample(bg, qms, kms, vms, oms, gates_s, mnw,
                                        state_C[l], state_n[l], m_in, Ls, tl["sample_group"])
    x1s, x1sb = _mix(xs, hm_s, hd_s, w_o, g1, b1, tl["mix_tm"], False)
    up_s = _ffn_up(x1sb, w_g, w_u, min(tl["ffn_up_tm"], Bs * Ls), tl["ffn_up_tf"])
    y_s = _ffn_down(x1s, up_s, w_d, g2, b2, tl["ffn_down_tm"])

    return (y_p.reshape(B, S, D_MODEL), y_s.reshape(Bs, Ls, D_MODEL),
            k32.reshape(1, B, S, H_D, DK_D), v32.reshape(1, B, S, H_D, DV_D),
            c_p[None], n_p[None], m_p[None, :, :, 0],
            ks.reshape(1, Bs, Ls, H_D, DK_D), vs.reshape(1, Bs, Ls, H_D, DV_D),
            c_s[None], n_s[None], m_s[None, :, :, 0])
```

```python
import functools
import math

import numpy as np
import jax
import jax.numpy as jnp
from jax import lax
from jax.experimental import pallas as pl
from jax.experimental.pallas import tpu as pltpu

D_MODEL = 2048
DEPTH = 1
PAGE_SIZE = 128
H_D = 8
DH_HALF = 64
DK_D = 2 * DH_HALF
DV_D = 2 * DH_HALF
DIFF_W = H_D * DV_D
H_M = 4
DH_M = 256
MLSTM_W = H_M * DH_M
D_FF = -(-8 * D_MODEL // (3 * 256)) * 256
N_BUCKETS = 32
MAX_DIST = 128
ALPHA = (2 * DEPTH) ** 0.25
LN_EPS = 1e-5
RMS_EPS = 1e-6
N_GATES = 2 * H_M
GATE_PAD = 128
DIFF_COLS = 3 * DIFF_W
MLSTM_COLS = 4 * MLSTM_W
MASK_VALUE = -1e30
LOG2E = math.log2(math.e)

VMEM_LIMIT_BYTES = 56 * 1024 * 1024

BF16 = jnp.bfloat16
F32 = jnp.float32


def _dot(a, b):
    return jnp.dot(a, b, preferred_element_type=F32)


def _dot_nt(a, b):
    return lax.dot_general(a, b, (((1,), (1,)), ((), ())), preferred_element_type=F32)


def _dot_tn(a, b):
    return lax.dot_general(a, b, (((0,), (0,)), ((), ())), preferred_element_type=F32)


def _params(*semantics):
    return pltpu.CompilerParams(dimension_semantics=semantics, vmem_limit_bytes=VMEM_LIMIT_BYTES)


def _const_spec(shape):
    n = len(shape)
    return pl.BlockSpec(shape, lambda *_: (0,) * n)


def _resident_spec(shape):
    n = len(shape)
    return pl.BlockSpec(shape, lambda *_: (0,) * n, pipeline_mode=pl.Buffered(1))


def _bucket_np(dist):
    n = np.maximum(dist, 0)
    max_exact = N_BUCKETS // 2
    nf = np.maximum(n, 1).astype(np.float32)
    large = max_exact + (np.log(nf / np.float32(max_exact)) / np.float32(math.log(MAX_DIST / max_exact))
                         * np.float32(N_BUCKETS - max_exact)).astype(np.int32)
    large = np.minimum(large, N_BUCKETS - 1)
    out = np.where(n < max_exact, n, large).astype(np.int32)
    return np.where(dist < 0, -1, out).astype(np.int32)


def _bias_kernel(rb_ref, pb_ref, sb_ref, pbias_ref, sbias_ref):
    h = pl.program_id(0)

    def lookup(bk, shift, scale):
        acc = jnp.full(bk.shape, MASK_VALUE, F32)
        for b in range(N_BUCKETS):
            acc = jnp.where(bk == b, (rb_ref[b, h] - shift) * scale, acc)
        return acc

    for t in range(pb_ref.shape[0]):
        pbias_ref[0, t] = lookup(pb_ref[t], rb_ref[N_BUCKETS - 1, h], LOG2E)
    sbias_ref[0] = lookup(sb_ref[...], 0.0, 1.0)


def _bias_tables(rel_bias, blk, past, dec_seq):
    k = np.arange(blk)[:, None]
    q = np.arange(blk)[None, :]
    pb = np.stack([_bucket_np(blk + q - k), _bucket_np(q - k)], 0)
    far = N_BUCKETS - 1
    assert (_bucket_np(np.arange(MAX_DIST, 4 * past + 4 * blk)) == far).all()
    assert blk >= MAX_DIST and PAGE_SIZE >= MAX_DIST
    qpos = past + np.arange(dec_seq)[:, None]
    last = _bucket_np(qpos - (past - PAGE_SIZE + np.arange(PAGE_SIZE))[None, :])
    new = _bucket_np(qpos - (past + np.arange(PAGE_SIZE))[None, :])
    new[:, dec_seq:] = -1
    sb = np.concatenate([np.full((dec_seq, past - PAGE_SIZE), far, np.int32), last, new], 1)
    sb = np.concatenate([sb, sb], 0)
    return pl.pallas_call(
        _bias_kernel,
        grid=(H_D,),
        in_specs=[pl.BlockSpec(memory_space=pltpu.SMEM),
                  _const_spec(pb.shape), _const_spec(sb.shape)],
        out_specs=[pl.BlockSpec((1,) + pb.shape, lambda h: (h, 0, 0, 0)),
                   pl.BlockSpec((1,) + sb.shape, lambda h: (h, 0, 0))],
        out_shape=[jax.ShapeDtypeStruct((H_D,) + pb.shape, F32),
                   jax.ShapeDtypeStruct((H_D,) + sb.shape, F32)],
        compiler_params=_params("arbitrary"),
        name="bias_tables",
    )(rel_bias, jnp.asarray(pb), jnp.asarray(sb))


Q_SCALE = DH_HALF ** -0.5
K_SCALE = DH_M ** -0.5


def _head_rows(ref, h, n_tokens):
    return ref[pl.ds(h, n_tokens, stride=H_D), :]


def _store_head_rows(ref, x):
    for h in range(H_D):
        ref[pl.ds(h, x.shape[0], stride=H_D), :] = x[:, h * DK_D:(h + 1) * DK_D]


def _load_head_rows(ref, n_tokens):
    return jnp.concatenate([_head_rows(ref, h, n_tokens) for h in range(H_D)], axis=1)


def _transpose_cast_kernel(wt_ref, w_ref):
    w_ref[...] = wt_ref[...].T.astype(BF16)


def _transpose_cast(wt, first_col, n_cols, blk):
    d = wt.shape[1]
    assert first_col % blk == 0 and n_cols % blk == 0
    return pl.pallas_call(
        _transpose_cast_kernel,
        grid=(n_cols // blk,),
        in_specs=[pl.BlockSpec((blk, d), lambda j: (first_col // blk + j, 0))],
        out_specs=pl.BlockSpec((d, blk), lambda j: (0, j)),
        out_shape=jax.ShapeDtypeStruct((d, n_cols), BF16),
        compiler_params=_params("parallel"),
        name="transpose_cast",
    )(wt)


def _in_proj_prompt_kernel(x_ref, wd_ref, wm_ref, wg_ref,
                           qt_ref, k32_ref, k16_ref, v32_ref, vt_ref,
                           qm_ref, km_ref, vm_ref, om_ref, g_ref):
    x = x_ref[...].astype(BF16)
    r = _dot(x, wd_ref[:, 0:DIFF_W]) * (Q_SCALE * LOG2E)
    for h in range(H_D):
        qt_ref[h] = r[:, h * DK_D:(h + 1) * DK_D].T.astype(BF16)
    r = _dot(x, wd_ref[:, DIFF_W:2 * DIFF_W])
    _store_head_rows(k32_ref, r)
    for h in range(H_D):
        k16_ref[h] = r[:, h * DK_D:(h + 1) * DK_D].astype(BF16)
    r = _dot(x, wd_ref[:, 2 * DIFF_W:3 * DIFF_W])
    _store_head_rows(v32_ref, r)
    for h in range(H_D):
        vt_ref[h] = r[:, h * DV_D:(h + 1) * DV_D].T.astype(BF16)
    qm_ref[...] = _dot(x, wm_ref[:, 0:MLSTM_W]).astype(BF16)
    km_ref[...] = (_dot(x, wm_ref[:, MLSTM_W:2 * MLSTM_W]) * K_SCALE).astype(BF16)
    vm_ref[...] = _dot(x, wm_ref[:, 2 * MLSTM_W:3 * MLSTM_W]).astype(BF16)
    om_ref[...] = _dot(x, wm_ref[:, 3 * MLSTM_W:4 * MLSTM_W]).astype(BF16)
    g_ref[...] = _dot(x, wg_ref[...])


def _in_proj_sample_kernel(x_ref, wd_ref, wm_ref, wg_ref,
                           q_ref, k_ref, v_ref, qm_ref, km_ref, vm_ref, om_ref, g_ref):
    x = x_ref[...].astype(BF16)
    q_ref[...] = _dot(x, wd_ref[:, 0:DIFF_W]) * Q_SCALE
    _store_head_rows(k_ref, _dot(x, wd_ref[:, DIFF_W:2 * DIFF_W]))
    _store_head_rows(v_ref, _dot(x, wd_ref[:, 2 * DIFF_W:3 * DIFF_W]))
    qm_ref[...] = _dot(x, wm_ref[:, 0:MLSTM_W])
    km_ref[...] = _dot(x, wm_ref[:, MLSTM_W:2 * MLSTM_W]) * K_SCALE
    vm_ref[...] = _dot(x, wm_ref[:, 2 * MLSTM_W:3 * MLSTM_W])
    om_ref[...] = _dot(x, wm_ref[:, 3 * MLSTM_W:4 * MLSTM_W])
    g_ref[...] = _dot(x, wg_ref[...])


def _in_proj(x, wd, wm, wg, tm, prompt):
    t = x.shape[0]
    assert t % tm == 0
    row = lambda w: pl.BlockSpec((tm, w), lambda i: (i, 0))
    heads = pl.BlockSpec((H_D, tm, DK_D), lambda i: (0, i, 0))
    cache = pl.BlockSpec((tm * H_D, DK_D), lambda i: (i, 0))
    sds = jax.ShapeDtypeStruct
    if prompt:
        body = _in_proj_prompt_kernel
        heads_t = pl.BlockSpec((H_D, DK_D, tm), lambda i: (0, 0, i))
        out_specs = [heads_t, cache, heads, cache, heads_t] + [row(MLSTM_W)] * 4 + [row(GATE_PAD)]
        out_shape = [sds((H_D, DK_D, t), BF16), sds((t * H_D, DK_D), F32), sds((H_D, t, DK_D), BF16),
                     sds((t * H_D, DV_D), F32), sds((H_D, DV_D, t), BF16)]
        out_shape += [sds((t, MLSTM_W), BF16)] * 4 + [sds((t, GATE_PAD), F32)]
    else:
        body = _in_proj_sample_kernel
        out_specs = [row(DIFF_W), cache, cache] + [row(MLSTM_W)] * 4 + [row(GATE_PAD)]
        out_shape = ([sds((t, DIFF_W), F32)] + [sds((t * H_D, DK_D), F32)] * 2 + [sds((t, MLSTM_W), F32)] * 4
                     + [sds((t, GATE_PAD), F32)])
    return pl.pallas_call(
        body,
        grid=(t // tm,),
        in_specs=[row(D_MODEL), _resident_spec(wd.shape), _resident_spec(wm.shape), _resident_spec(wg.shape)],
        out_specs=out_specs,
        out_shape=out_shape,
        compiler_params=_params("parallel"),
        name="in_proj_prompt" if prompt else "in_proj_sample",
    )(x, wd, wm, wg)


def _lambda_value(lam_ref, lam_init):
    lp = lam_ref[...]
    e1 = jnp.exp(jnp.sum(lp[0:1] * lp[1:2], axis=1, keepdims=True))
    e2 = jnp.exp(jnp.sum(lp[2:3] * lp[3:4], axis=1, keepdims=True))
    return e1 - e2 + lam_init


def _head_rms(o, w):
    return o * lax.rsqrt(jnp.mean(o * o, axis=-1, keepdims=True) + RMS_EPS) * w


def _layer_norm(x, g, b):
    xc = x - jnp.mean(x, axis=-1, keepdims=True)
    var = jnp.mean(xc * xc, axis=-1, keepdims=True)
    return xc * lax.rsqrt(var + LN_EPS) * g + b


def _attn_prompt_kernel(lam_init, tq, nq, group, qt_ref, k_ref, vt_ref, pbias_ref, lam_ref, nw_ref, o_ref):
    i = pl.program_id(1)
    lam = _lambda_value(lam_ref, lam_init)
    chan = lax.broadcasted_iota(jnp.int32, (DK_D, tq), 0)
    first_map = chan < DH_HALF

    def fold8(op, s):
        return op(s.reshape(s.shape[0] // 8, 8, s.shape[1]), axis=0)

    def process(ii):
        n = (ii + 1) * tq
        n_far = max(n - 2 * tq, 0)

        def head(h):
            qt = qt_ref[h]
            zero = jnp.zeros_like(qt)
            q2t = jnp.concatenate([jnp.where(first_map, qt, zero), jnp.where(first_map, zero, qt)], axis=1)
            parts = []
            if n_far:
                parts.append(_dot(k_ref[h, 0:n_far, :], q2t))
            if ii >= 1:
                bs = pbias_ref[h, 0]
                parts.append(_dot(k_ref[h, n - 2 * tq:n - tq, :], q2t) + jnp.concatenate([bs, bs], axis=1))
            bd = pbias_ref[h, 1]
            parts.append(_dot(k_ref[h, n - tq:n, :], q2t) + jnp.concatenate([bd, bd], axis=1))
            m8 = fold8(jnp.max, parts[0])
            for s in parts[1:]:
                m8 = jnp.maximum(m8, fold8(jnp.max, s))
            m = jnp.max(m8, axis=0, keepdims=True)
            probs = [jnp.exp2(s - m) for s in parts]
            l8 = fold8(jnp.sum, probs[0])
            for p in probs[1:]:
                l8 = l8 + fold8(jnp.sum, p)
            l = jnp.sum(l8, axis=0, keepdims=True)
            p_all = jnp.concatenate([p.astype(BF16) for p in probs], axis=0)
            o = _dot(vt_ref[h, :, 0:n], p_all) / l
            o = o[:, :tq] - lam * o[:, tq:]
            o = o * lax.rsqrt(jnp.mean(o * o, axis=0, keepdims=True) + RMS_EPS) * (nw_ref[...] * (1.0 - lam_init))
            o_ref[h] = o.T.astype(o_ref.dtype)

        def head_group(g, carry):
            for u in range(group):
                head(g * group + u)
            return carry

        lax.fori_loop(0, H_D // group, head_group, 0)

    for ii in range(nq):
        pl.when(i == ii)(functools.partial(process, ii))


def _attn_prompt(qt, k, vt, pbias, lam_params, norm_w, batch, seq, blk, group, lam_init):
    nq = seq // blk
    assert H_D % group == 0
    return pl.pallas_call(
        functools.partial(_attn_prompt_kernel, lam_init, blk, nq, group),
        grid=(batch, nq),
        in_specs=[pl.BlockSpec((H_D, DK_D, blk), lambda b, i: (0, 0, b * nq + i)),
                  pl.BlockSpec((H_D, seq, DK_D), lambda b, i: (0, b, 0)),
                  pl.BlockSpec((H_D, DV_D, seq), lambda b, i: (0, 0, b)),
                  _const_spec(pbias.shape),
                  _const_spec(lam_params.shape),
                  _const_spec(norm_w.shape)],
        out_specs=pl.BlockSpec((H_D, blk, DV_D), lambda b, i: (0, b * nq + i, 0)),
        out_shape=jax.ShapeDtypeStruct((H_D, batch * seq, DV_D), BF16),
        compiler_params=_params("parallel", "parallel"),
        name="attn_prompt",
    )(qt, k, vt, pbias, lam_params, norm_w)


def _attn_sample_body(lam_init, ls, q_ref, kn_ref, vn_ref, sbias_ref, lam_ref, nw_ref, k_pages, v_pages, o_ref):
    n_pages = len(k_pages)
    rows = 2 * ls * H_D
    lam = _lambda_value(lam_ref, lam_init)

    q = q_ref[...]
    qt = jnp.concatenate([q] * (2 * H_D), axis=0)
    rowi = lax.broadcasted_iota(jnp.int32, (rows, DIFF_W), 0)
    coli = lax.broadcasted_iota(jnp.int32, (rows, DIFF_W), 1)
    q_bd = jnp.where(coli // DH_HALF == rowi // ls, qt, 0.0).astype(BF16)

    pad = jnp.zeros((PAGE_SIZE - ls, DIFF_W), F32)

    def page(ref):
        return _load_head_rows(ref, PAGE_SIZE).astype(BF16)

    k_new = jnp.concatenate([_load_head_rows(kn_ref, ls), pad], axis=0).astype(BF16)
    v_new = jnp.concatenate([_load_head_rows(vn_ref, ls), pad], axis=0).astype(BF16)

    k_all = jnp.concatenate([page(r) for r in k_pages] + [k_new], axis=0)
    n_keys = k_all.shape[0]
    s = _dot_nt(q_bd, k_all) + sbias_ref[...]
    m = jnp.max(s, axis=1, keepdims=True)
    p = jnp.exp(s - m)
    l = jnp.sum(p, axis=1, keepdims=True)
    second_map = (lax.broadcasted_iota(jnp.int32, (rows, 1), 0) // ls) % 2 == 1
    fac = jnp.where(second_map, -lam, 1.0) / l
    p3 = (p * fac).reshape(H_D, 2 * ls, n_keys)
    w = (p3[:, :ls] + p3[:, ls:]).reshape(H_D * ls, n_keys).astype(BF16)
    v_all = jnp.concatenate([page(r) for r in v_pages] + [v_new], axis=0)
    acc = _dot(w, v_all)
    for h in range(H_D):
        o = acc[h * ls:(h + 1) * ls, h * DV_D:(h + 1) * DV_D]
        o_ref[:, h * DV_D:(h + 1) * DV_D] = _head_rms(o, nw_ref[...]) * (1.0 - lam_init)


def _mlstm_chunk(q, k, v, o_pre, i_pre, f_pre, c_prev, n_prev, m_prev, norm_w):
    L = q.shape[0]
    t_idx = lax.broadcasted_iota(jnp.int32, (L, L), 0)
    s_idx = lax.broadcasted_iota(jnp.int32, (L, L), 1)
    causal = s_idx <= t_idx
    eye = s_idx == t_idx

    def to_row(col):
        return jnp.sum(jnp.where(eye, col, 0.0), axis=0, keepdims=True)

    it_col = i_pre
    lf_col = jax.nn.log_sigmoid(f_pre)
    it_row = to_row(it_col)
    b_row = jnp.sum(jnp.where(t_idx <= s_idx, lf_col, 0.0), axis=0, keepdims=True)
    b_col = jnp.sum(jnp.where(causal, to_row(lf_col), 0.0), axis=1, keepdims=True)

    log_d = jnp.where(causal, b_col - b_row + it_row, -jnp.inf)
    m_t = jnp.maximum(b_col + m_prev, jnp.max(log_d, axis=1, keepdims=True))
    dmat = jnp.exp(log_d - m_t)
    inter = jnp.exp(b_col + m_prev - m_t)

    w = _dot_nt(q, k) * dmat
    kf = k.astype(F32)
    num = inter * _dot_nt(q, c_prev.astype(BF16)) + _dot(w.astype(BF16), v)
    den = inter * jnp.sum(q.astype(F32) * n_prev, axis=1, keepdims=True) + jnp.sum(w, axis=1, keepdims=True)
    h = num / jnp.maximum(jnp.abs(den), jnp.exp(-m_t))
    h = _head_rms(h, norm_w) * jax.nn.sigmoid(o_pre.astype(F32))

    m_new = m_t[L - 1:L]
    b_last = b_col[L - 1:L]
    g = jnp.exp(b_last - b_col + it_col - m_new)
    decay = jnp.exp(b_last + m_prev - m_new)
    gk = g * kf
    c_new = decay * c_prev + _dot_tn(v, gk.astype(BF16))
    n_new = decay * n_prev + jnp.sum(gk, axis=0, keepdims=True)
    return h, c_new, n_new, m_new


def _mlstm_prompt_kernel(bg_ref, q_ref, k_ref, v_ref, o_ref, g_ref, nw_ref, h_ref, c_ref, n_ref, m_ref):
    @pl.when(pl.program_id(1) == 0)
    def _():
        c_ref[...] = jnp.zeros_like(c_ref)
        n_ref[...] = jnp.zeros_like(n_ref)
        m_ref[...] = jnp.zeros_like(m_ref)

    for h in range(H_M):
        cols = slice(h * DH_M, (h + 1) * DH_M)
        out, c_new, n_new, m_new = _mlstm_chunk(
            q_ref[:, cols], k_ref[:, cols], v_ref[:, cols], o_ref[:, cols],
            g_ref[:, h:h + 1] + bg_ref[h], g_ref[:, H_M + h:H_M + h + 1] + bg_ref[H_M + h],
            c_ref[0, h], n_ref[0, h:h + 1, :], m_ref[0, h:h + 1, 0:1], nw_ref[:, cols])
        h_ref[:, cols] = out.astype(h_ref.dtype)
        c_ref[0, h] = c_new
        n_ref[0, h:h + 1, :] = n_new
        m_ref[0, h:h + 1, :] = jnp.broadcast_to(m_new, (1, m_ref.shape[2]))


def _mlstm_prompt(b_gates, qm, km, vm, om, gates, norm_w, batch, seq, chunk):
    nc = seq // chunk
    tok = lambda w: pl.BlockSpec((chunk, w), lambda b, c: (b * nc + c, 0))
    sds = jax.ShapeDtypeStruct
    return pl.pallas_call(
        _mlstm_prompt_kernel,
        grid=(batch, nc),
        in_specs=[pl.BlockSpec(memory_space=pltpu.SMEM),
                  tok(MLSTM_W), tok(MLSTM_W), tok(MLSTM_W), tok(MLSTM_W), tok(GATE_PAD),
                  _const_spec(norm_w.shape)],
        out_specs=[tok(MLSTM_W),
                   pl.BlockSpec((1, H_M, DH_M, DH_M), lambda b, c: (b, 0, 0, 0)),
                   pl.BlockSpec((1, H_M, DH_M), lambda b, c: (b, 0, 0)),
                   pl.BlockSpec((1, H_M, 128), lambda b, c: (b, 0, 0))],
        out_shape=[sds((batch * seq, MLSTM_W), BF16), sds((batch, H_M, DH_M, DH_M), F32),
                   sds((batch, H_M, DH_M), F32), sds((batch, H_M, 128), F32)],
        compiler_params=_params("parallel", "arbitrary"),
        name="mlstm_prompt",
    )(b_gates, qm, km, vm, om, gates, norm_w)


def _mlstm_sample_kernel(ls, group, bg_ref, q_ref, k_ref, v_ref, o_ref, g_ref, nw_ref, c_in, n_in, m_in,
                         h_ref, c_ref, n_ref, m_ref):
    pair = 2 if group % 2 == 0 else 1

    def seq_pair(s2, carry):
        for u in range(pair):
            seq(s2 * pair + u)
        return carry

    def seq(s):
        rows = pl.ds(pl.multiple_of(s * ls, ls), ls)
        for h in range(H_M):
            cols = slice(h * DH_M, (h + 1) * DH_M)
            out, c_new, n_new, m_new = _mlstm_chunk(
                q_ref[rows, cols].astype(BF16), k_ref[rows, cols].astype(BF16), v_ref[rows, cols].astype(BF16),
                o_ref[rows, cols],
                g_ref[rows, h:h + 1] + bg_ref[h], g_ref[rows, H_M + h:H_M + h + 1] + bg_ref[H_M + h],
                c_in[s, h], n_in[s, h:h + 1, :], m_in[s, h:h + 1, 0:1], nw_ref[:, cols])
            h_ref[rows, cols] = out
            c_ref[s, h] = c_new
            n_ref[s, h:h + 1, :] = n_new
            m_ref[s, h:h + 1, :] = jnp.broadcast_to(m_new, (1, m_ref.shape[2]))

    lax.fori_loop(0, group // pair, seq_pair, 0)


def _mlstm_sample(b_gates, qm, km, vm, om, gates, norm_w, state_c, state_n, state_m, ls, group):
    bs = state_c.shape[0]
    assert bs % group == 0
    tok = lambda w: pl.BlockSpec((group * ls, w), lambda i: (i, 0))
    c_spec = pl.BlockSpec((group, H_M, DH_M, DH_M), lambda i: (i, 0, 0, 0))
    n_spec = pl.BlockSpec((group, H_M, DH_M), lambda i: (i, 0, 0))
    m_spec = pl.BlockSpec((group, H_M, 128), lambda i: (i, 0, 0))
    sds = jax.ShapeDtypeStruct
    return pl.pallas_call(
        functools.partial(_mlstm_sample_kernel, ls, group),
        grid=(bs // group,),
        in_specs=[pl.BlockSpec(memory_space=pltpu.SMEM),
                  tok(MLSTM_W), tok(MLSTM_W), tok(MLSTM_W), tok(MLSTM_W), tok(GATE_PAD),
                  _const_spec(norm_w.shape), c_spec, n_spec, m_spec],
        out_specs=[tok(MLSTM_W), c_spec, n_spec, m_spec],
        out_shape=[sds((bs * ls, MLSTM_W), F32), sds(state_c.shape, F32),
                   sds(state_n.shape, F32), sds((bs, H_M, 128), F32)],
        compiler_params=_params("parallel"),
        name="mlstm_sample",
    )(b_gates, qm, km, vm, om, gates, norm_w, state_c, state_n, state_m)


def _mix_kernel(head_major, x_ref, hm_ref, hd_ref, w_ref, g_ref, b_ref, y_ref, yb_ref):
    if head_major:
        hd = jnp.concatenate([hd_ref[h] for h in range(H_D)], axis=-1)
    else:
        hd = hd_ref[...]
    mix = _dot(hm_ref[...].astype(BF16), w_ref[0:MLSTM_W, :]) + _dot(hd.astype(BF16), w_ref[MLSTM_W:, :])
    y = _layer_norm(ALPHA * x_ref[...] + mix, g_ref[...], b_ref[...])
    y_ref[...] = y
    yb_ref[...] = y.astype(BF16)


def _mix(x, hm, hd, w_out, ln_g, ln_b, tm, head_major):
    t = x.shape[0]
    assert t % tm == 0
    row = lambda w: pl.BlockSpec((tm, w), lambda i: (i, 0))
    hd_spec = pl.BlockSpec((H_D, tm, DV_D), lambda i: (0, i, 0)) if head_major else row(DIFF_W)
    return pl.pallas_call(
        functools.partial(_mix_kernel, head_major),
        grid=(t // tm,),
        in_specs=[row(D_MODEL), row(MLSTM_W), hd_spec, _resident_spec(w_out.shape),
                  _const_spec(ln_g.shape), _const_spec(ln_b.shape)],
        out_specs=[row(D_MODEL), row(D_MODEL)],
        out_shape=[jax.ShapeDtypeStruct((t, D_MODEL), F32), jax.ShapeDtypeStruct((t, D_MODEL), BF16)],
        compiler_params=_params("parallel"),
        name="mix_ln",
    )(x, hm, hd, w_out, ln_g, ln_b)


def _ffn_up_math(xb_ref, wg_ref, wu_ref, h_ref):
    xb = xb_ref[...]
    h_ref[...] = (jax.nn.silu(_dot(xb, wg_ref[...])) * _dot(xb, wu_ref[...])).astype(h_ref.dtype)


def _ffn_up_kernel(xb_ref, wg_ref, wu_ref, h_ref):
    _ffn_up_math(xb_ref, wg_ref, wu_ref, h_ref)


def _ffn_up(xb, w_gate, w_up, tm, tf):
    t = xb.shape[0]
    assert t % tm == 0 and D_FF % tf == 0
    return pl.pallas_call(
        _ffn_up_kernel,
        grid=(t // tm, D_FF // tf),
        in_specs=[pl.BlockSpec((tm, D_MODEL), lambda i, j: (i, 0)),
                  pl.BlockSpec((D_MODEL, tf), lambda i, j: (0, j)),
                  pl.BlockSpec((D_MODEL, tf), lambda i, j: (0, j))],
        out_specs=pl.BlockSpec((tm, tf), lambda i, j: (i, j)),
        out_shape=jax.ShapeDtypeStruct((t, D_FF), BF16),
        compiler_params=_params("parallel", "parallel"),
        name="ffn_up",
    )(xb, w_gate, w_up)


def _ffn_up_attn_kernel(lam_init, n_pages, ls, n_seq, pt_ref, xb_ref, wg_ref, wu_ref,
                        q_ref, kn_ref, vn_ref, sbias_ref, lam_ref, nw_ref, *rest):
    del pt_ref
    k_pages = rest[:n_pages]
    v_pages = rest[n_pages:2 * n_pages]
    h_ref, o_ref = rest[2 * n_pages:]
    _ffn_up_math(xb_ref, wg_ref, wu_ref, h_ref)
    step = pl.program_id(0) * pl.num_programs(1) + pl.program_id(1)

    @pl.when(step < n_seq)
    def _():
        _attn_sample_body(lam_init, ls, q_ref, kn_ref, vn_ref, sbias_ref, lam_ref, nw_ref, k_pages, v_pages, o_ref)


def _ffn_up_attn_sample(xb, w_gate, w_up, tm, tf, page_table, q, k_new, v_new, cache_k, cache_v, sbias,
                        lam_params, norm_w, ls, lam_init):
    t = xb.shape[0]
    bs, n_pages = page_table.shape
    n_i, n_j = t // tm, D_FF // tf
    assert t % tm == 0 and D_FF % tf == 0 and n_i * n_j >= bs
    rows_per_page = PAGE_SIZE * H_D
    cache_k = cache_k.reshape(-1, DK_D)
    cache_v = cache_v.reshape(-1, DV_D)

    def seq(i, j):
        return jnp.minimum(i * n_j + j, bs - 1)

    seq_spec = pl.BlockSpec((ls, DIFF_W), lambda i, j, pt: (seq(i, j), 0))
    new_spec = pl.BlockSpec((ls * H_D, DK_D), lambda i, j, pt: (seq(i, j), 0))

    def page_spec(p):
        return pl.BlockSpec((rows_per_page, DK_D), lambda i, j, pt: (pt[seq(i, j), p], 0))

    def const(shape):
        return pl.BlockSpec(shape, lambda i, j, pt: (0,) * len(shape))

    grid_spec = pltpu.PrefetchScalarGridSpec(
        num_scalar_prefetch=1,
        grid=(n_i, n_j),
        in_specs=[pl.BlockSpec((tm, D_MODEL), lambda i, j, pt: (i, 0)),
                  pl.BlockSpec((D_MODEL, tf), lambda i, j, pt: (0, j)),
                  pl.BlockSpec((D_MODEL, tf), lambda i, j, pt: (0, j)),
                  seq_spec, new_spec, new_spec,
                  const(sbias.shape), const(lam_params.shape), const(norm_w.shape)]
        + [page_spec(p) for p in range(n_pages)] * 2,
        out_specs=[pl.BlockSpec((tm, tf), lambda i, j, pt: (i, j)), seq_spec],
    )
    return pl.pallas_call(
        functools.partial(_ffn_up_attn_kernel, lam_init, n_pages, ls, bs),
        grid_spec=grid_spec,
        out_shape=[jax.ShapeDtypeStruct((t, D_FF), BF16), jax.ShapeDtypeStruct((bs * ls, DIFF_W), F32)],
        compiler_params=_params("arbitrary", "arbitrary"),
        name="ffn_up_attn_sample",
    )(page_table, xb, w_gate, w_up, q, k_new, v_new, sbias, lam_params, norm_w,
      *([cache_k] * n_pages), *([cache_v] * n_pages))


def _ffn_down_kernel(x_ref, h_ref, wd_ref, g_ref, b_ref, y_ref):
    y_ref[...] = _layer_norm(ALPHA * x_ref[...] + _dot(h_ref[...], wd_ref[...]), g_ref[...], b_ref[...])


def _ffn_down(x, h, w_down, ln_g, ln_b, tm):
    t = x.shape[0]
    assert t % tm == 0
    row = lambda w: pl.BlockSpec((tm, w), lambda i: (i, 0))
    return pl.pallas_call(
        _ffn_down_kernel,
        grid=(t // tm,),
        in_specs=[row(D_MODEL), row(D_FF), _resident_spec(w_down.shape),
                  _const_spec(ln_g.shape), _const_spec(ln_b.shape)],
        out_specs=row(D_MODEL),
        out_shape=jax.ShapeDtypeStruct((t, D_MODEL), F32),
        compiler_params=_params("parallel"),
        name="ffn_down_ln",
    )(x, h, w_down, ln_g, ln_b)


def _tiles(seq):
    return dict(
        cast_cols=512,
        proj_tm=256,
        attn_blk=min(256, seq),
        attn_group=2,
        mlstm_chunk=min(512, seq),
        sample_group=4,
        mix_tm=256,
        ffn_up_tm=1024,
        ffn_up_tf=256,
        ffn_down_tm=256,
    )


def kernel(x_prompt, x_sample, cache_k, cache_v, state_C, state_n, state_m, page_table, rel_bias, w_in, b_gates, lambda_q1, lambda_k1, lambda_q2, lambda_k2, diff_norm_w, mlstm_norm_w, w_out, ln1_g, ln1_b, w_gate, w_up, w_down, ln2_g, ln2_b):
    B, S, _ = x_prompt.shape
    Bs, Ls, _ = x_sample.shape
    n_pages = page_table.shape[1]
    past = n_pages * cache_k.shape[2]
    assert w_in.shape[0] == DEPTH == 1 and cache_k.shape[2] == PAGE_SIZE
    tl = _tiles(S)
    l = 0
    lam_init = 0.8 - 0.6 * math.exp(-0.3 * l)

    wt = w_in[l].T
    wd = _transpose_cast(wt, 0, DIFF_COLS, tl["cast_cols"])
    wm = _transpose_cast(wt, DIFF_COLS, MLSTM_COLS, tl["cast_cols"])
    wg = jnp.pad(wt[DIFF_COLS + MLSTM_COLS:], ((0, GATE_PAD - N_GATES), (0, 0))).T.astype(BF16)
    w_o = w_out[l].astype(BF16)
    w_g, w_u, w_d = w_gate[l].astype(BF16), w_up[l].astype(BF16), w_down[l].astype(BF16)
    lam_params = jnp.stack([lambda_q1[l], lambda_k1[l], lambda_q2[l], lambda_k2[l]], 0)
    dnw = diff_norm_w[l].reshape(1, DV_D)
    mnw = mlstm_norm_w[l].reshape(1, MLSTM_W)
    g1, b1 = ln1_g[l].reshape(1, D_MODEL), ln1_b[l].reshape(1, D_MODEL)
    g2, b2 = ln2_g[l].reshape(1, D_MODEL), ln2_b[l].reshape(1, D_MODEL)
    bg = b_gates[l]

    pbias, sbias = _bias_tables(rel_bias, tl["attn_blk"], past, Ls)
    sbias = sbias.reshape(H_D * 2 * Ls, past + PAGE_SIZE)

    xp = x_prompt.reshape(B * S, D_MODEL)
    qt, k32, k16, v32, vt, qm, km, vm, om, gates = _in_proj(xp, wd, wm, wg, tl["proj_tm"], True)
    hd = _attn_prompt(qt, k16, vt, pbias, lam_params, dnw.reshape(DV_D, 1), B, S, tl["attn_blk"],
                      tl["attn_group"], lam_init)
    hm, c_p, n_p, m_p = _mlstm_prompt(bg, qm, km, vm, om, gates, mnw, B, S, tl["mlstm_chunk"])
    x1, x1b = _mix(xp, hm, hd, w_o, g1, b1, tl["mix_tm"], True)

    xs = x_sample.reshape(Bs * Ls, D_MODEL)
    qs, ks, vs, qms, kms, vms, oms, gates_s = _in_proj(xs, wd, wm, wg, tl["proj_tm"], False)
    up_p, hd_s = _ffn_up_attn_sample(x1b, w_g, w_u, min(tl["ffn_up_tm"], B * S), tl["ffn_up_tf"], page_table,
                                     qs, ks, vs, cache_k, cache_v, sbias, lam_params, dnw, Ls, lam_init)
    y_p = _ffn_down(x1, up_p, w_d, g2, b2, tl["ffn_down_tm"])
    m_in = jnp.broadcast_to(state_m[l][:, :, None], (Bs, H_M, 128))
    hm_s, c_s, n_s, m_s = _mlstm_sample(bg, qms, kms, vms, oms, gates_s, mnw,
                                        state_C[l], state_n[l], m_in, Ls, tl["sample_group"])
    x1s, x1sb = _mix(xs, hm_s, hd_s, w_o, g1, b1, tl["mix_tm"], False)
    up_s = _ffn_up(x1sb, w_g, w_u, min(tl["ffn_up_tm"], Bs * Ls), tl["ffn_up_tf"])
    y_s = _ffn_down(x1s, up_s, w_d, g2, b2, tl["ffn_down_tm"])

    return (y_p.reshape(B, S, D_MODEL), y_s.reshape(Bs, Ls, D_MODEL),
            k32.reshape(1, B, S, H_D, DK_D), v32.reshape(1, B, S, H_D, DV_D),
            c_p[None], n_p[None], m_p[None, :, :, 0],
            ks.reshape(1, Bs, Ls, H_D, DK_D), vs.reshape(1, Bs, Ls, H_D, DV_D),
            c_s[None], n_s[None], m_s[None, :, :, 0])
```

```python
import functools
import math

import numpy as np
import jax
import jax.numpy as jnp
from jax import lax
from jax.experimental import pallas as pl
from jax.experimental.pallas import tpu as pltpu

D_MODEL = 2048
DEPTH = 1
PAGE_SIZE = 128
H_D = 8
DH_HALF = 64
DK_D = 2 * DH_HALF
DV_D = 2 * DH_HALF
DIFF_W = H_D * DV_D
H_M = 4
DH_M = 256
MLSTM_W = H_M * DH_M
D_FF = -(-8 * D_MODEL // (3 * 256)) * 256
N_BUCKETS = 32
MAX_DIST = 128
ALPHA = (2 * DEPTH) ** 0.25
LN_EPS = 1e-5
RMS_EPS = 1e-6
N_GATES = 2 * H_M
GATE_PAD = 128
DIFF_COLS = 3 * DIFF_W
MLSTM_COLS = 4 * MLSTM_W
MASK_VALUE = -1e30
LOG2E = math.log2(math.e)

VMEM_LIMIT_BYTES = 56 * 1024 * 1024

BF16 = jnp.bfloat16
F32 = jnp.float32


def _dot(a, b):
    return jnp.dot(a, b, preferred_element_type=F32)


def _dot_nt(a, b):
    return lax.dot_general(a, b, (((1,), (1,)), ((), ())), preferred_element_type=F32)


def _dot_tn(a, b):
    return lax.dot_general(a, b, (((0,), (0,)), ((), ())), preferred_element_type=F32)


def _params(*semantics):
    return pltpu.CompilerParams(dimension_semantics=semantics, vmem_limit_bytes=VMEM_LIMIT_BYTES)


def _const_spec(shape):
    n = len(shape)
    return pl.BlockSpec(shape, lambda *_: (0,) * n)


def _resident_spec(shape):
    n = len(shape)
    return pl.BlockSpec(shape, lambda *_: (0,) * n, pipeline_mode=pl.Buffered(1))


def _bucket_np(dist):
    n = np.maximum(dist, 0)
    max_exact = N_BUCKETS // 2
    nf = np.maximum(n, 1).astype(np.float32)
    large = max_exact + (np.log(nf / np.float32(max_exact)) / np.float32(math.log(MAX_DIST / max_exact))
                         * np.float32(N_BUCKETS - max_exact)).astype(np.int32)
    large = np.minimum(large, N_BUCKETS - 1)
    out = np.where(n < max_exact, n, large).astype(np.int32)
    return np.where(dist < 0, -1, out).astype(np.int32)


def _bias_kernel(rb_ref, pb_ref, sb_ref, pbias_ref, sbias_ref):
    h = pl.program_id(0)

    def lookup(bk, shift, scale):
        acc = jnp.full(bk.shape, MASK_VALUE, F32)
        for b in range(N_BUCKETS):
            acc = jnp.where(bk == b, (rb_ref[b, h] - shift) * scale, acc)
        return acc

    for t in range(pb_ref.shape[0]):
        pbias_ref[0, t] = lookup(pb_ref[t], rb_ref[N_BUCKETS - 1, h], LOG2E)
    sbias_ref[0] = lookup(sb_ref[...], 0.0, 1.0)


def _bias_tables(rel_bias, blk, past, dec_seq):
    k = np.arange(blk)[:, None]
    q = np.arange(blk)[None, :]
    pb = np.stack([_bucket_np(blk + q - k), _bucket_np(q - k)], 0)
    far = N_BUCKETS - 1
    assert (_bucket_np(np.arange(MAX_DIST, 4 * past + 4 * blk)) == far).all()
    assert blk >= MAX_DIST and PAGE_SIZE >= MAX_DIST
    qpos = past + np.arange(dec_seq)[:, None]
    last = _bucket_np(qpos - (past - PAGE_SIZE + np.arange(PAGE_SIZE))[None, :])
    new = _bucket_np(qpos - (past + np.arange(PAGE_SIZE))[None, :])
    new[:, dec_seq:] = -1
    sb = np.concatenate([np.full((dec_seq, past - PAGE_SIZE), far, np.int32), last, new], 1)
    sb = np.concatenate([sb, sb], 0)
    return pl.pallas_call(
        _bias_kernel,
        grid=(H_D,),
        in_specs=[pl.BlockSpec(memory_space=pltpu.SMEM),
                  _const_spec(pb.shape), _const_spec(sb.shape)],
        out_specs=[pl.BlockSpec((1,) + pb.shape, lambda h: (h, 0, 0, 0)),
                   pl.BlockSpec((1,) + sb.shape, lambda h: (h, 0, 0))],
        out_shape=[jax.ShapeDtypeStruct((H_D,) + pb.shape, F32),
                   jax.ShapeDtypeStruct((H_D,) + sb.shape, F32)],
        compiler_params=_params("arbitrary"),
        name="bias_tables",
    )(rel_bias, jnp.asarray(pb), jnp.asarray(sb))


Q_SCALE = DH_HALF ** -0.5
K_SCALE = DH_M ** -0.5


def _head_rows(ref, h, n_tokens):
    return ref[pl.ds(h, n_tokens, stride=H_D), :]


def _store_head_rows(ref, x):
    for h in range(H_D):
        ref[pl.ds(h, x.shape[0], stride=H_D), :] = x[:, h * DK_D:(h + 1) * DK_D]


def _load_head_rows(ref, n_tokens):
    return jnp.concatenate([_head_rows(ref, h, n_tokens) for h in range(H_D)], axis=1)


def _transpose_cast_kernel(wt_ref, w_ref):
    w_ref[...] = wt_ref[...].T.astype(BF16)


def _transpose_cast(wt, first_col, n_cols, blk):
    d = wt.shape[1]
    assert first_col % blk == 0 and n_cols % blk == 0
    return pl.pallas_call(
        _transpose_cast_kernel,
        grid=(n_cols // blk,),
        in_specs=[pl.BlockSpec((blk, d), lambda j: (first_col // blk + j, 0))],
        out_specs=pl.BlockSpec((d, blk), lambda j: (0, j)),
        out_shape=jax.ShapeDtypeStruct((d, n_cols), BF16),
        compiler_params=_params("parallel"),
        name="transpose_cast",
    )(wt)


def _gate_rows_kernel(wt_ref, o_ref):
    zeros = jnp.zeros((GATE_PAD - N_GATES, wt_ref.shape[1]), F32)
    o_ref[...] = jnp.concatenate([wt_ref[...], zeros], axis=0).astype(BF16)


def _gate_rows(wt, first_row):
    d = wt.shape[1]
    assert first_row % N_GATES == 0
    return pl.pallas_call(
        _gate_rows_kernel,
        grid=(1,),
        in_specs=[pl.BlockSpec((N_GATES, d), lambda i: (first_row // N_GATES, 0))],
        out_specs=pl.BlockSpec((GATE_PAD, d), lambda i: (0, 0)),
        out_shape=jax.ShapeDtypeStruct((GATE_PAD, d), BF16),
        compiler_params=_params("arbitrary"),
        name="gate_rows",
    )(wt)


def _in_proj_prompt_kernel(x_ref, wd_ref, wm_ref, wg_ref,
                           qt_ref, k32_ref, k16_ref, v32_ref, vt_ref,
                           qm_ref, km_ref, vm_ref, om_ref, g_ref):
    x = x_ref[...].astype(BF16)
    r = _dot(x, wd_ref[:, 0:DIFF_W]) * (Q_SCALE * LOG2E)
    for h in range(H_D):
        qt_ref[h] = r[:, h * DK_D:(h + 1) * DK_D].T.astype(BF16)
    r = _dot(x, wd_ref[:, DIFF_W:2 * DIFF_W])
    _store_head_rows(k32_ref, r)
    for h in range(H_D):
        k16_ref[h] = r[:, h * DK_D:(h + 1) * DK_D].astype(BF16)
    r = _dot(x, wd_ref[:, 2 * DIFF_W:3 * DIFF_W])
    _store_head_rows(v32_ref, r)
    for h in range(H_D):
        vt_ref[h] = r[:, h * DV_D:(h + 1) * DV_D].T.astype(BF16)
    qm_ref[...] = _dot(x, wm_ref[:, 0:MLSTM_W]).astype(BF16)
    km_ref[...] = (_dot(x, wm_ref[:, MLSTM_W:2 * MLSTM_W]) * K_SCALE).astype(BF16)
    vm_ref[...] = _dot(x, wm_ref[:, 2 * MLSTM_W:3 * MLSTM_W]).astype(BF16)
    om_ref[...] = _dot(x, wm_ref[:, 3 * MLSTM_W:4 * MLSTM_W]).astype(BF16)
    g_ref[...] = _dot_nt(x, wg_ref[...])


def _in_proj_sample_kernel(x_ref, wd_ref, wm_ref, wg_ref,
                           q_ref, k_ref, v_ref, qm_ref, km_ref, vm_ref, om_ref, g_ref):
    x = x_ref[...].astype(BF16)
    q_ref[...] = _dot(x, wd_ref[:, 0:DIFF_W]) * Q_SCALE
    _store_head_rows(k_ref, _dot(x, wd_ref[:, DIFF_W:2 * DIFF_W]))
    _store_head_rows(v_ref, _dot(x, wd_ref[:, 2 * DIFF_W:3 * DIFF_W]))
    qm_ref[...] = _dot(x, wm_ref[:, 0:MLSTM_W])
    km_ref[...] = _dot(x, wm_ref[:, MLSTM_W:2 * MLSTM_W]) * K_SCALE
    vm_ref[...] = _dot(x, wm_ref[:, 2 * MLSTM_W:3 * MLSTM_W])
    om_ref[...] = _dot(x, wm_ref[:, 3 * MLSTM_W:4 * MLSTM_W])
    g_ref[...] = _dot_nt(x, wg_ref[...])


def _in_proj(x, wd, wm, wg, tm, prompt):
    t = x.shape[0]
    assert t % tm == 0
    row = lambda w: pl.BlockSpec((tm, w), lambda i: (i, 0))
    heads = pl.BlockSpec((H_D, tm, DK_D), lambda i: (0, i, 0))
    cache = pl.BlockSpec((tm * H_D, DK_D), lambda i: (i, 0))
    sds = jax.ShapeDtypeStruct
    if prompt:
        body = _in_proj_prompt_kernel
        heads_t = pl.BlockSpec((H_D, DK_D, tm), lambda i: (0, 0, i))
        out_specs = [heads_t, cache, heads, cache, heads_t] + [row(MLSTM_W)] * 4 + [row(GATE_PAD)]
        out_shape = [sds((H_D, DK_D, t), BF16), sds((t * H_D, DK_D), F32), sds((H_D, t, DK_D), BF16),
                     sds((t * H_D, DV_D), F32), sds((H_D, DV_D, t), BF16)]
        out_shape += [sds((t, MLSTM_W), BF16)] * 4 + [sds((t, GATE_PAD), F32)]
    else:
        body = _in_proj_sample_kernel
        out_specs = [row(DIFF_W), cache, cache] + [row(MLSTM_W)] * 4 + [row(GATE_PAD)]
        out_shape = ([sds((t, DIFF_W), F32)] + [sds((t * H_D, DK_D), F32)] * 2 + [sds((t, MLSTM_W), F32)] * 4
                     + [sds((t, GATE_PAD), F32)])
    return pl.pallas_call(
        body,
        grid=(t // tm,),
        in_specs=[row(D_MODEL), _resident_spec(wd.shape), _resident_spec(wm.shape), _resident_spec(wg.shape)],
        out_specs=out_specs,
        out_shape=out_shape,
        compiler_params=_params("parallel"),
        name="in_proj_prompt" if prompt else "in_proj_sample",
    )(x, wd, wm, wg)


def _lambda_value(lam_ref, lam_init):
    lp = lam_ref[...]
    e1 = jnp.exp(jnp.sum(lp[0:1] * lp[1:2], axis=1, keepdims=True))
    e2 = jnp.exp(jnp.sum(lp[2:3] * lp[3:4], axis=1, keepdims=True))
    return e1 - e2 + lam_init


def _head_rms(o, w):
    return o * lax.rsqrt(jnp.mean(o * o, axis=-1, keepdims=True) + RMS_EPS) * w


def _layer_norm(x, g, b):
    xc = x - jnp.mean(x, axis=-1, keepdims=True)
    var = jnp.mean(xc * xc, axis=-1, keepdims=True)
    return xc * lax.rsqrt(var + LN_EPS) * g + b


def _attn_prompt_kernel(lam_init, tq, nq, group, qt_ref, k_ref, vt_ref, pbias_ref, lam_ref, nw_ref, o_ref):
    i = pl.program_id(1)
    lam = _lambda_value(lam_ref, lam_init)
    chan = lax.broadcasted_iota(jnp.int32, (DK_D, tq), 0)
    first_map = chan < DH_HALF

    def fold8(op, s):
        return op(s.reshape(s.shape[0] // 8, 8, s.shape[1]), axis=0)

    def process(ii):
        n = (ii + 1) * tq
        n_far = max(n - 2 * tq, 0)

        def head(h):
            qt = qt_ref[h]
            zero = jnp.zeros_like(qt)
            q2t = jnp.concatenate([jnp.where(first_map, qt, zero), jnp.where(first_map, zero, qt)], axis=1)
            parts = []
            if n_far:
                parts.append(_dot(k_ref[h, 0:n_far, :], q2t))
            if ii >= 1:
                bs = pbias_ref[h, 0]
                parts.append(_dot(k_ref[h, n - 2 * tq:n - tq, :], q2t) + jnp.concatenate([bs, bs], axis=1))
            bd = pbias_ref[h, 1]
            parts.append(_dot(k_ref[h, n - tq:n, :], q2t) + jnp.concatenate([bd, bd], axis=1))
            m8 = fold8(jnp.max, parts[0])
            for s in parts[1:]:
                m8 = jnp.maximum(m8, fold8(jnp.max, s))
            m = jnp.max(m8, axis=0, keepdims=True)
            probs = [jnp.exp2(s - m) for s in parts]
            l8 = fold8(jnp.sum, probs[0])
            for p in probs[1:]:
                l8 = l8 + fold8(jnp.sum, p)
            l = jnp.sum(l8, axis=0, keepdims=True)
            p_all = jnp.concatenate([p.astype(BF16) for p in probs], axis=0)
            o = _dot(vt_ref[h, :, 0:n], p_all) / l
            o = o[:, :tq] - lam * o[:, tq:]
            o = o * lax.rsqrt(jnp.mean(o * o, axis=0, keepdims=True) + RMS_EPS) * (nw_ref[...] * (1.0 - lam_init))
            o_ref[h] = o.T.astype(o_ref.dtype)

        def head_group(g, carry):
            for u in range(group):
                head(g * group + u)
            return carry

        lax.fori_loop(0, H_D // group, head_group, 0)

    for ii in range(nq):
        pl.when(i == ii)(functools.partial(process, ii))


def _attn_prompt(qt, k, vt, pbias, lam_params, norm_w, batch, seq, blk, group, lam_init):
    nq = seq // blk
    assert H_D % group == 0
    return pl.pallas_call(
        functools.partial(_attn_prompt_kernel, lam_init, blk, nq, group),
        grid=(batch, nq),
        in_specs=[pl.BlockSpec((H_D, DK_D, blk), lambda b, i: (0, 0, b * nq + i)),
                  pl.BlockSpec((H_D, seq, DK_D), lambda b, i: (0, b, 0)),
                  pl.BlockSpec((H_D, DV_D, seq), lambda b, i: (0, 0, b)),
                  _const_spec(pbias.shape),
                  _const_spec(lam_params.shape),
                  _const_spec(norm_w.shape)],
        out_specs=pl.BlockSpec((H_D, blk, DV_D), lambda b, i: (0, b * nq + i, 0)),
        out_shape=jax.ShapeDtypeStruct((H_D, batch * seq, DV_D), BF16),
        compiler_params=_params("parallel", "parallel"),
        name="attn_prompt",
    )(qt, k, vt, pbias, lam_params, norm_w)


def _attn_sample_body(lam_init, ls, q_ref, kn_ref, vn_ref, sbias_ref, lam_ref, nw_ref, k_pages, v_pages, o_ref):
    n_pages = len(k_pages)
    rows = 2 * ls * H_D
    lam = _lambda_value(lam_ref, lam_init)

    q = q_ref[...]
    qt = jnp.concatenate([q] * (2 * H_D), axis=0)
    rowi = lax.broadcasted_iota(jnp.int32, (rows, DIFF_W), 0)
    coli = lax.broadcasted_iota(jnp.int32, (rows, DIFF_W), 1)
    q_bd = jnp.where(coli // DH_HALF == rowi // ls, qt, 0.0).astype(BF16)

    pad = jnp.zeros((PAGE_SIZE - ls, DIFF_W), F32)

    def page(ref):
        return _load_head_rows(ref, PAGE_SIZE).astype(BF16)

    k_new = jnp.concatenate([_load_head_rows(kn_ref, ls), pad], axis=0).astype(BF16)
    v_new = jnp.concatenate([_load_head_rows(vn_ref, ls), pad], axis=0).astype(BF16)

    k_all = jnp.concatenate([page(r) for r in k_pages] + [k_new], axis=0)
    n_keys = k_all.shape[0]
    s = _dot_nt(q_bd, k_all) + sbias_ref[...]
    m = jnp.max(s, axis=1, keepdims=True)
    p = jnp.exp(s - m)
    l = jnp.sum(p, axis=1, keepdims=True)
    second_map = (lax.broadcasted_iota(jnp.int32, (rows, 1), 0) // ls) % 2 == 1
    fac = jnp.where(second_map, -lam, 1.0) / l
    p3 = (p * fac).reshape(H_D, 2 * ls, n_keys)
    w = (p3[:, :ls] + p3[:, ls:]).reshape(H_D * ls, n_keys).astype(BF16)
    v_all = jnp.concatenate([page(r) for r in v_pages] + [v_new], axis=0)
    acc = _dot(w, v_all)
    for h in range(H_D):
        o = acc[h * ls:(h + 1) * ls, h * DV_D:(h + 1) * DV_D]
        o_ref[:, h * DV_D:(h + 1) * DV_D] = _head_rms(o, nw_ref[...]) * (1.0 - lam_init)


def _mlstm_chunk(q, k, v, o_pre, i_pre, f_pre, c_prev, n_prev, m_prev, norm_w):
    L = q.shape[0]
    t_idx = lax.broadcasted_iota(jnp.int32, (L, L), 0)
    s_idx = lax.broadcasted_iota(jnp.int32, (L, L), 1)
    causal = s_idx <= t_idx
    eye = s_idx == t_idx

    def to_row(col):
        return jnp.sum(jnp.where(eye, col, 0.0), axis=0, keepdims=True)

    it_col = i_pre
    lf_col = jax.nn.log_sigmoid(f_pre)
    it_row = to_row(it_col)
    b_row = jnp.sum(jnp.where(t_idx <= s_idx, lf_col, 0.0), axis=0, keepdims=True)
    b_col = jnp.sum(jnp.where(causal, to_row(lf_col), 0.0), axis=1, keepdims=True)

    log_d = jnp.where(causal, b_col - b_row + it_row, -jnp.inf)
    m_t = jnp.maximum(b_col + m_prev, jnp.max(log_d, axis=1, keepdims=True))
    dmat = jnp.exp(log_d - m_t)
    inter = jnp.exp(b_col + m_prev - m_t)

    w = _dot_nt(q, k) * dmat
    kf = k.astype(F32)
    num = inter * _dot_nt(q, c_prev.astype(BF16)) + _dot(w.astype(BF16), v)
    den = inter * jnp.sum(q.astype(F32) * n_prev, axis=1, keepdims=True) + jnp.sum(w, axis=1, keepdims=True)
    h = num / jnp.maximum(jnp.abs(den), jnp.exp(-m_t))
    h = _head_rms(h, norm_w) * jax.nn.sigmoid(o_pre.astype(F32))

    m_new = m_t[L - 1:L]
    b_last = b_col[L - 1:L]
    g = jnp.exp(b_last - b_col + it_col - m_new)
    decay = jnp.exp(b_last + m_prev - m_new)
    gk = g * kf
    c_new = decay * c_prev + _dot_tn(v, gk.astype(BF16))
    n_new = decay * n_prev + jnp.sum(gk, axis=0, keepdims=True)
    return h, c_new, n_new, m_new


def _mlstm_prompt_kernel(bg_ref, q_ref, k_ref, v_ref, o_ref, g_ref, nw_ref, h_ref, c_ref, n_ref, m_ref):
    @pl.when(pl.program_id(1) == 0)
    def _():
        c_ref[...] = jnp.zeros_like(c_ref)
        n_ref[...] = jnp.zeros_like(n_ref)
        m_ref[...] = jnp.zeros_like(m_ref)

    for h in range(H_M):
        cols = slice(h * DH_M, (h + 1) * DH_M)
        out, c_new, n_new, m_new = _mlstm_chunk(
            q_ref[:, cols], k_ref[:, cols], v_ref[:, cols], o_ref[:, cols],
            g_ref[:, h:h + 1] + bg_ref[h], g_ref[:, H_M + h:H_M + h + 1] + bg_ref[H_M + h],
            c_ref[0, h], n_ref[0, h:h + 1, :], m_ref[0, h:h + 1, 0:1], nw_ref[:, cols])
        h_ref[:, cols] = out.astype(h_ref.dtype)
        c_ref[0, h] = c_new
        n_ref[0, h:h + 1, :] = n_new
        m_ref[0, h:h + 1, :] = jnp.broadcast_to(m_new, (1, m_ref.shape[2]))


def _mlstm_prompt(b_gates, qm, km, vm, om, gates, norm_w, batch, seq, chunk):
    nc = seq // chunk
    tok = lambda w: pl.BlockSpec((chunk, w), lambda b, c: (b * nc + c, 0))
    sds = jax.ShapeDtypeStruct
    return pl.pallas_call(
        _mlstm_prompt_kernel,
        grid=(batch, nc),
        in_specs=[pl.BlockSpec(memory_space=pltpu.SMEM),
                  tok(MLSTM_W), tok(MLSTM_W), tok(MLSTM_W), tok(MLSTM_W), tok(GATE_PAD),
                  _const_spec(norm_w.shape)],
        out_specs=[tok(MLSTM_W),
                   pl.BlockSpec((1, H_M, DH_M, DH_M), lambda b, c: (b, 0, 0, 0)),
                   pl.BlockSpec((1, H_M, DH_M), lambda b, c: (b, 0, 0)),
                   pl.BlockSpec((1, H_M, 128), lambda b, c: (b, 0, 0))],
        out_shape=[sds((batch * seq, MLSTM_W), BF16), sds((batch, H_M, DH_M, DH_M), F32),
                   sds((batch, H_M, DH_M), F32), sds((batch, H_M, 128), F32)],
        compiler_params=_params("parallel", "arbitrary"),
        name="mlstm_prompt",
    )(b_gates, qm, km, vm, om, gates, norm_w)


def _mlstm_sample_kernel(ls, group, bg_ref, q_ref, k_ref, v_ref, o_ref, g_ref, nw_ref, c_in, n_in, m_in,
                         h_ref, c_ref, n_ref, m_ref):
    pair = 2 if group % 2 == 0 else 1

    def seq_pair(s2, carry):
        for u in range(pair):
            seq(s2 * pair + u)
        return carry

    def seq(s):
        rows = pl.ds(pl.multiple_of(s * ls, ls), ls)
        for h in range(H_M):
            cols = slice(h * DH_M, (h + 1) * DH_M)
            out, c_new, n_new, m_new = _mlstm_chunk(
                q_ref[rows, cols].astype(BF16), k_ref[rows, cols].astype(BF16), v_ref[rows, cols].astype(BF16),
                o_ref[rows, cols],
                g_ref[rows, h:h + 1] + bg_ref[h], g_ref[rows, H_M + h:H_M + h + 1] + bg_ref[H_M + h],
                c_in[s, h], n_in[s, h:h + 1, :], m_in[s, h:h + 1, 0:1], nw_ref[:, cols])
            h_ref[rows, cols] = out
            c_ref[s, h] = c_new
            n_ref[s, h:h + 1, :] = n_new
            m_ref[s, h:h + 1, :] = jnp.broadcast_to(m_new, (1, m_ref.shape[2]))

    lax.fori_loop(0, group // pair, seq_pair, 0)


def _mlstm_sample(b_gates, qm, km, vm, om, gates, norm_w, state_c, state_n, state_m, ls, group):
    bs = state_c.shape[0]
    assert bs % group == 0
    tok = lambda w: pl.BlockSpec((group * ls, w), lambda i: (i, 0))
    c_spec = pl.BlockSpec((group, H_M, DH_M, DH_M), lambda i: (i, 0, 0, 0))
    n_spec = pl.BlockSpec((group, H_M, DH_M), lambda i: (i, 0, 0))
    m_spec = pl.BlockSpec((group, H_M, 128), lambda i: (i, 0, 0))
    sds = jax.ShapeDtypeStruct
    return pl.pallas_call(
        functools.partial(_mlstm_sample_kernel, ls, group),
        grid=(bs // group,),
        in_specs=[pl.BlockSpec(memory_space=pltpu.SMEM),
                  tok(MLSTM_W), tok(MLSTM_W), tok(MLSTM_W), tok(MLSTM_W), tok(GATE_PAD),
                  _const_spec(norm_w.shape), c_spec, n_spec, m_spec],
        out_specs=[tok(MLSTM_W), c_spec, n_spec, m_spec],
        out_shape=[sds((bs * ls, MLSTM_W), F32), sds(state_c.shape, F32),
                   sds(state_n.shape, F32), sds((bs, H_M, 128), F32)],
        compiler_params=_params("parallel"),
        name="mlstm_sample",
    )(b_gates, qm, km, vm, om, gates, norm_w, state_c, state_n, state_m)


def _mix_kernel(head_major, x_ref, hm_ref, hd_ref, w_ref, g_ref, b_ref, y_ref, yb_ref):
    if head_major:
        hd = jnp.concatenate([hd_ref[h] for h in range(H_D)], axis=-1)
    else:
        hd = hd_ref[...]
    mix = _dot(hm_ref[...].astype(BF16), w_ref[0:MLSTM_W, :]) + _dot(hd.astype(BF16), w_ref[MLSTM_W:, :])
    y = _layer_norm(ALPHA * x_ref[...] + mix, g_ref[...], b_ref[...])
    y_ref[...] = y
    yb_ref[...] = y.astype(BF16)


def _mix(x, hm, hd, w_out, ln_g, ln_b, tm, head_major):
    t = x.shape[0]
    assert t % tm == 0
    row = lambda w: pl.BlockSpec((tm, w), lambda i: (i, 0))
    hd_spec = pl.BlockSpec((H_D, tm, DV_D), lambda i: (0, i, 0)) if head_major else row(DIFF_W)
    return pl.pallas_call(
        functools.partial(_mix_kernel, head_major),
        grid=(t // tm,),
        in_specs=[row(D_MODEL), row(MLSTM_W), hd_spec, _resident_spec(w_out.shape),
                  _const_spec(ln_g.shape), _const_spec(ln_b.shape)],
        out_specs=[row(D_MODEL), row(D_MODEL)],
        out_shape=[jax.ShapeDtypeStruct((t, D_MODEL), F32), jax.ShapeDtypeStruct((t, D_MODEL), BF16)],
        compiler_params=_params("parallel"),
        name="mix_ln",
    )(x, hm, hd, w_out, ln_g, ln_b)


def _ffn_up_math(xb_ref, wg_ref, wu_ref, h_ref):
    xb = xb_ref[...]
    h_ref[...] = (jax.nn.silu(_dot(xb, wg_ref[...])) * _dot(xb, wu_ref[...])).astype(h_ref.dtype)


def _ffn_up_kernel(xb_ref, wg_ref, wu_ref, h_ref):
    _ffn_up_math(xb_ref, wg_ref, wu_ref, h_ref)


def _ffn_up(xb, w_gate, w_up, tm, tf):
    t = xb.shape[0]
    assert t % tm == 0 and D_FF % tf == 0
    return pl.pallas_call(
        _ffn_up_kernel,
        grid=(t // tm, D_FF // tf),
        in_specs=[pl.BlockSpec((tm, D_MODEL), lambda i, j: (i, 0)),
                  pl.BlockSpec((D_MODEL, tf), lambda i, j: (0, j)),
                  pl.BlockSpec((D_MODEL, tf), lambda i, j: (0, j))],
        out_specs=pl.BlockSpec((tm, tf), lambda i, j: (i, j)),
        out_shape=jax.ShapeDtypeStruct((t, D_FF), BF16),
        compiler_params=_params("parallel", "parallel"),
        name="ffn_up",
    )(xb, w_gate, w_up)


def _ffn_up_attn_kernel(lam_init, n_pages, ls, n_seq, pt_ref, xb_ref, wg_ref, wu_ref,
                        q_ref, kn_ref, vn_ref, sbias_ref, lam_ref, nw_ref, *rest):
    del pt_ref
    k_pages = rest[:n_pages]
    v_pages = rest[n_pages:2 * n_pages]
    h_ref, o_ref = rest[2 * n_pages:]
    _ffn_up_math(xb_ref, wg_ref, wu_ref, h_ref)
    step = pl.program_id(0) * pl.num_programs(1) + pl.program_id(1)

    @pl.when(step < n_seq)
    def _():
        _attn_sample_body(lam_init, ls, q_ref, kn_ref, vn_ref, sbias_ref, lam_ref, nw_ref, k_pages, v_pages, o_ref)


def _ffn_up_attn_sample(xb, w_gate, w_up, tm, tf, page_table, q, k_new, v_new, cache_k, cache_v, sbias,
                        lam_params, norm_w, ls, lam_init):
    t = xb.shape[0]
    bs, n_pages = page_table.shape
    n_i, n_j = t // tm, D_FF // tf
    assert t % tm == 0 and D_FF % tf == 0 and n_i * n_j >= bs
    rows_per_page = PAGE_SIZE * H_D
    cache_k = cache_k.reshape(-1, DK_D)
    cache_v = cache_v.reshape(-1, DV_D)

    def seq(i, j):
        return jnp.minimum(i * n_j + j, bs - 1)

    seq_spec = pl.BlockSpec((ls, DIFF_W), lambda i, j, pt: (seq(i, j), 0))
    new_spec = pl.BlockSpec((ls * H_D, DK_D), lambda i, j, pt: (seq(i, j), 0))

    def page_spec(p):
        return pl.BlockSpec((rows_per_page, DK_D), lambda i, j, pt: (pt[seq(i, j), p], 0))

    def const(shape):
        return pl.BlockSpec(shape, lambda i, j, pt: (0,) * len(shape))

    grid_spec = pltpu.PrefetchScalarGridSpec(
        num_scalar_prefetch=1,
        grid=(n_i, n_j),
        in_specs=[pl.BlockSpec((tm, D_MODEL), lambda i, j, pt: (i, 0)),
                  pl.BlockSpec((D_MODEL, tf), lambda i, j, pt: (0, j)),
                  pl.BlockSpec((D_MODEL, tf), lambda i, j, pt: (0, j)),
                  seq_spec, new_spec, new_spec,
                  const(sbias.shape), const(lam_params.shape), const(norm_w.shape)]
        + [page_spec(p) for p in range(n_pages)] * 2,
        out_specs=[pl.BlockSpec((tm, tf), lambda i, j, pt: (i, j)), seq_spec],
    )
    return pl.pallas_call(
        functools.partial(_ffn_up_attn_kernel, lam_init, n_pages, ls, bs),
        grid_spec=grid_spec,
        out_shape=[jax.ShapeDtypeStruct((t, D_FF), BF16), jax.ShapeDtypeStruct((bs * ls, DIFF_W), F32)],
        compiler_params=_params("arbitrary", "arbitrary"),
        name="ffn_up_attn_sample",
    )(page_table, xb, w_gate, w_up, q, k_new, v_new, sbias, lam_params, norm_w,
      *([cache_k] * n_pages), *([cache_v] * n_pages))


def _ffn_down_kernel(x_ref, h_ref, wd_ref, g_ref, b_ref, y_ref):
    y_ref[...] = _layer_norm(ALPHA * x_ref[...] + _dot(h_ref[...], wd_ref[...]), g_ref[...], b_ref[...])


def _ffn_down(x, h, w_down, ln_g, ln_b, tm):
    t = x.shape[0]
    assert t % tm == 0
    row = lambda w: pl.BlockSpec((tm, w), lambda i: (i, 0))
    return pl.pallas_call(
        _ffn_down_kernel,
        grid=(t // tm,),
        in_specs=[row(D_MODEL), row(D_FF), _resident_spec(w_down.shape),
                  _const_spec(ln_g.shape), _const_spec(ln_b.shape)],
        out_specs=row(D_MODEL),
        out_shape=jax.ShapeDtypeStruct((t, D_MODEL), F32),
        compiler_params=_params("parallel"),
        name="ffn_down_ln",
    )(x, h, w_down, ln_g, ln_b)


def _tiles(seq):
    return dict(
        cast_cols=512,
        proj_tm=256,
        attn_blk=min(256, seq),
        attn_group=4,
        mlstm_chunk=min(512, seq),
        sample_group=4,
        mix_tm=256,
        ffn_up_tm=1024,
        ffn_up_tf=256,
        ffn_down_tm=256,
    )


def kernel(x_prompt, x_sample, cache_k, cache_v, state_C, state_n, state_m, page_table, rel_bias, w_in, b_gates, lambda_q1, lambda_k1, lambda_q2, lambda_k2, diff_norm_w, mlstm_norm_w, w_out, ln1_g, ln1_b, w_gate, w_up, w_down, ln2_g, ln2_b):
    B, S, _ = x_prompt.shape
    Bs, Ls, _ = x_sample.shape
    n_pages = page_table.shape[1]
    past = n_pages * cache_k.shape[2]
    assert w_in.shape[0] == DEPTH == 1 and cache_k.shape[2] == PAGE_SIZE
    tl = _tiles(S)
    l = 0
    lam_init = 0.8 - 0.6 * math.exp(-0.3 * l)

    wt = w_in[l].T
    wd = _transpose_cast(wt, 0, DIFF_COLS, tl["cast_cols"])
    wm = _transpose_cast(wt, DIFF_COLS, MLSTM_COLS, tl["cast_cols"])
    wg = _gate_rows(wt, DIFF_COLS + MLSTM_COLS)
    w_o = w_out[l].astype(BF16)
    w_g, w_u, w_d = w_gate[l].astype(BF16), w_up[l].astype(BF16), w_down[l].astype(BF16)
    lam_params = jnp.stack([lambda_q1[l], lambda_k1[l], lambda_q2[l], lambda_k2[l]], 0)
    dnw = diff_norm_w[l].reshape(1, DV_D)
    mnw = mlstm_norm_w[l].reshape(1, MLSTM_W)
    g1, b1 = ln1_g[l].reshape(1, D_MODEL), ln1_b[l].reshape(1, D_MODEL)
    g2, b2 = ln2_g[l].reshape(1, D_MODEL), ln2_b[l].reshape(1, D_MODEL)
    bg = b_gates[l]

    pbias, sbias = _bias_tables(rel_bias, tl["attn_blk"], past, Ls)
    sbias = sbias.reshape(H_D * 2 * Ls, past + PAGE_SIZE)

    xp = x_prompt.reshape(B * S, D_MODEL)
    qt, k32, k16, v32, vt, qm, km, vm, om, gates = _in_proj(xp, wd, wm, wg, tl["proj_tm"], True)
    hd = _attn_prompt(qt, k16, vt, pbias, lam_params, dnw.reshape(DV_D, 1), B, S, tl["attn_blk"],
                      tl["attn_group"], lam_init)
    hm, c_p, n_p, m_p = _mlstm_prompt(bg, qm, km, vm, om, gates, mnw, B, S, tl["mlstm_chunk"])
    x1, x1b = _mix(xp, hm, hd, w_o, g1, b1, tl["mix_tm"], True)

    xs = x_sample.reshape(Bs * Ls, D_MODEL)
    qs, ks, vs, qms, kms, vms, oms, gates_s = _in_proj(xs, wd, wm, wg, tl["proj_tm"], False)
    up_p, hd_s = _ffn_up_attn_sample(x1b, w_g, w_u, min(tl["ffn_up_tm"], B * S), tl["ffn_up_tf"], page_table,
                                     qs, ks, vs, cache_k, cache_v, sbias, lam_params, dnw, Ls, lam_init)
    y_p = _ffn_down(x1, up_p, w_d, g2, b2, tl["ffn_down_tm"])
    m_in = jnp.broadcast_to(state_m[l][:, :, None], (Bs, H_M, 128))
    hm_s, c_s, n_s, m_s = _mlstm_sample(bg, qms, kms, vms, oms, gates_s, mnw,
                                        state_C[l], state_n[l], m_in, Ls, tl["sample_group"])
    x1s, x1sb = _mix(xs, hm_s, hd_s, w_o, g1, b1, tl["mix_tm"], False)
    up_s = _ffn_up(x1sb, w_g, w_u, min(tl["ffn_up_tm"], Bs * Ls), tl["ffn_up_tf"])
    y_s = _ffn_down(x1s, up_s, w_d, g2, b2, tl["ffn_down_tm"])

    return (y_p.reshape(B, S, D_MODEL), y_s.reshape(Bs, Ls, D_MODEL),
            k32.reshape(1, B, S, H_D, DK_D), v32.reshape(1, B, S, H_D, DV_D),
            c_p[None], n_p[None], m_p[None, :, :, 0],
            ks.reshape(1, Bs, Ls, H_D, DK_D), vs.reshape(1, Bs, Ls, H_D, DV_D),
            c_s[None], n_s[None], m_s[None, :, :, 0])
```

```python
import functools
import math

import numpy as np
import jax
import jax.numpy as jnp
from jax import lax
from jax.experimental import pallas as pl
from jax.experimental.pallas import tpu as pltpu

D_MODEL = 2048
DEPTH = 1
PAGE_SIZE = 128
H_D = 8
DH_HALF = 64
DK_D = 2 * DH_HALF
DV_D = 2 * DH_HALF
DIFF_W = H_D * DV_D
H_M = 4
DH_M = 256
MLSTM_W = H_M * DH_M
D_FF = -(-8 * D_MODEL // (3 * 256)) * 256
N_BUCKETS = 32
MAX_DIST = 128
ALPHA = (2 * DEPTH) ** 0.25
LN_EPS = 1e-5
RMS_EPS = 1e-6
N_GATES = 2 * H_M
GATE_PAD = 128
DIFF_COLS = 3 * DIFF_W
MLSTM_COLS = 4 * MLSTM_W
MASK_VALUE = -1e30
LOG2E = math.log2(math.e)

VMEM_LIMIT_BYTES = 56 * 1024 * 1024

BF16 = jnp.bfloat16
F32 = jnp.float32


def _dot(a, b):
    return jnp.dot(a, b, preferred_element_type=F32)


def _dot_nt(a, b):
    return lax.dot_general(a, b, (((1,), (1,)), ((), ())), preferred_element_type=F32)


def _dot_tn(a, b):
    return lax.dot_general(a, b, (((0,), (0,)), ((), ())), preferred_element_type=F32)


def _params(*semantics):
    return pltpu.CompilerParams(dimension_semantics=semantics, vmem_limit_bytes=VMEM_LIMIT_BYTES)


def _const_spec(shape):
    n = len(shape)
    return pl.BlockSpec(shape, lambda *_: (0,) * n)


def _resident_spec(shape):
    n = len(shape)
    return pl.BlockSpec(shape, lambda *_: (0,) * n, pipeline_mode=pl.Buffered(1))


def _bucket_np(dist):
    n = np.maximum(dist, 0)
    max_exact = N_BUCKETS // 2
    nf = np.maximum(n, 1).astype(np.float32)
    large = max_exact + (np.log(nf / np.float32(max_exact)) / np.float32(math.log(MAX_DIST / max_exact))
                         * np.float32(N_BUCKETS - max_exact)).astype(np.int32)
    large = np.minimum(large, N_BUCKETS - 1)
    out = np.where(n < max_exact, n, large).astype(np.int32)
    return np.where(dist < 0, -1, out).astype(np.int32)


def _bias_kernel(rb_ref, pb_ref, sb_ref, pbias_ref, sbias_ref):
    h = pl.program_id(0)

    def lookup(bk, shift, scale):
        acc = jnp.full(bk.shape, MASK_VALUE, F32)
        for b in range(N_BUCKETS):
            acc = jnp.where(bk == b, (rb_ref[b, h] - shift) * scale, acc)
        return acc

    for t in range(pb_ref.shape[0]):
        pbias_ref[0, t] = lookup(pb_ref[t], rb_ref[N_BUCKETS - 1, h], LOG2E)
    sbias_ref[0] = lookup(sb_ref[...], 0.0, 1.0)


def _bias_tables(rel_bias, blk, past, dec_seq):
    k = np.arange(blk)[:, None]
    q = np.arange(blk)[None, :]
    pb = np.stack([_bucket_np(blk + q - k), _bucket_np(q - k)], 0)
    far = N_BUCKETS - 1
    assert (_bucket_np(np.arange(MAX_DIST, 4 * past + 4 * blk)) == far).all()
    assert blk >= MAX_DIST and PAGE_SIZE >= MAX_DIST
    qpos = past + np.arange(dec_seq)[:, None]
    last = _bucket_np(qpos - (past - PAGE_SIZE + np.arange(PAGE_SIZE))[None, :])
    new = _bucket_np(qpos - (past + np.arange(PAGE_SIZE))[None, :])
    new[:, dec_seq:] = -1
    sb = np.concatenate([np.full((dec_seq, past - PAGE_SIZE), far, np.int32), last, new], 1)
    sb = np.concatenate([sb, sb], 0)
    return pl.pallas_call(
        _bias_kernel,
        grid=(H_D,),
        in_specs=[pl.BlockSpec(memory_space=pltpu.SMEM),
                  _const_spec(pb.shape), _const_spec(sb.shape)],
        out_specs=[pl.BlockSpec((1,) + pb.shape, lambda h: (h, 0, 0, 0)),
                   pl.BlockSpec((1,) + sb.shape, lambda h: (h, 0, 0))],
        out_shape=[jax.ShapeDtypeStruct((H_D,) + pb.shape, F32),
                   jax.ShapeDtypeStruct((H_D,) + sb.shape, F32)],
        compiler_params=_params("arbitrary"),
        name="bias_tables",
    )(rel_bias, jnp.asarray(pb), jnp.asarray(sb))


Q_SCALE = DH_HALF ** -0.5
K_SCALE = DH_M ** -0.5


def _head_rows(ref, h, n_tokens):
    return ref[pl.ds(h, n_tokens, stride=H_D), :]


def _store_head_rows(ref, x):
    for h in range(H_D):
        ref[pl.ds(h, x.shape[0], stride=H_D), :] = x[:, h * DK_D:(h + 1) * DK_D]


def _load_head_rows(ref, n_tokens):
    return jnp.concatenate([_head_rows(ref, h, n_tokens) for h in range(H_D)], axis=1)


def _transpose_cast_kernel(wt_ref, w_ref):
    w_ref[...] = wt_ref[...].T.astype(BF16)


def _transpose_cast(wt, first_col, n_cols, blk):
    d = wt.shape[1]
    assert first_col % blk == 0 and n_cols % blk == 0
    return pl.pallas_call(
        _transpose_cast_kernel,
        grid=(n_cols // blk,),
        in_specs=[pl.BlockSpec((blk, d), lambda j: (first_col // blk + j, 0))],
        out_specs=pl.BlockSpec((d, blk), lambda j: (0, j)),
        out_shape=jax.ShapeDtypeStruct((d, n_cols), BF16),
        compiler_params=_params("parallel"),
        name="transpose_cast",
    )(wt)


def _gate_rows_kernel(wt_ref, o_ref):
    zeros = jnp.zeros((GATE_PAD - N_GATES, wt_ref.shape[1]), F32)
    o_ref[...] = jnp.concatenate([wt_ref[...], zeros], axis=0).astype(BF16)


def _gate_rows(wt, first_row):
    d = wt.shape[1]
    assert first_row % N_GATES == 0
    return pl.pallas_call(
        _gate_rows_kernel,
        grid=(1,),
        in_specs=[pl.BlockSpec((N_GATES, d), lambda i: (first_row // N_GATES, 0))],
        out_specs=pl.BlockSpec((GATE_PAD, d), lambda i: (0, 0)),
        out_shape=jax.ShapeDtypeStruct((GATE_PAD, d), BF16),
        compiler_params=_params("arbitrary"),
        name="gate_rows",
    )(wt)


def _in_proj_prompt_kernel(x_ref, wd_ref, wm_ref, wg_ref,
                           qt_ref, k32_ref, k16_ref, v32_ref, vt_ref,
                           qm_ref, km_ref, vm_ref, om_ref, g_ref):
    x = x_ref[...].astype(BF16)
    r = _dot(x, wd_ref[:, 0:DIFF_W]) * (Q_SCALE * LOG2E)
    for h in range(H_D):
        qt_ref[h] = r[:, h * DK_D:(h + 1) * DK_D].T.astype(BF16)
    r = _dot(x, wd_ref[:, DIFF_W:2 * DIFF_W])
    _store_head_rows(k32_ref, r)
    for h in range(H_D):
        k16_ref[h] = r[:, h * DK_D:(h + 1) * DK_D].astype(BF16)
    r = _dot(x, wd_ref[:, 2 * DIFF_W:3 * DIFF_W])
    _store_head_rows(v32_ref, r)
    for h in range(H_D):
        vt_ref[h] = r[:, h * DV_D:(h + 1) * DV_D].T.astype(BF16)
    qm_ref[...] = _dot(x, wm_ref[:, 0:MLSTM_W]).astype(BF16)
    km_ref[...] = (_dot(x, wm_ref[:, MLSTM_W:2 * MLSTM_W]) * K_SCALE).astype(BF16)
    vm_ref[...] = _dot(x, wm_ref[:, 2 * MLSTM_W:3 * MLSTM_W]).astype(BF16)
    om_ref[...] = _dot(x, wm_ref[:, 3 * MLSTM_W:4 * MLSTM_W]).astype(BF16)
    g_ref[...] = _dot_nt(x, wg_ref[...])


def _in_proj_sample_kernel(x_ref, wd_ref, wm_ref, wg_ref,
                           q_ref, k_ref, v_ref, qm_ref, km_ref, vm_ref, om_ref, g_ref):
    x = x_ref[...].astype(BF16)
    q_ref[...] = _dot(x, wd_ref[:, 0:DIFF_W]) * Q_SCALE
    _store_head_rows(k_ref, _dot(x, wd_ref[:, DIFF_W:2 * DIFF_W]))
    _store_head_rows(v_ref, _dot(x, wd_ref[:, 2 * DIFF_W:3 * DIFF_W]))
    qm_ref[...] = _dot(x, wm_ref[:, 0:MLSTM_W])
    km_ref[...] = _dot(x, wm_ref[:, MLSTM_W:2 * MLSTM_W]) * K_SCALE
    vm_ref[...] = _dot(x, wm_ref[:, 2 * MLSTM_W:3 * MLSTM_W])
    om_ref[...] = _dot(x, wm_ref[:, 3 * MLSTM_W:4 * MLSTM_W])
    g_ref[...] = _dot_nt(x, wg_ref[...])


def _in_proj(x, wd, wm, wg, tm, prompt):
    t = x.shape[0]
    assert t % tm == 0
    row = lambda w: pl.BlockSpec((tm, w), lambda i: (i, 0))
    heads = pl.BlockSpec((H_D, tm, DK_D), lambda i: (0, i, 0))
    cache = pl.BlockSpec((tm * H_D, DK_D), lambda i: (i, 0))
    sds = jax.ShapeDtypeStruct
    if prompt:
        body = _in_proj_prompt_kernel
        heads_t = pl.BlockSpec((H_D, DK_D, tm), lambda i: (0, 0, i))
        out_specs = [heads_t, cache, heads, cache, heads_t] + [row(MLSTM_W)] * 4 + [row(GATE_PAD)]
        out_shape = [sds((H_D, DK_D, t), BF16), sds((t * H_D, DK_D), F32), sds((H_D, t, DK_D), BF16),
                     sds((t * H_D, DV_D), F32), sds((H_D, DV_D, t), BF16)]
        out_shape += [sds((t, MLSTM_W), BF16)] * 4 + [sds((t, GATE_PAD), F32)]
    else:
        body = _in_proj_sample_kernel
        out_specs = [row(DIFF_W), cache, cache] + [row(MLSTM_W)] * 4 + [row(GATE_PAD)]
        out_shape = ([sds((t, DIFF_W), F32)] + [sds((t * H_D, DK_D), F32)] * 2 + [sds((t, MLSTM_W), F32)] * 4
                     + [sds((t, GATE_PAD), F32)])
    return pl.pallas_call(
        body,
        grid=(t // tm,),
        in_specs=[row(D_MODEL), _resident_spec(wd.shape), _resident_spec(wm.shape), _resident_spec(wg.shape)],
        out_specs=out_specs,
        out_shape=out_shape,
        compiler_params=_params("parallel"),
        name="in_proj_prompt" if prompt else "in_proj_sample",
    )(x, wd, wm, wg)


def _lambda_value(lam_ref, lam_init):
    lp = lam_ref[...]
    e1 = jnp.exp(jnp.sum(lp[0:1] * lp[1:2], axis=1, keepdims=True))
    e2 = jnp.exp(jnp.sum(lp[2:3] * lp[3:4], axis=1, keepdims=True))
    return e1 - e2 + lam_init


def _head_rms(o, w):
    return o * lax.rsqrt(jnp.mean(o * o, axis=-1, keepdims=True) + RMS_EPS) * w


def _layer_norm(x, g, b):
    xc = x - jnp.mean(x, axis=-1, keepdims=True)
    var = jnp.mean(xc * xc, axis=-1, keepdims=True)
    return xc * lax.rsqrt(var + LN_EPS) * g + b


def _attn_prompt_kernel(lam_init, tq, nq, group, qt_ref, k_ref, vt_ref, pbias_ref, lam_ref, nw_ref, o_ref):
    i = pl.program_id(1)
    lam = _lambda_value(lam_ref, lam_init)
    chan = lax.broadcasted_iota(jnp.int32, (DK_D, tq), 0)
    first_map = chan < DH_HALF

    def fold8(op, s):
        return op(s.reshape(s.shape[0] // 8, 8, s.shape[1]), axis=0)

    def process(ii):
        n = (ii + 1) * tq
        n_far = max(n - 2 * tq, 0)

        def scores(h):
            qt = qt_ref[h]
            zero = jnp.zeros_like(qt)
            q2t = jnp.concatenate([jnp.where(first_map, qt, zero), jnp.where(first_map, zero, qt)], axis=1)
            parts = []
            if n_far:
                parts.append(_dot(k_ref[h, 0:n_far, :], q2t))
            if ii >= 1:
                bs = pbias_ref[h, 0]
                parts.append(_dot(k_ref[h, n - 2 * tq:n - tq, :], q2t) + jnp.concatenate([bs, bs], axis=1))
            bd = pbias_ref[h, 1]
            parts.append(_dot(k_ref[h, n - tq:n, :], q2t) + jnp.concatenate([bd, bd], axis=1))
            return parts

        def softmax(parts):
            m8 = fold8(jnp.max, parts[0])
            for s in parts[1:]:
                m8 = jnp.maximum(m8, fold8(jnp.max, s))
            m = jnp.max(m8, axis=0, keepdims=True)
            probs = [jnp.exp2(s - m) for s in parts]
            l8 = fold8(jnp.sum, probs[0])
            for p in probs[1:]:
                l8 = l8 + fold8(jnp.sum, p)
            l = jnp.sum(l8, axis=0, keepdims=True)
            return jnp.concatenate([p.astype(BF16) for p in probs], axis=0), l

        def values(h, p_all, l):
            o = _dot(vt_ref[h, :, 0:n], p_all) / l
            o = o[:, :tq] - lam * o[:, tq:]
            o = o * lax.rsqrt(jnp.mean(o * o, axis=0, keepdims=True) + RMS_EPS) * (nw_ref[...] * (1.0 - lam_init))
            o_ref[h] = o.T.astype(o_ref.dtype)

        def head_group(g, carry):
            hs = [g * group + u for u in range(group)]
            sc, sm = {}, {}
            for t in range(group + 2):
                if t < group:
                    sc[t] = scores(hs[t])
                if 0 <= t - 1 < group:
                    sm[t - 1] = softmax(sc.pop(t - 1))
                if 0 <= t - 2 < group:
                    values(hs[t - 2], *sm.pop(t - 2))
            return carry

        lax.fori_loop(0, H_D // group, head_group, 0)

    for ii in range(nq):
        pl.when(i == ii)(functools.partial(process, ii))


def _attn_prompt(qt, k, vt, pbias, lam_params, norm_w, batch, seq, blk, group, lam_init):
    nq = seq // blk
    assert H_D % group == 0
    return pl.pallas_call(
        functools.partial(_attn_prompt_kernel, lam_init, blk, nq, group),
        grid=(batch, nq),
        in_specs=[pl.BlockSpec((H_D, DK_D, blk), lambda b, i: (0, 0, b * nq + i)),
                  pl.BlockSpec((H_D, seq, DK_D), lambda b, i: (0, b, 0)),
                  pl.BlockSpec((H_D, DV_D, seq), lambda b, i: (0, 0, b)),
                  _const_spec(pbias.shape),
                  _const_spec(lam_params.shape),
                  _const_spec(norm_w.shape)],
        out_specs=pl.BlockSpec((H_D, blk, DV_D), lambda b, i: (0, b * nq + i, 0)),
        out_shape=jax.ShapeDtypeStruct((H_D, batch * seq, DV_D), BF16),
        compiler_params=_params("parallel", "parallel"),
        name="attn_prompt",
    )(qt, k, vt, pbias, lam_params, norm_w)


def _attn_sample_body(lam_init, ls, q_ref, kn_ref, vn_ref, sbias_ref, lam_ref, nw_ref, k_pages, v_pages, o_ref):
    n_pages = len(k_pages)
    rows = 2 * ls * H_D
    lam = _lambda_value(lam_ref, lam_init)

    q = q_ref[...]
    qt = jnp.concatenate([q] * (2 * H_D), axis=0)
    rowi = lax.broadcasted_iota(jnp.int32, (rows, DIFF_W), 0)
    coli = lax.broadcasted_iota(jnp.int32, (rows, DIFF_W), 1)
    q_bd = jnp.where(coli // DH_HALF == rowi // ls, qt, 0.0).astype(BF16)

    pad = jnp.zeros((PAGE_SIZE - ls, DIFF_W), F32)

    def page(ref):
        return _load_head_rows(ref, PAGE_SIZE).astype(BF16)

    k_new = jnp.concatenate([_load_head_rows(kn_ref, ls), pad], axis=0).astype(BF16)
    v_new = jnp.concatenate([_load_head_rows(vn_ref, ls), pad], axis=0).astype(BF16)

    k_all = jnp.concatenate([page(r) for r in k_pages] + [k_new], axis=0)
    n_keys = k_all.shape[0]
    s = _dot_nt(q_bd, k_all) + sbias_ref[...]
    m = jnp.max(s, axis=1, keepdims=True)
    p = jnp.exp(s - m)
    l = jnp.sum(p, axis=1, keepdims=True)
    second_map = (lax.broadcasted_iota(jnp.int32, (rows, 1), 0) // ls) % 2 == 1
    fac = jnp.where(second_map, -lam, 1.0) / l
    p3 = (p * fac).reshape(H_D, 2 * ls, n_keys)
    w = (p3[:, :ls] + p3[:, ls:]).reshape(H_D * ls, n_keys).astype(BF16)
    v_all = jnp.concatenate([page(r) for r in v_pages] + [v_new], axis=0)
    acc = _dot(w, v_all)
    for h in range(H_D):
        o = acc[h * ls:(h + 1) * ls, h * DV_D:(h + 1) * DV_D]
        o_ref[:, h * DV_D:(h + 1) * DV_D] = _head_rms(o, nw_ref[...]) * (1.0 - lam_init)


def _mlstm_chunk(q, k, v, o_pre, i_pre, f_pre, c_prev, n_prev, m_prev, norm_w):
    L = q.shape[0]
    t_idx = lax.broadcasted_iota(jnp.int32, (L, L), 0)
    s_idx = lax.broadcasted_iota(jnp.int32, (L, L), 1)
    causal = s_idx <= t_idx
    eye = s_idx == t_idx

    def to_row(col):
        return jnp.sum(jnp.where(eye, col, 0.0), axis=0, keepdims=True)

    it_col = i_pre
    lf_col = jax.nn.log_sigmoid(f_pre)
    it_row = to_row(it_col)
    b_row = jnp.sum(jnp.where(t_idx <= s_idx, lf_col, 0.0), axis=0, keepdims=True)
    b_col = jnp.sum(jnp.where(causal, to_row(lf_col), 0.0), axis=1, keepdims=True)

    log_d = jnp.where(causal, b_col - b_row + it_row, -jnp.inf)
    m_t = jnp.maximum(b_col + m_prev, jnp.max(log_d, axis=1, keepdims=True))
    dmat = jnp.exp(log_d - m_t)
    inter = jnp.exp(b_col + m_prev - m_t)

    w = _dot_nt(q, k) * dmat
    kf = k.astype(F32)
    num = inter * _dot_nt(q, c_prev.astype(BF16)) + _dot(w.astype(BF16), v)
    den = inter * jnp.sum(q.astype(F32) * n_prev, axis=1, keepdims=True) + jnp.sum(w, axis=1, keepdims=True)
    h = num / jnp.maximum(jnp.abs(den), jnp.exp(-m_t))
    h = _head_rms(h, norm_w) * jax.nn.sigmoid(o_pre.astype(F32))

    m_new = m_t[L - 1:L]
    b_last = b_col[L - 1:L]
    g = jnp.exp(b_last - b_col + it_col - m_new)
    decay = jnp.exp(b_last + m_prev - m_new)
    gk = g * kf
    c_new = decay * c_prev + _dot_tn(v, gk.astype(BF16))
    n_new = decay * n_prev + jnp.sum(gk, axis=0, keepdims=True)
    return h, c_new, n_new, m_new


def _mlstm_prompt_kernel(bg_ref, q_ref, k_ref, v_ref, o_ref, g_ref, nw_ref, h_ref, c_ref, n_ref, m_ref):
    @pl.when(pl.program_id(1) == 0)
    def _():
        c_ref[...] = jnp.zeros_like(c_ref)
        n_ref[...] = jnp.zeros_like(n_ref)
        m_ref[...] = jnp.zeros_like(m_ref)

    for h in range(H_M):
        cols = slice(h * DH_M, (h + 1) * DH_M)
        out, c_new, n_new, m_new = _mlstm_chunk(
            q_ref[:, cols], k_ref[:, cols], v_ref[:, cols], o_ref[:, cols],
            g_ref[:, h:h + 1] + bg_ref[h], g_ref[:, H_M + h:H_M + h + 1] + bg_ref[H_M + h],
            c_ref[0, h], n_ref[0, h:h + 1, :], m_ref[0, h:h + 1, 0:1], nw_ref[:, cols])
        h_ref[:, cols] = out.astype(h_ref.dtype)
        c_ref[0, h] = c_new
        n_ref[0, h:h + 1, :] = n_new
        m_ref[0, h:h + 1, :] = jnp.broadcast_to(m_new, (1, m_ref.shape[2]))


def _mlstm_prompt(b_gates, qm, km, vm, om, gates, norm_w, batch, seq, chunk):
    nc = seq // chunk
    tok = lambda w: pl.BlockSpec((chunk, w), lambda b, c: (b * nc + c, 0))
    sds = jax.ShapeDtypeStruct
    return pl.pallas_call(
        _mlstm_prompt_kernel,
        grid=(batch, nc),
        in_specs=[pl.BlockSpec(memory_space=pltpu.SMEM),
                  tok(MLSTM_W), tok(MLSTM_W), tok(MLSTM_W), tok(MLSTM_W), tok(GATE_PAD),
                  _const_spec(norm_w.shape)],
        out_specs=[tok(MLSTM_W),
                   pl.BlockSpec((1, H_M, DH_M, DH_M), lambda b, c: (b, 0, 0, 0)),
                   pl.BlockSpec((1, H_M, DH_M), lambda b, c: (b, 0, 0)),
                   pl.BlockSpec((1, H_M, 128), lambda b, c: (b, 0, 0))],
        out_shape=[sds((batch * seq, MLSTM_W), BF16), sds((batch, H_M, DH_M, DH_M), F32),
                   sds((batch, H_M, DH_M), F32), sds((batch, H_M, 128), F32)],
        compiler_params=_params("parallel", "arbitrary"),
        name="mlstm_prompt",
    )(b_gates, qm, km, vm, om, gates, norm_w)


def _mlstm_sample_kernel(ls, group, bg_ref, q_ref, k_ref, v_ref, o_ref, g_ref, nw_ref, c_in, n_in, m_in,
                         h_ref, c_ref, n_ref, m_ref):
    pair = 2 if group % 2 == 0 else 1

    def seq_pair(s2, carry):
        for u in range(pair):
            seq(s2 * pair + u)
        return carry

    def seq(s):
        rows = pl.ds(pl.multiple_of(s * ls, ls), ls)
        for h in range(H_M):
            cols = slice(h * DH_M, (h + 1) * DH_M)
            out, c_new, n_new, m_new = _mlstm_chunk(
                q_ref[rows, cols].astype(BF16), k_ref[rows, cols].astype(BF16), v_ref[rows, cols].astype(BF16),
                o_ref[rows, cols],
                g_ref[rows, h:h + 1] + bg_ref[h], g_ref[rows, H_M + h:H_M + h + 1] + bg_ref[H_M + h],
                c_in[s, h], n_in[s, h:h + 1, :], m_in[s, h:h + 1, 0:1], nw_ref[:, cols])
            h_ref[rows, cols] = out
            c_ref[s, h] = c_new
            n_ref[s, h:h + 1, :] = n_new
            m_ref[s, h:h + 1, :] = jnp.broadcast_to(m_new, (1, m_ref.shape[2]))

    lax.fori_loop(0, group // pair, seq_pair, 0)


def _mlstm_sample(b_gates, qm, km, vm, om, gates, norm_w, state_c, state_n, state_m, ls, group):
    bs = state_c.shape[0]
    assert bs % group == 0
    tok = lambda w: pl.BlockSpec((group * ls, w), lambda i: (i, 0))
    c_spec = pl.BlockSpec((group, H_M, DH_M, DH_M), lambda i: (i, 0, 0, 0))
    n_spec = pl.BlockSpec((group, H_M, DH_M), lambda i: (i, 0, 0))
    m_spec = pl.BlockSpec((group, H_M, 128), lambda i: (i, 0, 0))
    sds = jax.ShapeDtypeStruct
    return pl.pallas_call(
        functools.partial(_mlstm_sample_kernel, ls, group),
        grid=(bs // group,),
        in_specs=[pl.BlockSpec(memory_space=pltpu.SMEM),
                  tok(MLSTM_W), tok(MLSTM_W), tok(MLSTM_W), tok(MLSTM_W), tok(GATE_PAD),
                  _const_spec(norm_w.shape), c_spec, n_spec, m_spec],
        out_specs=[tok(MLSTM_W), c_spec, n_spec, m_spec],
        out_shape=[sds((bs * ls, MLSTM_W), F32), sds(state_c.shape, F32),
                   sds(state_n.shape, F32), sds((bs, H_M, 128), F32)],
        compiler_params=_params("parallel"),
        name="mlstm_sample",
    )(b_gates, qm, km, vm, om, gates, norm_w, state_c, state_n, state_m)


def _mix_kernel(head_major, x_ref, hm_ref, hd_ref, w_ref, g_ref, b_ref, y_ref, yb_ref):
    if head_major:
        hd = jnp.concatenate([hd_ref[h] for h in range(H_D)], axis=-1)
    else:
        hd = hd_ref[...]
    mix = _dot(hm_ref[...].astype(BF16), w_ref[0:MLSTM_W, :]) + _dot(hd.astype(BF16), w_ref[MLSTM_W:, :])
    y = _layer_norm(ALPHA * x_ref[...] + mix, g_ref[...], b_ref[...])
    y_ref[...] = y
    yb_ref[...] = y.astype(BF16)


def _mix(x, hm, hd, w_out, ln_g, ln_b, tm, head_major):
    t = x.shape[0]
    assert t % tm == 0
    row = lambda w: pl.BlockSpec((tm, w), lambda i: (i, 0))
    hd_spec = pl.BlockSpec((H_D, tm, DV_D), lambda i: (0, i, 0)) if head_major else row(DIFF_W)
    return pl.pallas_call(
        functools.partial(_mix_kernel, head_major),
        grid=(t // tm,),
        in_specs=[row(D_MODEL), row(MLSTM_W), hd_spec, _resident_spec(w_out.shape),
                  _const_spec(ln_g.shape), _const_spec(ln_b.shape)],
        out_specs=[row(D_MODEL), row(D_MODEL)],
        out_shape=[jax.ShapeDtypeStruct((t, D_MODEL), F32), jax.ShapeDtypeStruct((t, D_MODEL), BF16)],
        compiler_params=_params("parallel"),
        name="mix_ln",
    )(x, hm, hd, w_out, ln_g, ln_b)


def _ffn_up_math(xb_ref, wg_ref, wu_ref, h_ref):
    xb = xb_ref[...]
    h_ref[...] = (jax.nn.silu(_dot(xb, wg_ref[...])) * _dot(xb, wu_ref[...])).astype(h_ref.dtype)


def _ffn_up_kernel(xb_ref, wg_ref, wu_ref, h_ref):
    _ffn_up_math(xb_ref, wg_ref, wu_ref, h_ref)


def _ffn_up(xb, w_gate, w_up, tm, tf):
    t = xb.shape[0]
    assert t % tm == 0 and D_FF % tf == 0
    return pl.pallas_call(
        _ffn_up_kernel,
        grid=(t // tm, D_FF // tf),
        in_specs=[pl.BlockSpec((tm, D_MODEL), lambda i, j: (i, 0)),
                  pl.BlockSpec((D_MODEL, tf), lambda i, j: (0, j)),
                  pl.BlockSpec((D_MODEL, tf), lambda i, j: (0, j))],
        out_specs=pl.BlockSpec((tm, tf), lambda i, j: (i, j)),
        out_shape=jax.ShapeDtypeStruct((t, D_FF), BF16),
        compiler_params=_params("parallel", "parallel"),
        name="ffn_up",
    )(xb, w_gate, w_up)


def _ffn_up_attn_kernel(lam_init, n_pages, ls, n_seq, pt_ref, xb_ref, wg_ref, wu_ref,
                        q_ref, kn_ref, vn_ref, sbias_ref, lam_ref, nw_ref, *rest):
    del pt_ref
    k_pages = rest[:n_pages]
    v_pages = rest[n_pages:2 * n_pages]
    h_ref, o_ref = rest[2 * n_pages:]
    _ffn_up_math(xb_ref, wg_ref, wu_ref, h_ref)
    step = pl.program_id(0) * pl.num_programs(1) + pl.program_id(1)

    @pl.when(step < n_seq)
    def _():
        _attn_sample_body(lam_init, ls, q_ref, kn_ref, vn_ref, sbias_ref, lam_ref, nw_ref, k_pages, v_pages, o_ref)


def _ffn_up_attn_sample(xb, w_gate, w_up, tm, tf, page_table, q, k_new, v_new, cache_k, cache_v, sbias,
                        lam_params, norm_w, ls, lam_init):
    t = xb.shape[0]
    bs, n_pages = page_table.shape
    n_i, n_j = t // tm, D_FF // tf
    assert t % tm == 0 and D_FF % tf == 0 and n_i * n_j >= bs
    rows_per_page = PAGE_SIZE * H_D
    cache_k = cache_k.reshape(-1, DK_D)
    cache_v = cache_v.reshape(-1, DV_D)

    def seq(i, j):
        return jnp.minimum(i * n_j + j, bs - 1)

    seq_spec = pl.BlockSpec((ls, DIFF_W), lambda i, j, pt: (seq(i, j), 0))
    new_spec = pl.BlockSpec((ls * H_D, DK_D), lambda i, j, pt: (seq(i, j), 0))

    def page_spec(p):
        return pl.BlockSpec((rows_per_page, DK_D), lambda i, j, pt: (pt[seq(i, j), p], 0))

    def const(shape):
        return pl.BlockSpec(shape, lambda i, j, pt: (0,) * len(shape))

    grid_spec = pltpu.PrefetchScalarGridSpec(
        num_scalar_prefetch=1,
        grid=(n_i, n_j),
        in_specs=[pl.BlockSpec((tm, D_MODEL), lambda i, j, pt: (i, 0)),
                  pl.BlockSpec((D_MODEL, tf), lambda i, j, pt: (0, j)),
                  pl.BlockSpec((D_MODEL, tf), lambda i, j, pt: (0, j)),
                  seq_spec, new_spec, new_spec,
                  const(sbias.shape), const(lam_params.shape), const(norm_w.shape)]
        + [page_spec(p) for p in range(n_pages)] * 2,
        out_specs=[pl.BlockSpec((tm, tf), lambda i, j, pt: (i, j)), seq_spec],
    )
    return pl.pallas_call(
        functools.partial(_ffn_up_attn_kernel, lam_init, n_pages, ls, bs),
        grid_spec=grid_spec,
        out_shape=[jax.ShapeDtypeStruct((t, D_FF), BF16), jax.ShapeDtypeStruct((bs * ls, DIFF_W), F32)],
        compiler_params=_params("arbitrary", "arbitrary"),
        name="ffn_up_attn_sample",
    )(page_table, xb, w_gate, w_up, q, k_new, v_new, sbias, lam_params, norm_w,
      *([cache_k] * n_pages), *([cache_v] * n_pages))


def _ffn_down_kernel(x_ref, h_ref, wd_ref, g_ref, b_ref, y_ref):
    y_ref[...] = _layer_norm(ALPHA * x_ref[...] + _dot(h_ref[...], wd_ref[...]), g_ref[...], b_ref[...])


def _ffn_down(x, h, w_down, ln_g, ln_b, tm):
    t = x.shape[0]
    assert t % tm == 0
    row = lambda w: pl.BlockSpec((tm, w), lambda i: (i, 0))
    return pl.pallas_call(
        _ffn_down_kernel,
        grid=(t // tm,),
        in_specs=[row(D_MODEL), row(D_FF), _resident_spec(w_down.shape),
                  _const_spec(ln_g.shape), _const_spec(ln_b.shape)],
        out_specs=row(D_MODEL),
        out_shape=jax.ShapeDtypeStruct((t, D_MODEL), F32),
        compiler_params=_params("parallel"),
        name="ffn_down_ln",
    )(x, h, w_down, ln_g, ln_b)


def _tiles(seq):
    return dict(
        cast_cols=512,
        proj_tm=256,
        attn_blk=min(256, seq),
        attn_group=4,
        mlstm_chunk=min(512, seq),
        sample_group=4,
        mix_tm=256,
        ffn_up_tm=1024,
        ffn_up_tf=256,
        ffn_down_tm=256,
    )


def kernel(x_prompt, x_sample, cache_k, cache_v, state_C, state_n, state_m, page_table, rel_bias, w_in, b_gates, lambda_q1, lambda_k1, lambda_q2, lambda_k2, diff_norm_w, mlstm_norm_w, w_out, ln1_g, ln1_b, w_gate, w_up, w_down, ln2_g, ln2_b):
    B, S, _ = x_prompt.shape
    Bs, Ls, _ = x_sample.shape
    n_pages = page_table.shape[1]
    past = n_pages * cache_k.shape[2]
    assert w_in.shape[0] == DEPTH == 1 and cache_k.shape[2] == PAGE_SIZE
    tl = _tiles(S)
    l = 0
    lam_init = 0.8 - 0.6 * math.exp(-0.3 * l)

    wt = w_in[l].T
    wd = _transpose_cast(wt, 0, DIFF_COLS, tl["cast_cols"])
    wm = _transpose_cast(wt, DIFF_COLS, MLSTM_COLS, tl["cast_cols"])
    wg = _gate_rows(wt, DIFF_COLS + MLSTM_COLS)
    w_o = w_out[l].astype(BF16)
    w_g, w_u, w_d = w_gate[l].astype(BF16), w_up[l].astype(BF16), w_down[l].astype(BF16)
    lam_params = jnp.stack([lambda_q1[l], lambda_k1[l], lambda_q2[l], lambda_k2[l]], 0)
    dnw = diff_norm_w[l].reshape(1, DV_D)
    mnw = mlstm_norm_w[l].reshape(1, MLSTM_W)
    g1, b1 = ln1_g[l].reshape(1, D_MODEL), ln1_b[l].reshape(1, D_MODEL)
    g2, b2 = ln2_g[l].reshape(1, D_MODEL), ln2_b[l].reshape(1, D_MODEL)
    bg = b_gates[l]

    pbias, sbias = _bias_tables(rel_bias, tl["attn_blk"], past, Ls)
    sbias = sbias.reshape(H_D * 2 * Ls, past + PAGE_SIZE)

    xp = x_prompt.reshape(B * S, D_MODEL)
    qt, k32, k16, v32, vt, qm, km, vm, om, gates = _in_proj(xp, wd, wm, wg, tl["proj_tm"], True)
    hd = _attn_prompt(qt, k16, vt, pbias, lam_params, dnw.reshape(DV_D, 1), B, S, tl["attn_blk"],
                      tl["attn_group"], lam_init)
    hm, c_p, n_p, m_p = _mlstm_prompt(bg, qm, km, vm, om, gates, mnw, B, S, tl["mlstm_chunk"])
    x1, x1b = _mix(xp, hm, hd, w_o, g1, b1, tl["mix_tm"], True)

    xs = x_sample.reshape(Bs * Ls, D_MODEL)
    qs, ks, vs, qms, kms, vms, oms, gates_s = _in_proj(xs, wd, wm, wg, tl["proj_tm"], False)
    up_p, hd_s = _ffn_up_attn_sample(x1b, w_g, w_u, min(tl["ffn_up_tm"], B * S), tl["ffn_up_tf"], page_table,
                                     qs, ks, vs, cache_k, cache_v, sbias, lam_params, dnw, Ls, lam_init)
    y_p = _ffn_down(x1, up_p, w_d, g2, b2, tl["ffn_down_tm"])
    m_in = jnp.broadcast_to(state_m[l][:, :, None], (Bs, H_M, 128))
    hm_s, c_s, n_s, m_s = _mlstm_sample(bg, qms, kms, vms, oms, gates_s, mnw,
                                        state_C[l], state_n[l], m_in, Ls, tl["sample_group"])
    x1s, x1sb = _mix(xs, hm_s, hd_s, w_o, g1, b1, tl["mix_tm"], False)
    up_s = _ffn_up(x1sb, w_g, w_u, min(tl["ffn_up_tm"], Bs * Ls), tl["ffn_up_tf"])
    y_s = _ffn_down(x1s, up_s, w_d, g2, b2, tl["ffn_down_tm"])

    return (y_p.reshape(B, S, D_MODEL), y_s.reshape(Bs, Ls, D_MODEL),
            k32.reshape(1, B, S, H_D, DK_D), v32.reshape(1, B, S, H_D, DV_D),
            c_p[None], n_p[None], m_p[None, :, :, 0],
            ks.reshape(1, Bs, Ls, H_D, DK_D), vs.reshape(1, Bs, Ls, H_D, DV_D),
            c_s[None], n_s[None], m_s[None, :, :, 0])
```

```python
import functools
import math

import numpy as np
import jax
import jax.numpy as jnp
from jax import lax
from jax.experimental import pallas as pl
from jax.experimental.pallas import tpu as pltpu

D_MODEL = 2048
DEPTH = 1
PAGE_SIZE = 128
H_D = 8
DH_HALF = 64
DK_D = 2 * DH_HALF
DV_D = 2 * DH_HALF
DIFF_W = H_D * DV_D
H_M = 4
DH_M = 256
MLSTM_W = H_M * DH_M
D_FF = -(-8 * D_MODEL // (3 * 256)) * 256
N_BUCKETS = 32
MAX_DIST = 128
ALPHA = (2 * DEPTH) ** 0.25
LN_EPS = 1e-5
RMS_EPS = 1e-6
N_GATES = 2 * H_M
GATE_PAD = 128
DIFF_COLS = 3 * DIFF_W
MLSTM_COLS = 4 * MLSTM_W
MASK_VALUE = -1e30
LOG2E = math.log2(math.e)

VMEM_LIMIT_BYTES = 56 * 1024 * 1024

BF16 = jnp.bfloat16
F32 = jnp.float32


def _dot(a, b):
    return jnp.dot(a, b, preferred_element_type=F32)


def _dot_nt(a, b):
    return lax.dot_general(a, b, (((1,), (1,)), ((), ())), preferred_element_type=F32)


def _dot_tn(a, b):
    return lax.dot_general(a, b, (((0,), (0,)), ((), ())), preferred_element_type=F32)


def _params(*semantics):
    return pltpu.CompilerParams(dimension_semantics=semantics, vmem_limit_bytes=VMEM_LIMIT_BYTES)


def _const_spec(shape):
    n = len(shape)
    return pl.BlockSpec(shape, lambda *_: (0,) * n)


def _cast_riders(riders, n_steps, step_of):
    in_specs, out_specs, out_shapes = [], [], []
    for w in riders:
        rows, cols = w.shape
        assert rows % n_steps == 0 and (rows // n_steps) % 16 == 0
        spec = pl.BlockSpec((rows // n_steps, cols), lambda *g: (step_of(*g), 0))
        in_specs.append(spec)
        out_specs.append(spec)
        out_shapes.append(jax.ShapeDtypeStruct(w.shape, BF16))
    return in_specs, out_specs, out_shapes


def _cast_rider_blocks(in_refs, out_refs):
    for src, dst in zip(in_refs, out_refs):
        dst[...] = src[...].astype(BF16)


def _resident_spec(shape):
    n = len(shape)
    return pl.BlockSpec(shape, lambda *_: (0,) * n, pipeline_mode=pl.Buffered(1))


def _bucket_np(dist):
    n = np.maximum(dist, 0)
    max_exact = N_BUCKETS // 2
    nf = np.maximum(n, 1).astype(np.float32)
    large = max_exact + (np.log(nf / np.float32(max_exact)) / np.float32(math.log(MAX_DIST / max_exact))
                         * np.float32(N_BUCKETS - max_exact)).astype(np.int32)
    large = np.minimum(large, N_BUCKETS - 1)
    out = np.where(n < max_exact, n, large).astype(np.int32)
    return np.where(dist < 0, -1, out).astype(np.int32)


def _bias_kernel(rb_ref, pb_ref, sb_ref, pbias_ref, sbias_ref):
    h = pl.program_id(0)

    def lookup(bk, shift, scale):
        acc = jnp.full(bk.shape, MASK_VALUE, F32)
        for b in range(N_BUCKETS):
            acc = jnp.where(bk == b, (rb_ref[b, h] - shift) * scale, acc)
        return acc

    for t in range(pb_ref.shape[0]):
        pbias_ref[0, t] = lookup(pb_ref[t], rb_ref[N_BUCKETS - 1, h], LOG2E)
    sbias_ref[0] = lookup(sb_ref[...], 0.0, 1.0)


def _bias_tables(rel_bias, blk, past, dec_seq):
    k = np.arange(blk)[:, None]
    q = np.arange(blk)[None, :]
    pb = np.stack([_bucket_np(blk + q - k), _bucket_np(q - k)], 0)
    far = N_BUCKETS - 1
    assert (_bucket_np(np.arange(MAX_DIST, 4 * past + 4 * blk)) == far).all()
    assert blk >= MAX_DIST and PAGE_SIZE >= MAX_DIST
    qpos = past + np.arange(dec_seq)[:, None]
    last = _bucket_np(qpos - (past - PAGE_SIZE + np.arange(PAGE_SIZE))[None, :])
    new = _bucket_np(qpos - (past + np.arange(PAGE_SIZE))[None, :])
    new[:, dec_seq:] = -1
    sb = np.concatenate([np.full((dec_seq, past - PAGE_SIZE), far, np.int32), last, new], 1)
    sb = np.concatenate([sb, sb], 0)
    return pl.pallas_call(
        _bias_kernel,
        grid=(H_D,),
        in_specs=[pl.BlockSpec(memory_space=pltpu.SMEM),
                  _const_spec(pb.shape), _const_spec(sb.shape)],
        out_specs=[pl.BlockSpec((1,) + pb.shape, lambda h: (h, 0, 0, 0)),
                   pl.BlockSpec((1,) + sb.shape, lambda h: (h, 0, 0))],
        out_shape=[jax.ShapeDtypeStruct((H_D,) + pb.shape, F32),
                   jax.ShapeDtypeStruct((H_D,) + sb.shape, F32)],
        compiler_params=_params("arbitrary"),
        name="bias_tables",
    )(rel_bias, jnp.asarray(pb), jnp.asarray(sb))


Q_SCALE = DH_HALF ** -0.5
K_SCALE = DH_M ** -0.5


def _head_rows(ref, h, n_tokens):
    return ref[pl.ds(h, n_tokens, stride=H_D), :]


def _store_head_rows(ref, x):
    for h in range(H_D):
        ref[pl.ds(h, x.shape[0], stride=H_D), :] = x[:, h * DK_D:(h + 1) * DK_D]


def _load_head_rows(ref, n_tokens):
    return jnp.concatenate([_head_rows(ref, h, n_tokens) for h in range(H_D)], axis=1)


def _transpose_cast_kernel(wt_ref, w_ref):
    w_ref[...] = wt_ref[...].T.astype(BF16)


def _transpose_cast(wt, first_col, n_cols, blk):
    d = wt.shape[1]
    assert first_col % blk == 0 and n_cols % blk == 0
    return pl.pallas_call(
        _transpose_cast_kernel,
        grid=(n_cols // blk,),
        in_specs=[pl.BlockSpec((blk, d), lambda j: (first_col // blk + j, 0))],
        out_specs=pl.BlockSpec((d, blk), lambda j: (0, j)),
        out_shape=jax.ShapeDtypeStruct((d, n_cols), BF16),
        compiler_params=_params("parallel"),
        name="transpose_cast",
    )(wt)


def _gate_rows_kernel(wt_ref, o_ref):
    zeros = jnp.zeros((GATE_PAD - N_GATES, wt_ref.shape[1]), F32)
    o_ref[...] = jnp.concatenate([wt_ref[...], zeros], axis=0).astype(BF16)


def _gate_rows(wt, first_row):
    d = wt.shape[1]
    assert first_row % N_GATES == 0
    return pl.pallas_call(
        _gate_rows_kernel,
        grid=(1,),
        in_specs=[pl.BlockSpec((N_GATES, d), lambda i: (first_row // N_GATES, 0))],
        out_specs=pl.BlockSpec((GATE_PAD, d), lambda i: (0, 0)),
        out_shape=jax.ShapeDtypeStruct((GATE_PAD, d), BF16),
        compiler_params=_params("arbitrary"),
        name="gate_rows",
    )(wt)


def _in_proj_prompt_kernel(x_ref, wd_ref, wm_ref, wg_ref, rider_ref,
                           qt_ref, k32_ref, k16_ref, v32_ref, vt_ref,
                           qm_ref, km_ref, vm_ref, om_ref, g_ref, rider_out_ref):
    _cast_rider_blocks([rider_ref], [rider_out_ref])
    x = x_ref[...].astype(BF16)
    r = _dot(x, wd_ref[:, 0:DIFF_W]) * (Q_SCALE * LOG2E)
    for h in range(H_D):
        qt_ref[h] = r[:, h * DK_D:(h + 1) * DK_D].T.astype(BF16)
    r = _dot(x, wd_ref[:, DIFF_W:2 * DIFF_W])
    _store_head_rows(k32_ref, r)
    for h in range(H_D):
        k16_ref[h] = r[:, h * DK_D:(h + 1) * DK_D].astype(BF16)
    r = _dot(x, wd_ref[:, 2 * DIFF_W:3 * DIFF_W])
    _store_head_rows(v32_ref, r)
    for h in range(H_D):
        vt_ref[h] = r[:, h * DV_D:(h + 1) * DV_D].T.astype(BF16)
    qm_ref[...] = _dot(x, wm_ref[:, 0:MLSTM_W]).astype(BF16)
    km_ref[...] = (_dot(x, wm_ref[:, MLSTM_W:2 * MLSTM_W]) * K_SCALE).astype(BF16)
    vm_ref[...] = _dot(x, wm_ref[:, 2 * MLSTM_W:3 * MLSTM_W]).astype(BF16)
    om_ref[...] = _dot(x, wm_ref[:, 3 * MLSTM_W:4 * MLSTM_W]).astype(BF16)
    g_ref[...] = _dot_nt(x, wg_ref[...])


def _in_proj_sample_kernel(x_ref, wd_ref, wm_ref, wg_ref,
                           q_ref, k_ref, v_ref, qm_ref, km_ref, vm_ref, om_ref, g_ref):
    x = x_ref[...].astype(BF16)
    q_ref[...] = _dot(x, wd_ref[:, 0:DIFF_W]) * Q_SCALE
    _store_head_rows(k_ref, _dot(x, wd_ref[:, DIFF_W:2 * DIFF_W]))
    _store_head_rows(v_ref, _dot(x, wd_ref[:, 2 * DIFF_W:3 * DIFF_W]))
    qm_ref[...] = _dot(x, wm_ref[:, 0:MLSTM_W])
    km_ref[...] = _dot(x, wm_ref[:, MLSTM_W:2 * MLSTM_W]) * K_SCALE
    vm_ref[...] = _dot(x, wm_ref[:, 2 * MLSTM_W:3 * MLSTM_W])
    om_ref[...] = _dot(x, wm_ref[:, 3 * MLSTM_W:4 * MLSTM_W])
    g_ref[...] = _dot_nt(x, wg_ref[...])


def _in_proj(x, wd, wm, wg, tm, prompt, riders=()):
    t = x.shape[0]
    assert t % tm == 0
    r_in, r_out, r_shapes = _cast_riders(riders, t // tm, lambda i: i)
    row = lambda w: pl.BlockSpec((tm, w), lambda i: (i, 0))
    heads = pl.BlockSpec((H_D, tm, DK_D), lambda i: (0, i, 0))
    cache = pl.BlockSpec((tm * H_D, DK_D), lambda i: (i, 0))
    sds = jax.ShapeDtypeStruct
    if prompt:
        body = _in_proj_prompt_kernel
        heads_t = pl.BlockSpec((H_D, DK_D, tm), lambda i: (0, 0, i))
        out_specs = [heads_t, cache, heads, cache, heads_t] + [row(MLSTM_W)] * 4 + [row(GATE_PAD)]
        out_shape = [sds((H_D, DK_D, t), BF16), sds((t * H_D, DK_D), F32), sds((H_D, t, DK_D), BF16),
                     sds((t * H_D, DV_D), F32), sds((H_D, DV_D, t), BF16)]
        out_shape += [sds((t, MLSTM_W), BF16)] * 4 + [sds((t, GATE_PAD), F32)]
    else:
        body = _in_proj_sample_kernel
        out_specs = [row(DIFF_W), cache, cache] + [row(MLSTM_W)] * 4 + [row(GATE_PAD)]
        out_shape = ([sds((t, DIFF_W), F32)] + [sds((t * H_D, DK_D), F32)] * 2 + [sds((t, MLSTM_W), F32)] * 4
                     + [sds((t, GATE_PAD), F32)])
    return pl.pallas_call(
        body,
        grid=(t // tm,),
        in_specs=[row(D_MODEL), _resident_spec(wd.shape), _resident_spec(wm.shape), _resident_spec(wg.shape)] + r_in,
        out_specs=out_specs + r_out,
        out_shape=out_shape + r_shapes,
        compiler_params=_params("parallel"),
        name="in_proj_prompt" if prompt else "in_proj_sample",
    )(x, wd, wm, wg, *riders)


def _lambda_value(lam_ref, lam_init):
    lp = lam_ref[...]
    e1 = jnp.exp(jnp.sum(lp[0:1] * lp[1:2], axis=1, keepdims=True))
    e2 = jnp.exp(jnp.sum(lp[2:3] * lp[3:4], axis=1, keepdims=True))
    return e1 - e2 + lam_init


def _head_rms(o, w):
    return o * lax.rsqrt(jnp.mean(o * o, axis=-1, keepdims=True) + RMS_EPS) * w


def _layer_norm(x, g, b):
    xc = x - jnp.mean(x, axis=-1, keepdims=True)
    var = jnp.mean(xc * xc, axis=-1, keepdims=True)
    return xc * lax.rsqrt(var + LN_EPS) * g + b


def _attn_prompt_kernel(lam_init, tq, nq, group, n_riders, qt_ref, k_ref, vt_ref, pbias_ref, lam_ref, nw_ref, *rest):
    o_ref = rest[n_riders]
    _cast_rider_blocks(rest[:n_riders], rest[n_riders + 1:])
    i = pl.program_id(1)
    lam = _lambda_value(lam_ref, lam_init)
    chan = lax.broadcasted_iota(jnp.int32, (DK_D, tq), 0)
    first_map = chan < DH_HALF

    def fold8(op, s):
        return op(s.reshape(s.shape[0] // 8, 8, s.shape[1]), axis=0)

    def process(ii):
        n = (ii + 1) * tq
        n_far = max(n - 2 * tq, 0)

        def scores(h):
            qt = qt_ref[h]
            zero = jnp.zeros_like(qt)
            q2t = jnp.concatenate([jnp.where(first_map, qt, zero), jnp.where(first_map, zero, qt)], axis=1)
            parts = []
            if n_far:
                parts.append(_dot(k_ref[h, 0:n_far, :], q2t))
            if ii >= 1:
                bs = pbias_ref[h, 0]
                parts.append(_dot(k_ref[h, n - 2 * tq:n - tq, :], q2t) + jnp.concatenate([bs, bs], axis=1))
            bd = pbias_ref[h, 1]
            parts.append(_dot(k_ref[h, n - tq:n, :], q2t) + jnp.concatenate([bd, bd], axis=1))
            return parts

        def softmax(parts):
            m8 = fold8(jnp.max, parts[0])
            for s in parts[1:]:
                m8 = jnp.maximum(m8, fold8(jnp.max, s))
            m = jnp.max(m8, axis=0, keepdims=True)
            probs = [jnp.exp2(s - m) for s in parts]
            l8 = fold8(jnp.sum, probs[0])
            for p in probs[1:]:
                l8 = l8 + fold8(jnp.sum, p)
            l = jnp.sum(l8, axis=0, keepdims=True)
            return jnp.concatenate([p.astype(BF16) for p in probs], axis=0), l

        def values(h, p_all, l):
            o = _dot(vt_ref[h, :, 0:n], p_all) / l
            o = o[:, :tq] - lam * o[:, tq:]
            o = o * lax.rsqrt(jnp.mean(o * o, axis=0, keepdims=True) + RMS_EPS) * (nw_ref[...] * (1.0 - lam_init))
            o_ref[h] = o.T.astype(o_ref.dtype)

        def head_group(g, carry):
            hs = [g * group + u for u in range(group)]
            sc, sm = {}, {}
            for t in range(group + 2):
                if t < group:
                    sc[t] = scores(hs[t])
                if 0 <= t - 1 < group:
                    sm[t - 1] = softmax(sc.pop(t - 1))
                if 0 <= t - 2 < group:
                    values(hs[t - 2], *sm.pop(t - 2))
            return carry

        lax.fori_loop(0, H_D // group, head_group, 0)

    for ii in range(nq):
        pl.when(i == ii)(functools.partial(process, ii))


def _attn_prompt(qt, k, vt, pbias, lam_params, norm_w, batch, seq, blk, group, lam_init, riders):
    nq = seq // blk
    assert H_D % group == 0
    r_in, r_out, r_shapes = _cast_riders(riders, batch * nq, lambda b, i: b * nq + i)
    return pl.pallas_call(
        functools.partial(_attn_prompt_kernel, lam_init, blk, nq, group, len(riders)),
        grid=(batch, nq),
        in_specs=[pl.BlockSpec((H_D, DK_D, blk), lambda b, i: (0, 0, b * nq + i)),
                  pl.BlockSpec((H_D, seq, DK_D), lambda b, i: (0, b, 0)),
                  pl.BlockSpec((H_D, DV_D, seq), lambda b, i: (0, 0, b)),
                  _const_spec(pbias.shape),
                  _const_spec(lam_params.shape),
                  _const_spec(norm_w.shape)] + r_in,
        out_specs=[pl.BlockSpec((H_D, blk, DV_D), lambda b, i: (0, b * nq + i, 0))] + r_out,
        out_shape=[jax.ShapeDtypeStruct((H_D, batch * seq, DV_D), BF16)] + r_shapes,
        compiler_params=_params("parallel", "parallel"),
        name="attn_prompt",
    )(qt, k, vt, pbias, lam_params, norm_w, *riders)


def _attn_sample_body(lam_init, ls, q_ref, kn_ref, vn_ref, sbias_ref, lam_ref, nw_ref, k_pages, v_pages, o_ref):
    n_pages = len(k_pages)
    rows = 2 * ls * H_D
    lam = _lambda_value(lam_ref, lam_init)

    q = q_ref[...]
    qt = jnp.concatenate([q] * (2 * H_D), axis=0)
    rowi = lax.broadcasted_iota(jnp.int32, (rows, DIFF_W), 0)
    coli = lax.broadcasted_iota(jnp.int32, (rows, DIFF_W), 1)
    q_bd = jnp.where(coli // DH_HALF == rowi // ls, qt, 0.0).astype(BF16)

    pad = jnp.zeros((PAGE_SIZE - ls, DIFF_W), F32)

    def page(ref):
        return _load_head_rows(ref, PAGE_SIZE).astype(BF16)

    k_new = jnp.concatenate([_load_head_rows(kn_ref, ls), pad], axis=0).astype(BF16)
    v_new = jnp.concatenate([_load_head_rows(vn_ref, ls), pad], axis=0).astype(BF16)

    k_all = jnp.concatenate([page(r) for r in k_pages] + [k_new], axis=0)
    n_keys = k_all.shape[0]
    s = _dot_nt(q_bd, k_all) + sbias_ref[...]
    m = jnp.max(s, axis=1, keepdims=True)
    p = jnp.exp(s - m)
    l = jnp.sum(p, axis=1, keepdims=True)
    second_map = (lax.broadcasted_iota(jnp.int32, (rows, 1), 0) // ls) % 2 == 1
    fac = jnp.where(second_map, -lam, 1.0) / l
    p3 = (p * fac).reshape(H_D, 2 * ls, n_keys)
    w = (p3[:, :ls] + p3[:, ls:]).reshape(H_D * ls, n_keys).astype(BF16)
    v_all = jnp.concatenate([page(r) for r in v_pages] + [v_new], axis=0)
    acc = _dot(w, v_all)
    for h in range(H_D):
        o = acc[h * ls:(h + 1) * ls, h * DV_D:(h + 1) * DV_D]
        o_ref[:, h * DV_D:(h + 1) * DV_D] = _head_rms(o, nw_ref[...]) * (1.0 - lam_init)


def _mlstm_chunk(q, k, v, o_pre, i_pre, f_pre, c_prev, n_prev, m_prev, norm_w):
    L = q.shape[0]
    t_idx = lax.broadcasted_iota(jnp.int32, (L, L), 0)
    s_idx = lax.broadcasted_iota(jnp.int32, (L, L), 1)
    causal = s_idx <= t_idx
    eye = s_idx == t_idx

    def to_row(col):
        return jnp.sum(jnp.where(eye, col, 0.0), axis=0, keepdims=True)

    it_col = i_pre
    lf_col = jax.nn.log_sigmoid(f_pre)
    it_row = to_row(it_col)
    b_row = jnp.sum(jnp.where(t_idx <= s_idx, lf_col, 0.0), axis=0, keepdims=True)
    b_col = jnp.sum(jnp.where(causal, to_row(lf_col), 0.0), axis=1, keepdims=True)

    log_d = jnp.where(causal, b_col - b_row + it_row, -jnp.inf)
    m_t = jnp.maximum(b_col + m_prev, jnp.max(log_d, axis=1, keepdims=True))
    dmat = jnp.exp(log_d - m_t)
    inter = jnp.exp(b_col + m_prev - m_t)

    w = _dot_nt(q, k) * dmat
    kf = k.astype(F32)
    num = inter * _dot_nt(q, c_prev.astype(BF16)) + _dot(w.astype(BF16), v)
    den = inter * jnp.sum(q.astype(F32) * n_prev, axis=1, keepdims=True) + jnp.sum(w, axis=1, keepdims=True)
    h = num / jnp.maximum(jnp.abs(den), jnp.exp(-m_t))
    h = _head_rms(h, norm_w) * jax.nn.sigmoid(o_pre.astype(F32))

    m_new = m_t[L - 1:L]
    b_last = b_col[L - 1:L]
    g = jnp.exp(b_last - b_col + it_col - m_new)
    decay = jnp.exp(b_last + m_prev - m_new)
    gk = g * kf
    c_new = decay * c_prev + _dot_tn(v, gk.astype(BF16))
    n_new = decay * n_prev + jnp.sum(gk, axis=0, keepdims=True)
    return h, c_new, n_new, m_new


def _mlstm_prompt_kernel(bg_ref, q_ref, k_ref, v_ref, o_ref, g_ref, nw_ref, h_ref, c_ref, n_ref, m_ref):
    @pl.when(pl.program_id(1) == 0)
    def _():
        c_ref[...] = jnp.zeros_like(c_ref)
        n_ref[...] = jnp.zeros_like(n_ref)
        m_ref[...] = jnp.zeros_like(m_ref)

    for h in range(H_M):
        cols = slice(h * DH_M, (h + 1) * DH_M)
        out, c_new, n_new, m_new = _mlstm_chunk(
            q_ref[:, cols], k_ref[:, cols], v_ref[:, cols], o_ref[:, cols],
            g_ref[:, h:h + 1] + bg_ref[h], g_ref[:, H_M + h:H_M + h + 1] + bg_ref[H_M + h],
            c_ref[0, h], n_ref[0, h:h + 1, :], m_ref[0, h:h + 1, 0:1], nw_ref[:, cols])
        h_ref[:, cols] = out.astype(h_ref.dtype)
        c_ref[0, h] = c_new
        n_ref[0, h:h + 1, :] = n_new
        m_ref[0, h:h + 1, :] = jnp.broadcast_to(m_new, (1, m_ref.shape[2]))


def _mlstm_prompt(b_gates, qm, km, vm, om, gates, norm_w, batch, seq, chunk):
    nc = seq // chunk
    tok = lambda w: pl.BlockSpec((chunk, w), lambda b, c: (b * nc + c, 0))
    sds = jax.ShapeDtypeStruct
    return pl.pallas_call(
        _mlstm_prompt_kernel,
        grid=(batch, nc),
        in_specs=[pl.BlockSpec(memory_space=pltpu.SMEM),
                  tok(MLSTM_W), tok(MLSTM_W), tok(MLSTM_W), tok(MLSTM_W), tok(GATE_PAD),
                  _const_spec(norm_w.shape)],
        out_specs=[tok(MLSTM_W),
                   pl.BlockSpec((1, H_M, DH_M, DH_M), lambda b, c: (b, 0, 0, 0)),
                   pl.BlockSpec((1, H_M, DH_M), lambda b, c: (b, 0, 0)),
                   pl.BlockSpec((1, H_M, 128), lambda b, c: (b, 0, 0))],
        out_shape=[sds((batch * seq, MLSTM_W), BF16), sds((batch, H_M, DH_M, DH_M), F32),
                   sds((batch, H_M, DH_M), F32), sds((batch, H_M, 128), F32)],
        compiler_params=_params("parallel", "arbitrary"),
        name="mlstm_prompt",
    )(b_gates, qm, km, vm, om, gates, norm_w)


def _mlstm_sample_kernel(ls, group, bg_ref, q_ref, k_ref, v_ref, o_ref, g_ref, nw_ref, c_in, n_in, m_in,
                         h_ref, c_ref, n_ref, m_ref):
    pair = 2 if group % 2 == 0 else 1

    def seq_pair(s2, carry):
        for u in range(pair):
            seq(s2 * pair + u)
        return carry

    def seq(s):
        rows = pl.ds(pl.multiple_of(s * ls, ls), ls)
        for h in range(H_M):
            cols = slice(h * DH_M, (h + 1) * DH_M)
            out, c_new, n_new, m_new = _mlstm_chunk(
                q_ref[rows, cols].astype(BF16), k_ref[rows, cols].astype(BF16), v_ref[rows, cols].astype(BF16),
                o_ref[rows, cols],
                g_ref[rows, h:h + 1] + bg_ref[h], g_ref[rows, H_M + h:H_M + h + 1] + bg_ref[H_M + h],
                c_in[s, h], n_in[s, h:h + 1, :], m_in[s, h:h + 1, 0:1], nw_ref[:, cols])
            h_ref[rows, cols] = out
            c_ref[s, h] = c_new
            n_ref[s, h:h + 1, :] = n_new
            m_ref[s, h:h + 1, :] = jnp.broadcast_to(m_new, (1, m_ref.shape[2]))

    lax.fori_loop(0, group // pair, seq_pair, 0)


def _mlstm_sample(b_gates, qm, km, vm, om, gates, norm_w, state_c, state_n, state_m, ls, group):
    bs = state_c.shape[0]
    assert bs % group == 0
    tok = lambda w: pl.BlockSpec((group * ls, w), lambda i: (i, 0))
    c_spec = pl.BlockSpec((group, H_M, DH_M, DH_M), lambda i: (i, 0, 0, 0))
    n_spec = pl.BlockSpec((group, H_M, DH_M), lambda i: (i, 0, 0))
    m_spec = pl.BlockSpec((group, H_M, 128), lambda i: (i, 0, 0))
    sds = jax.ShapeDtypeStruct
    return pl.pallas_call(
        functools.partial(_mlstm_sample_kernel, ls, group),
        grid=(bs // group,),
        in_specs=[pl.BlockSpec(memory_space=pltpu.SMEM),
                  tok(MLSTM_W), tok(MLSTM_W), tok(MLSTM_W), tok(MLSTM_W), tok(GATE_PAD),
                  _const_spec(norm_w.shape), c_spec, n_spec, m_spec],
        out_specs=[tok(MLSTM_W), c_spec, n_spec, m_spec],
        out_shape=[sds((bs * ls, MLSTM_W), F32), sds(state_c.shape, F32),
                   sds(state_n.shape, F32), sds((bs, H_M, 128), F32)],
        compiler_params=_params("parallel"),
        name="mlstm_sample",
    )(b_gates, qm, km, vm, om, gates, norm_w, state_c, state_n, state_m)


def _mix_kernel(head_major, x_ref, hm_ref, hd_ref, w_ref, g_ref, b_ref, y_ref, yb_ref):
    if head_major:
        hd = jnp.concatenate([hd_ref[h] for h in range(H_D)], axis=-1)
    else:
        hd = hd_ref[...]
    mix = _dot(hm_ref[...].astype(BF16), w_ref[0:MLSTM_W, :]) + _dot(hd.astype(BF16), w_ref[MLSTM_W:, :])
    y = _layer_norm(ALPHA * x_ref[...] + mix, g_ref[...], b_ref[...])
    y_ref[...] = y
    yb_ref[...] = y.astype(BF16)


def _mix(x, hm, hd, w_out, ln_g, ln_b, tm, head_major):
    t = x.shape[0]
    assert t % tm == 0
    row = lambda w: pl.BlockSpec((tm, w), lambda i: (i, 0))
    hd_spec = pl.BlockSpec((H_D, tm, DV_D), lambda i: (0, i, 0)) if head_major else row(DIFF_W)
    return pl.pallas_call(
        functools.partial(_mix_kernel, head_major),
        grid=(t // tm,),
        in_specs=[row(D_MODEL), row(MLSTM_W), hd_spec, _resident_spec(w_out.shape),
                  _const_spec(ln_g.shape), _const_spec(ln_b.shape)],
        out_specs=[row(D_MODEL), row(D_MODEL)],
        out_shape=[jax.ShapeDtypeStruct((t, D_MODEL), F32), jax.ShapeDtypeStruct((t, D_MODEL), BF16)],
        compiler_params=_params("parallel"),
        name="mix_ln",
    )(x, hm, hd, w_out, ln_g, ln_b)


def _ffn_up_math(xb_ref, wg_ref, wu_ref, h_ref):
    xb = xb_ref[...]
    h_ref[...] = (jax.nn.silu(_dot(xb, wg_ref[...])) * _dot(xb, wu_ref[...])).astype(h_ref.dtype)


def _ffn_up_kernel(xb_ref, wg_ref, wu_ref, h_ref):
    _ffn_up_math(xb_ref, wg_ref, wu_ref, h_ref)


def _ffn_up(xb, w_gate, w_up, tm, tf):
    t = xb.shape[0]
    assert t % tm == 0 and D_FF % tf == 0
    return pl.pallas_call(
        _ffn_up_kernel,
        grid=(t // tm, D_FF // tf),
        in_specs=[pl.BlockSpec((tm, D_MODEL), lambda i, j: (i, 0)),
                  pl.BlockSpec((D_MODEL, tf), lambda i, j: (0, j)),
                  pl.BlockSpec((D_MODEL, tf), lambda i, j: (0, j))],
        out_specs=pl.BlockSpec((tm, tf), lambda i, j: (i, j)),
        out_shape=jax.ShapeDtypeStruct((t, D_FF), BF16),
        compiler_params=_params("parallel", "parallel"),
        name="ffn_up",
    )(xb, w_gate, w_up)


def _ffn_up_attn_kernel(lam_init, n_pages, ls, n_seq, pt_ref, xb_ref, wg_ref, wu_ref,
                        q_ref, kn_ref, vn_ref, sbias_ref, lam_ref, nw_ref, *rest):
    del pt_ref
    k_pages = rest[:n_pages]
    v_pages = rest[n_pages:2 * n_pages]
    h_ref, o_ref = rest[2 * n_pages:]
    _ffn_up_math(xb_ref, wg_ref, wu_ref, h_ref)
    step = pl.program_id(0) * pl.num_programs(1) + pl.program_id(1)

    @pl.when(step < n_seq)
    def _():
        _attn_sample_body(lam_init, ls, q_ref, kn_ref, vn_ref, sbias_ref, lam_ref, nw_ref, k_pages, v_pages, o_ref)


def _ffn_up_attn_sample(xb, w_gate, w_up, tm, tf, page_table, q, k_new, v_new, cache_k, cache_v, sbias,
                        lam_params, norm_w, ls, lam_init):
    t = xb.shape[0]
    bs, n_pages = page_table.shape
    n_i, n_j = t // tm, D_FF // tf
    assert t % tm == 0 and D_FF % tf == 0 and n_i * n_j >= bs
    rows_per_page = PAGE_SIZE * H_D
    cache_k = cache_k.reshape(-1, DK_D)
    cache_v = cache_v.reshape(-1, DV_D)

    def seq(i, j):
        return jnp.minimum(i * n_j + j, bs - 1)

    seq_spec = pl.BlockSpec((ls, DIFF_W), lambda i, j, pt: (seq(i, j), 0))
    new_spec = pl.BlockSpec((ls * H_D, DK_D), lambda i, j, pt: (seq(i, j), 0))

    def page_spec(p):
        return pl.BlockSpec((rows_per_page, DK_D), lambda i, j, pt: (pt[seq(i, j), p], 0))

    def const(shape):
        return pl.BlockSpec(shape, lambda i, j, pt: (0,) * len(shape))

    grid_spec = pltpu.PrefetchScalarGridSpec(
        num_scalar_prefetch=1,
        grid=(n_i, n_j),
        in_specs=[pl.BlockSpec((tm, D_MODEL), lambda i, j, pt: (i, 0)),
                  pl.BlockSpec((D_MODEL, tf), lambda i, j, pt: (0, j)),
                  pl.BlockSpec((D_MODEL, tf), lambda i, j, pt: (0, j)),
                  seq_spec, new_spec, new_spec,
                  const(sbias.shape), const(lam_params.shape), const(norm_w.shape)]
        + [page_spec(p) for p in range(n_pages)] * 2,
        out_specs=[pl.BlockSpec((tm, tf), lambda i, j, pt: (i, j)), seq_spec],
    )
    return pl.pallas_call(
        functools.partial(_ffn_up_attn_kernel, lam_init, n_pages, ls, bs),
        grid_spec=grid_spec,
        out_shape=[jax.ShapeDtypeStruct((t, D_FF), BF16), jax.ShapeDtypeStruct((bs * ls, DIFF_W), F32)],
        compiler_params=_params("arbitrary", "arbitrary"),
        name="ffn_up_attn_sample",
    )(page_table, xb, w_gate, w_up, q, k_new, v_new, sbias, lam_params, norm_w,
      *([cache_k] * n_pages), *([cache_v] * n_pages))


def _ffn_down_kernel(x_ref, h_ref, wd_ref, g_ref, b_ref, y_ref):
    y_ref[...] = _layer_norm(ALPHA * x_ref[...] + _dot(h_ref[...], wd_ref[...]), g_ref[...], b_ref[...])


def _ffn_down(x, h, w_down, ln_g, ln_b, tm):
    t = x.shape[0]
    assert t % tm == 0
    row = lambda w: pl.BlockSpec((tm, w), lambda i: (i, 0))
    return pl.pallas_call(
        _ffn_down_kernel,
        grid=(t // tm,),
        in_specs=[row(D_MODEL), row(D_FF), _resident_spec(w_down.shape),
                  _const_spec(ln_g.shape), _const_spec(ln_b.shape)],
        out_specs=row(D_MODEL),
        out_shape=jax.ShapeDtypeStruct((t, D_MODEL), F32),
        compiler_params=_params("parallel"),
        name="ffn_down_ln",
    )(x, h, w_down, ln_g, ln_b)


def _tiles(seq):
    return dict(
        cast_cols=512,
        proj_tm=256,
        attn_blk=min(256, seq),
        attn_group=4,
        mlstm_chunk=min(512, seq),
        sample_group=4,
        mix_tm=256,
        ffn_up_tm=1024,
        ffn_up_tf=256,
        ffn_down_tm=256,
    )


def kernel(x_prompt, x_sample, cache_k, cache_v, state_C, state_n, state_m, page_table, rel_bias, w_in, b_gates, lambda_q1, lambda_k1, lambda_q2, lambda_k2, diff_norm_w, mlstm_norm_w, w_out, ln1_g, ln1_b, w_gate, w_up, w_down, ln2_g, ln2_b):
    B, S, _ = x_prompt.shape
    Bs, Ls, _ = x_sample.shape
    n_pages = page_table.shape[1]
    past = n_pages * cache_k.shape[2]
    assert w_in.shape[0] == DEPTH == 1 and cache_k.shape[2] == PAGE_SIZE
    tl = _tiles(S)
    l = 0
    lam_init = 0.8 - 0.6 * math.exp(-0.3 * l)

    wt = w_in[l].T
    wd = _transpose_cast(wt, 0, DIFF_COLS, tl["cast_cols"])
    wm = _transpose_cast(wt, DIFF_COLS, MLSTM_COLS, tl["cast_cols"])
    wg = _gate_rows(wt, DIFF_COLS + MLSTM_COLS)
    lam_params = jnp.stack([lambda_q1[l], lambda_k1[l], lambda_q2[l], lambda_k2[l]], 0)
    dnw = diff_norm_w[l].reshape(1, DV_D)
    mnw = mlstm_norm_w[l].reshape(1, MLSTM_W)
    g1, b1 = ln1_g[l].reshape(1, D_MODEL), ln1_b[l].reshape(1, D_MODEL)
    g2, b2 = ln2_g[l].reshape(1, D_MODEL), ln2_b[l].reshape(1, D_MODEL)
    bg = b_gates[l]

    pbias, sbias = _bias_tables(rel_bias, tl["attn_blk"], past, Ls)
    sbias = sbias.reshape(H_D * 2 * Ls, past + PAGE_SIZE)

    xp = x_prompt.reshape(B * S, D_MODEL)
    qt, k32, k16, v32, vt, qm, km, vm, om, gates, w_o = _in_proj(xp, wd, wm, wg, tl["proj_tm"], True,
                                                                 riders=(w_out[l],))
    hd, w_g, w_u, w_d = _attn_prompt(qt, k16, vt, pbias, lam_params, dnw.reshape(DV_D, 1), B, S, tl["attn_blk"],
                                     tl["attn_group"], lam_init, riders=(w_gate[l], w_up[l], w_down[l]))
    hm, c_p, n_p, m_p = _mlstm_prompt(bg, qm, km, vm, om, gates, mnw, B, S, tl["mlstm_chunk"])
    x1, x1b = _mix(xp, hm, hd, w_o, g1, b1, tl["mix_tm"], True)

    xs = x_sample.reshape(Bs * Ls, D_MODEL)
    qs, ks, vs, qms, kms, vms, oms, gates_s = _in_proj(xs, wd, wm, wg, tl["proj_tm"], False)
    up_p, hd_s = _ffn_up_attn_sample(x1b, w_g, w_u, min(tl["ffn_up_tm"], B * S), tl["ffn_up_tf"], page_table,
                                     qs, ks, vs, cache_k, cache_v, sbias, lam_params, dnw, Ls, lam_init)
    y_p = _ffn_down(x1, up_p, w_d, g2, b2, tl["ffn_down_tm"])
    m_in = jnp.broadcast_to(state_m[l][:, :, None], (Bs, H_M, 128))
    hm_s, c_s, n_s, m_s = _mlstm_sample(bg, qms, kms, vms, oms, gates_s, mnw,
                                        state_C[l], state_n[l], m_in, Ls, tl["sample_group"])
    x1s, x1sb = _mix(xs, hm_s, hd_s, w_o, g1, b1, tl["mix_tm"], False)
    up_s = _ffn_up(x1sb, w_g, w_u, min(tl["ffn_up_tm"], Bs * Ls), tl["ffn_up_tf"])
    y_s = _ffn_down(x1s, up_s, w_d, g2, b2, tl["ffn_down_tm"])

    return (y_p.reshape(B, S, D_MODEL), y_s.reshape(Bs, Ls, D_MODEL),
            k32.reshape(1, B, S, H_D, DK_D), v32.reshape(1, B, S, H_D, DV_D),
            c_p[None], n_p[None], m_p[None, :, :, 0],
            ks.reshape(1, Bs, Ls, H_D, DK_D), vs.reshape(1, Bs, Ls, H_D, DV_D),
            c_s[None], n_s[None], m_s[None, :, :, 0])
```

```python
import functools
import math

import numpy as np
import jax
import jax.numpy as jnp
from jax import lax
from jax.experimental import pallas as pl
from jax.experimental.pallas import tpu as pltpu

D_MODEL = 2048
DEPTH = 1
PAGE_SIZE = 128
H_D = 8
DH_HALF = 64
DK_D = 2 * DH_HALF
DV_D = 2 * DH_HALF
DIFF_W = H_D * DV_D
H_M = 4
DH_M = 256
MLSTM_W = H_M * DH_M
D_FF = -(-8 * D_MODEL // (3 * 256)) * 256
N_BUCKETS = 32
MAX_DIST = 128
ALPHA = (2 * DEPTH) ** 0.25
LN_EPS = 1e-5
RMS_EPS = 1e-6
N_GATES = 2 * H_M
LANES = 128
SUBLANES = 8
GATE_PAD = LANES
DIFF_COLS = 3 * DIFF_W
MLSTM_COLS = 4 * MLSTM_W
MASK_VALUE = -1e30
LOG2E = math.log2(math.e)

VMEM_LIMIT_BYTES = 56 * 1024 * 1024

BF16 = jnp.bfloat16
F32 = jnp.float32


def _dot(a, b):
    return jnp.dot(a, b, preferred_element_type=F32)


def _dot_nt(a, b):
    return lax.dot_general(a, b, (((1,), (1,)), ((), ())), preferred_element_type=F32)


def _dot_tn(a, b):
    return lax.dot_general(a, b, (((0,), (0,)), ((), ())), preferred_element_type=F32)


def _params(*semantics):
    return pltpu.CompilerParams(dimension_semantics=semantics, vmem_limit_bytes=VMEM_LIMIT_BYTES)


def _const_spec(shape):
    n = len(shape)
    return pl.BlockSpec(shape, lambda *_: (0,) * n)


def _cast_riders(riders, n_steps, step_of):
    in_specs, out_specs, out_shapes = [], [], []
    for w in riders:
        rows, cols = w.shape
        assert rows % n_steps == 0 and (rows // n_steps) % 16 == 0
        spec = pl.BlockSpec((rows // n_steps, cols), lambda *g: (step_of(*g), 0))
        in_specs.append(spec)
        out_specs.append(spec)
        out_shapes.append(jax.ShapeDtypeStruct(w.shape, BF16))
    return in_specs, out_specs, out_shapes


def _cast_rider_blocks(in_refs, out_refs):
    for src, dst in zip(in_refs, out_refs):
        dst[...] = src[...].astype(BF16)


def _resident_spec(shape):
    n = len(shape)
    return pl.BlockSpec(shape, lambda *_: (0,) * n, pipeline_mode=pl.Buffered(1))


def _bucket_np(dist):
    n = np.maximum(dist, 0)
    max_exact = N_BUCKETS // 2
    nf = np.maximum(n, 1).astype(np.float32)
    large = max_exact + (np.log(nf / np.float32(max_exact)) / np.float32(math.log(MAX_DIST / max_exact))
                         * np.float32(N_BUCKETS - max_exact)).astype(np.int32)
    large = np.minimum(large, N_BUCKETS - 1)
    out = np.where(n < max_exact, n, large).astype(np.int32)
    return np.where(dist < 0, -1, out).astype(np.int32)


def _bias_kernel(rb_ref, pb_ref, sb_ref, pbias_ref, sbias_ref):
    h = pl.program_id(0)

    def lookup(bk, shift, scale):
        acc = jnp.full(bk.shape, MASK_VALUE, F32)
        for b in range(N_BUCKETS):
            acc = jnp.where(bk == b, (rb_ref[b, h] - shift) * scale, acc)
        return acc

    for t in range(pb_ref.shape[0]):
        pbias_ref[0, t] = lookup(pb_ref[t], rb_ref[N_BUCKETS - 1, h], LOG2E)
    sbias_ref[0] = lookup(sb_ref[...], 0.0, 1.0)


def _bias_tables(rel_bias, blk, past, dec_seq):
    k = np.arange(blk)[:, None]
    q = np.arange(blk)[None, :]
    pb = np.stack([_bucket_np(blk + q - k), _bucket_np(q - k)], 0)
    far = N_BUCKETS - 1
    assert (_bucket_np(np.arange(MAX_DIST, 4 * past + 4 * blk)) == far).all()
    assert blk >= MAX_DIST and PAGE_SIZE >= MAX_DIST
    qpos = past + np.arange(dec_seq)[:, None]
    last = _bucket_np(qpos - (past - PAGE_SIZE + np.arange(PAGE_SIZE))[None, :])
    new = _bucket_np(qpos - (past + np.arange(PAGE_SIZE))[None, :])
    new[:, dec_seq:] = -1
    sb = np.concatenate([np.full((dec_seq, past - PAGE_SIZE), far, np.int32), last, new], 1)
    sb = np.concatenate([sb, sb], 0)
    return pl.pallas_call(
        _bias_kernel,
        grid=(H_D,),
        in_specs=[pl.BlockSpec(memory_space=pltpu.SMEM),
                  _const_spec(pb.shape), _const_spec(sb.shape)],
        out_specs=[pl.BlockSpec((1,) + pb.shape, lambda h: (h, 0, 0, 0)),
                   pl.BlockSpec((1,) + sb.shape, lambda h: (h, 0, 0))],
        out_shape=[jax.ShapeDtypeStruct((H_D,) + pb.shape, F32),
                   jax.ShapeDtypeStruct((H_D,) + sb.shape, F32)],
        compiler_params=_params("arbitrary"),
        name="bias_tables",
    )(rel_bias, jnp.asarray(pb), jnp.asarray(sb))


Q_SCALE = DH_HALF ** -0.5
K_SCALE = DH_M ** -0.5


def _head_rows(ref, h, n_tokens):
    return ref[pl.ds(h, n_tokens, stride=H_D), :]


def _store_head_rows(ref, x):
    for h in range(H_D):
        ref[pl.ds(h, x.shape[0], stride=H_D), :] = x[:, h * DK_D:(h + 1) * DK_D]


def _load_head_rows(ref, n_tokens):
    return jnp.concatenate([_head_rows(ref, h, n_tokens) for h in range(H_D)], axis=1)


def _transpose_cast_kernel(wt_ref, w_ref):
    w_ref[...] = wt_ref[...].T.astype(BF16)


def _transpose_cast(wt, first_col, n_cols, blk):
    d = wt.shape[1]
    assert first_col % blk == 0 and n_cols % blk == 0
    return pl.pallas_call(
        _transpose_cast_kernel,
        grid=(n_cols // blk,),
        in_specs=[pl.BlockSpec((blk, d), lambda j: (first_col // blk + j, 0))],
        out_specs=pl.BlockSpec((d, blk), lambda j: (0, j)),
        out_shape=jax.ShapeDtypeStruct((d, n_cols), BF16),
        compiler_params=_params("parallel"),
        name="transpose_cast",
    )(wt)


def _gate_rows_kernel(wt_ref, o_ref):
    zeros = jnp.zeros((GATE_PAD - N_GATES, wt_ref.shape[1]), F32)
    o_ref[...] = jnp.concatenate([wt_ref[...], zeros], axis=0).astype(BF16)


def _gate_rows(wt, first_row):
    d = wt.shape[1]
    assert first_row % N_GATES == 0
    return pl.pallas_call(
        _gate_rows_kernel,
        grid=(1,),
        in_specs=[pl.BlockSpec((N_GATES, d), lambda i: (first_row // N_GATES, 0))],
        out_specs=pl.BlockSpec((GATE_PAD, d), lambda i: (0, 0)),
        out_shape=jax.ShapeDtypeStruct((GATE_PAD, d), BF16),
        compiler_params=_params("arbitrary"),
        name="gate_rows",
    )(wt)


def _in_proj_prompt_kernel(x_ref, wd_ref, wm_ref, wg_ref, rider_ref,
                           qt_ref, k32_ref, k16_ref, v32_ref, vt_ref,
                           qm_ref, km_ref, vm_ref, om_ref, g_ref, rider_out_ref):
    _cast_rider_blocks([rider_ref], [rider_out_ref])
    x = x_ref[...].astype(BF16)
    r = _dot(x, wd_ref[:, 0:DIFF_W]) * (Q_SCALE * LOG2E)
    for h in range(H_D):
        qt_ref[h] = r[:, h * DK_D:(h + 1) * DK_D].T.astype(BF16)
    r = _dot(x, wd_ref[:, DIFF_W:2 * DIFF_W])
    _store_head_rows(k32_ref, r)
    for h in range(H_D):
        k16_ref[h] = r[:, h * DK_D:(h + 1) * DK_D].astype(BF16)
    r = _dot(x, wd_ref[:, 2 * DIFF_W:3 * DIFF_W])
    _store_head_rows(v32_ref, r)
    for h in range(H_D):
        vt_ref[h] = r[:, h * DV_D:(h + 1) * DV_D].T.astype(BF16)
    qm_ref[...] = _dot(x, wm_ref[:, 0:MLSTM_W]).astype(BF16)
    km_ref[...] = (_dot(x, wm_ref[:, MLSTM_W:2 * MLSTM_W]) * K_SCALE).astype(BF16)
    vm_ref[...] = _dot(x, wm_ref[:, 2 * MLSTM_W:3 * MLSTM_W]).astype(BF16)
    om_ref[...] = _dot(x, wm_ref[:, 3 * MLSTM_W:4 * MLSTM_W]).astype(BF16)
    g_ref[...] = _dot_nt(x, wg_ref[...])


def _in_proj_sample_kernel(x_ref, wd_ref, wm_ref, wg_ref,
                           q_ref, k_ref, v_ref, qm_ref, km_ref, vm_ref, om_ref, g_ref):
    x = x_ref[...].astype(BF16)
    q_ref[...] = _dot(x, wd_ref[:, 0:DIFF_W]) * Q_SCALE
    _store_head_rows(k_ref, _dot(x, wd_ref[:, DIFF_W:2 * DIFF_W]))
    _store_head_rows(v_ref, _dot(x, wd_ref[:, 2 * DIFF_W:3 * DIFF_W]))
    qm_ref[...] = _dot(x, wm_ref[:, 0:MLSTM_W])
    km_ref[...] = _dot(x, wm_ref[:, MLSTM_W:2 * MLSTM_W]) * K_SCALE
    vm_ref[...] = _dot(x, wm_ref[:, 2 * MLSTM_W:3 * MLSTM_W])
    om_ref[...] = _dot(x, wm_ref[:, 3 * MLSTM_W:4 * MLSTM_W])
    g_ref[...] = _dot_nt(x, wg_ref[...])


def _in_proj(x, wd, wm, wg, tm, prompt, riders=()):
    t = x.shape[0]
    assert t % tm == 0
    r_in, r_out, r_shapes = _cast_riders(riders, t // tm, lambda i: i)
    row = lambda w: pl.BlockSpec((tm, w), lambda i: (i, 0))
    heads = pl.BlockSpec((H_D, tm, DK_D), lambda i: (0, i, 0))
    cache = pl.BlockSpec((tm * H_D, DK_D), lambda i: (i, 0))
    sds = jax.ShapeDtypeStruct
    if prompt:
        body = _in_proj_prompt_kernel
        heads_t = pl.BlockSpec((H_D, DK_D, tm), lambda i: (0, 0, i))
        out_specs = [heads_t, cache, heads, cache, heads_t] + [row(MLSTM_W)] * 4 + [row(GATE_PAD)]
        out_shape = [sds((H_D, DK_D, t), BF16), sds((t * H_D, DK_D), F32), sds((H_D, t, DK_D), BF16),
                     sds((t * H_D, DV_D), F32), sds((H_D, DV_D, t), BF16)]
        out_shape += [sds((t, MLSTM_W), BF16)] * 4 + [sds((t, GATE_PAD), F32)]
    else:
        body = _in_proj_sample_kernel
        out_specs = [row(DIFF_W), cache, cache] + [row(MLSTM_W)] * 4 + [row(GATE_PAD)]
        out_shape = ([sds((t, DIFF_W), F32)] + [sds((t * H_D, DK_D), F32)] * 2 + [sds((t, MLSTM_W), F32)] * 4
                     + [sds((t, GATE_PAD), F32)])
    return pl.pallas_call(
        body,
        grid=(t // tm,),
        in_specs=[row(D_MODEL), _resident_spec(wd.shape), _resident_spec(wm.shape), _resident_spec(wg.shape)] + r_in,
        out_specs=out_specs + r_out,
        out_shape=out_shape + r_shapes,
        compiler_params=_params("parallel"),
        name="in_proj_prompt" if prompt else "in_proj_sample",
    )(x, wd, wm, wg, *riders)


def _lambda_value(lam_ref, lam_init):
    lp = lam_ref[...]
    e1 = jnp.exp(jnp.sum(lp[0:1] * lp[1:2], axis=1, keepdims=True))
    e2 = jnp.exp(jnp.sum(lp[2:3] * lp[3:4], axis=1, keepdims=True))
    return e1 - e2 + lam_init


def _head_rms(o, w):
    return o * lax.rsqrt(jnp.mean(o * o, axis=-1, keepdims=True) + RMS_EPS) * w


def _layer_norm(x, g, b):
    xc = x - jnp.mean(x, axis=-1, keepdims=True)
    var = jnp.mean(xc * xc, axis=-1, keepdims=True)
    return xc * lax.rsqrt(var + LN_EPS) * g + b


def _attn_prompt_kernel(lam_init, tq, nq, group, n_riders, qt_ref, k_ref, vt_ref, pbias_ref, lam_ref, nw_ref, *rest):
    o_ref = rest[n_riders]
    _cast_rider_blocks(rest[:n_riders], rest[n_riders + 1:])
    i = pl.program_id(1)
    lam = _lambda_value(lam_ref, lam_init)
    chan = lax.broadcasted_iota(jnp.int32, (DK_D, tq), 0)
    first_map = chan < DH_HALF

    def fold8(op, s):
        return op(s.reshape(s.shape[0] // SUBLANES, SUBLANES, s.shape[1]), axis=0)

    def process(ii):
        n = (ii + 1) * tq
        n_far = max(n - 2 * tq, 0)

        def scores(h):
            qt = qt_ref[h]
            zero = jnp.zeros_like(qt)
            q2t = jnp.concatenate([jnp.where(first_map, qt, zero), jnp.where(first_map, zero, qt)], axis=1)
            parts = []
            if n_far:
                parts.append(_dot(k_ref[h, 0:n_far, :], q2t))
            if ii >= 1:
                bs = pbias_ref[h, 0]
                parts.append(_dot(k_ref[h, n - 2 * tq:n - tq, :], q2t) + jnp.concatenate([bs, bs], axis=1))
            bd = pbias_ref[h, 1]
            parts.append(_dot(k_ref[h, n - tq:n, :], q2t) + jnp.concatenate([bd, bd], axis=1))
            return parts

        def softmax(parts):
            m8 = fold8(jnp.max, parts[0])
            for s in parts[1:]:
                m8 = jnp.maximum(m8, fold8(jnp.max, s))
            m = jnp.max(m8, axis=0, keepdims=True)
            probs = [jnp.exp2(s - m) for s in parts]
            l8 = fold8(jnp.sum, probs[0])
            for p in probs[1:]:
                l8 = l8 + fold8(jnp.sum, p)
            l = jnp.sum(l8, axis=0, keepdims=True)
            return jnp.concatenate([p.astype(BF16) for p in probs], axis=0), l

        def values(h, p_all, l):
            o = _dot(vt_ref[h, :, 0:n], p_all) / l
            o = o[:, :tq] - lam * o[:, tq:]
            o = o * lax.rsqrt(jnp.mean(o * o, axis=0, keepdims=True) + RMS_EPS) * (nw_ref[...] * (1.0 - lam_init))
            o_ref[h] = o.T.astype(o_ref.dtype)

        def head_group(g, carry):
            hs = [g * group + u for u in range(group)]
            sc, sm = {}, {}
            for t in range(group + 2):
                if t < group:
                    sc[t] = scores(hs[t])
                if 0 <= t - 1 < group:
                    sm[t - 1] = softmax(sc.pop(t - 1))
                if 0 <= t - 2 < group:
                    values(hs[t - 2], *sm.pop(t - 2))
            return carry

        lax.fori_loop(0, H_D // group, head_group, 0)

    for ii in range(nq):
        pl.when(i == ii)(functools.partial(process, ii))


def _attn_prompt(qt, k, vt, pbias, lam_params, norm_w, batch, seq, blk, group, lam_init, riders):
    nq = seq // blk
    assert H_D % group == 0
    r_in, r_out, r_shapes = _cast_riders(riders, batch * nq, lambda b, i: b * nq + i)
    return pl.pallas_call(
        functools.partial(_attn_prompt_kernel, lam_init, blk, nq, group, len(riders)),
        grid=(batch, nq),
        in_specs=[pl.BlockSpec((H_D, DK_D, blk), lambda b, i: (0, 0, b * nq + i)),
                  pl.BlockSpec((H_D, seq, DK_D), lambda b, i: (0, b, 0)),
                  pl.BlockSpec((H_D, DV_D, seq), lambda b, i: (0, 0, b)),
                  _const_spec(pbias.shape),
                  _const_spec(lam_params.shape),
                  _const_spec(norm_w.shape)] + r_in,
        out_specs=[pl.BlockSpec((H_D, blk, DV_D), lambda b, i: (0, b * nq + i, 0))] + r_out,
        out_shape=[jax.ShapeDtypeStruct((H_D, batch * seq, DV_D), BF16)] + r_shapes,
        compiler_params=_params("parallel", "parallel"),
        name="attn_prompt",
    )(qt, k, vt, pbias, lam_params, norm_w, *riders)


def _attn_sample_body(lam_init, ls, q_ref, kn_ref, vn_ref, sbias_ref, lam_ref, nw_ref, k_pages, v_pages, o_ref,
                      after_page=None):
    n_pages = len(k_pages)
    rows = 2 * ls * H_D
    lam = _lambda_value(lam_ref, lam_init)

    q = q_ref[...]
    qt = jnp.concatenate([q] * (2 * H_D), axis=0)
    rowi = lax.broadcasted_iota(jnp.int32, (rows, DIFF_W), 0)
    coli = lax.broadcasted_iota(jnp.int32, (rows, DIFF_W), 1)
    q_bd = jnp.where(coli // DH_HALF == rowi // ls, qt, 0.0).astype(BF16)

    pad = jnp.zeros((PAGE_SIZE - ls, DIFF_W), F32)

    def page(ref):
        return _load_head_rows(ref, PAGE_SIZE).astype(BF16)

    k_new = jnp.concatenate([_load_head_rows(kn_ref, ls), pad], axis=0).astype(BF16)
    v_new = jnp.concatenate([_load_head_rows(vn_ref, ls), pad], axis=0).astype(BF16)

    def pages(refs):
        out = []
        for r in refs:
            out.append(page(r))
            if after_page is not None:
                after_page()
        return out

    k_all = jnp.concatenate(pages(k_pages) + [k_new], axis=0)
    n_keys = k_all.shape[0]
    s = _dot_nt(q_bd, k_all) + sbias_ref[...]
    m = jnp.max(s, axis=1, keepdims=True)
    p = jnp.exp(s - m)
    l = jnp.sum(p, axis=1, keepdims=True)
    second_map = (lax.broadcasted_iota(jnp.int32, (rows, 1), 0) // ls) % 2 == 1
    fac = jnp.where(second_map, -lam, 1.0) / l
    p3 = (p * fac).reshape(H_D, 2 * ls, n_keys)
    w = (p3[:, :ls] + p3[:, ls:]).reshape(H_D * ls, n_keys).astype(BF16)
    v_all = jnp.concatenate(pages(v_pages) + [v_new], axis=0)
    acc = _dot(w, v_all)
    for h in range(H_D):
        o = acc[h * ls:(h + 1) * ls, h * DV_D:(h + 1) * DV_D]
        o_ref[:, h * DV_D:(h + 1) * DV_D] = _head_rms(o, nw_ref[...]) * (1.0 - lam_init)


def _mlstm_chunk(q, k, v, o_pre, i_pre, f_pre, c_prev, n_prev, m_prev, norm_w):
    L = q.shape[0]
    t_idx = lax.broadcasted_iota(jnp.int32, (L, L), 0)
    s_idx = lax.broadcasted_iota(jnp.int32, (L, L), 1)
    causal = s_idx <= t_idx
    eye = s_idx == t_idx

    def to_row(col):
        return jnp.sum(jnp.where(eye, col, 0.0), axis=0, keepdims=True)

    it_col = i_pre
    lf_col = jax.nn.log_sigmoid(f_pre)
    it_row = to_row(it_col)
    b_row = jnp.sum(jnp.where(t_idx <= s_idx, lf_col, 0.0), axis=0, keepdims=True)
    b_col = jnp.sum(jnp.where(causal, to_row(lf_col), 0.0), axis=1, keepdims=True)

    log_d = jnp.where(causal, b_col - b_row + it_row, -jnp.inf)
    m_t = jnp.maximum(b_col + m_prev, jnp.max(log_d, axis=1, keepdims=True))
    dmat = jnp.exp(log_d - m_t)
    inter = jnp.exp(b_col + m_prev - m_t)

    w = _dot_nt(q, k) * dmat
    kf = k.astype(F32)
    num = inter * _dot_nt(q, c_prev.astype(BF16)) + _dot(w.astype(BF16), v)
    den = inter * jnp.sum(q.astype(F32) * n_prev, axis=1, keepdims=True) + jnp.sum(w, axis=1, keepdims=True)
    h = num / jnp.maximum(jnp.abs(den), jnp.exp(-m_t))
    h = _head_rms(h, norm_w) * jax.nn.sigmoid(o_pre.astype(F32))

    m_new = m_t[L - 1:L]
    b_last = b_col[L - 1:L]
    g = jnp.exp(b_last - b_col + it_col - m_new)
    decay = jnp.exp(b_last + m_prev - m_new)
    gk = g * kf
    c_new = decay * c_prev + _dot_tn(v, gk.astype(BF16))
    n_new = decay * n_prev + jnp.sum(gk, axis=0, keepdims=True)
    return h, c_new, n_new, m_new


def _mlstm_prompt_kernel(bg_ref, q_ref, k_ref, v_ref, o_ref, g_ref, nw_ref, h_ref, c_ref, n_ref, m_ref):
    @pl.when(pl.program_id(1) == 0)
    def _():
        c_ref[...] = jnp.zeros_like(c_ref)
        n_ref[...] = jnp.zeros_like(n_ref)
        m_ref[...] = jnp.zeros_like(m_ref)

    for h in range(H_M):
        cols = slice(h * DH_M, (h + 1) * DH_M)
        out, c_new, n_new, m_new = _mlstm_chunk(
            q_ref[:, cols], k_ref[:, cols], v_ref[:, cols], o_ref[:, cols],
            g_ref[:, h:h + 1] + bg_ref[h], g_ref[:, H_M + h:H_M + h + 1] + bg_ref[H_M + h],
            c_ref[0, h], n_ref[0, h:h + 1, :], m_ref[0, h:h + 1, 0:1], nw_ref[:, cols])
        h_ref[:, cols] = out.astype(h_ref.dtype)
        c_ref[0, h] = c_new
        n_ref[0, h:h + 1, :] = n_new
        m_ref[0, h:h + 1, :] = jnp.broadcast_to(m_new, (1, m_ref.shape[2]))


def _mlstm_prompt(b_gates, qm, km, vm, om, gates, norm_w, batch, seq, chunk):
    nc = seq // chunk
    tok = lambda w: pl.BlockSpec((chunk, w), lambda b, c: (b * nc + c, 0))
    sds = jax.ShapeDtypeStruct
    return pl.pallas_call(
        _mlstm_prompt_kernel,
        grid=(batch, nc),
        in_specs=[pl.BlockSpec(memory_space=pltpu.SMEM),
                  tok(MLSTM_W), tok(MLSTM_W), tok(MLSTM_W), tok(MLSTM_W), tok(GATE_PAD),
                  _const_spec(norm_w.shape)],
        out_specs=[tok(MLSTM_W),
                   pl.BlockSpec((1, H_M, DH_M, DH_M), lambda b, c: (b, 0, 0, 0)),
                   pl.BlockSpec((1, H_M, DH_M), lambda b, c: (b, 0, 0)),
                   pl.BlockSpec((1, H_M, LANES), lambda b, c: (b, 0, 0))],
        out_shape=[sds((batch * seq, MLSTM_W), BF16), sds((batch, H_M, DH_M, DH_M), F32),
                   sds((batch, H_M, DH_M), F32), sds((batch, H_M, LANES), F32)],
        compiler_params=_params("parallel", "arbitrary"),
        name="mlstm_prompt",
    )(b_gates, qm, km, vm, om, gates, norm_w)


def _mlstm_sample_kernel(ls, group, bg_ref, q_ref, k_ref, v_ref, o_ref, g_ref, nw_ref, c_in, n_in, m_in,
                         h_ref, c_ref, n_ref, m_ref):
    pair = 2 if group % 2 == 0 else 1

    def seq_pair(s2, carry):
        for u in range(pair):
            seq(s2 * pair + u)
        return carry

    def seq(s):
        rows = pl.ds(pl.multiple_of(s * ls, ls), ls)
        for h in range(H_M):
            cols = slice(h * DH_M, (h + 1) * DH_M)
            out, c_new, n_new, m_new = _mlstm_chunk(
                q_ref[rows, cols].astype(BF16), k_ref[rows, cols].astype(BF16), v_ref[rows, cols].astype(BF16),
                o_ref[rows, cols],
                g_ref[rows, h:h + 1] + bg_ref[h], g_ref[rows, H_M + h:H_M + h + 1] + bg_ref[H_M + h],
                c_in[s, h], n_in[s, h:h + 1, :], m_in[s, h:h + 1, 0:1], nw_ref[:, cols])
            h_ref[rows, cols] = out
            c_ref[s, h] = c_new
            n_ref[s, h:h + 1, :] = n_new
            m_ref[s, h:h + 1, :] = jnp.broadcast_to(m_new, (1, m_ref.shape[2]))

    lax.fori_loop(0, group // pair, seq_pair, 0)


def _mlstm_sample(b_gates, qm, km, vm, om, gates, norm_w, state_c, state_n, state_m, ls, group):
    bs = state_c.shape[0]
    assert bs % group == 0
    tok = lambda w: pl.BlockSpec((group * ls, w), lambda i: (i, 0))
    c_spec = pl.BlockSpec((group, H_M, DH_M, DH_M), lambda i: (i, 0, 0, 0))
    n_spec = pl.BlockSpec((group, H_M, DH_M), lambda i: (i, 0, 0))
    m_spec = pl.BlockSpec((group, H_M, LANES), lambda i: (i, 0, 0))
    sds = jax.ShapeDtypeStruct
    return pl.pallas_call(
        functools.partial(_mlstm_sample_kernel, ls, group),
        grid=(bs // group,),
        in_specs=[pl.BlockSpec(memory_space=pltpu.SMEM),
                  tok(MLSTM_W), tok(MLSTM_W), tok(MLSTM_W), tok(MLSTM_W), tok(GATE_PAD),
                  _const_spec(norm_w.shape), c_spec, n_spec, m_spec],
        out_specs=[tok(MLSTM_W), c_spec, n_spec, m_spec],
        out_shape=[sds((bs * ls, MLSTM_W), F32), sds(state_c.shape, F32),
                   sds(state_n.shape, F32), sds((bs, H_M, LANES), F32)],
        compiler_params=_params("parallel"),
        name="mlstm_sample",
    )(b_gates, qm, km, vm, om, gates, norm_w, state_c, state_n, state_m)


def _mix_kernel(head_major, x_ref, hm_ref, hd_ref, w_ref, g_ref, b_ref, y_ref, yb_ref):
    half = x_ref.shape[0] // 2
    halves = [slice(0, half), slice(half, 2 * half)]

    def project(rows):
        if head_major:
            hd = jnp.concatenate([hd_ref[h, rows, :] for h in range(H_D)], axis=-1)
        else:
            hd = hd_ref[rows, :]
        return _dot(hm_ref[rows, :].astype(BF16), w_ref[0:MLSTM_W, :]) + _dot(hd.astype(BF16), w_ref[MLSTM_W:, :])

    def finish(rows, mix):
        y = _layer_norm(ALPHA * x_ref[rows, :] + mix, g_ref[...], b_ref[...])
        y_ref[rows, :] = y
        yb_ref[rows, :] = y.astype(BF16)

    mixes = [project(rows) for rows in halves]
    for rows, mix in zip(halves, mixes):
        finish(rows, mix)


def _mix(x, hm, hd, w_out, ln_g, ln_b, tm, head_major):
    t = x.shape[0]
    assert t % tm == 0
    row = lambda w: pl.BlockSpec((tm, w), lambda i: (i, 0))
    hd_spec = pl.BlockSpec((H_D, tm, DV_D), lambda i: (0, i, 0)) if head_major else row(DIFF_W)
    return pl.pallas_call(
        functools.partial(_mix_kernel, head_major),
        grid=(t // tm,),
        in_specs=[row(D_MODEL), row(MLSTM_W), hd_spec, _resident_spec(w_out.shape),
                  _const_spec(ln_g.shape), _const_spec(ln_b.shape)],
        out_specs=[row(D_MODEL), row(D_MODEL)],
        out_shape=[jax.ShapeDtypeStruct((t, D_MODEL), F32), jax.ShapeDtypeStruct((t, D_MODEL), BF16)],
        compiler_params=_params("parallel"),
        name="mix_ln",
    )(x, hm, hd, w_out, ln_g, ln_b)


def _ffn_up_math(xb_ref, wg_ref, wu_ref, h_ref):
    xb = xb_ref[...]
    h_ref[...] = (jax.nn.silu(_dot(xb, wg_ref[...])) * _dot(xb, wu_ref[...])).astype(h_ref.dtype)


def _ffn_up_kernel(xb_ref, wg_ref, wu_ref, h_ref):
    _ffn_up_math(xb_ref, wg_ref, wu_ref, h_ref)


def _ffn_up(xb, w_gate, w_up, tm, tf):
    t = xb.shape[0]
    assert t % tm == 0 and D_FF % tf == 0
    return pl.pallas_call(
        _ffn_up_kernel,
        grid=(t // tm, D_FF // tf),
        in_specs=[pl.BlockSpec((tm, D_MODEL), lambda i, j: (i, 0)),
                  pl.BlockSpec((D_MODEL, tf), lambda i, j: (0, j)),
                  pl.BlockSpec((D_MODEL, tf), lambda i, j: (0, j))],
        out_specs=pl.BlockSpec((tm, tf), lambda i, j: (i, j)),
        out_shape=jax.ShapeDtypeStruct((t, D_FF), BF16),
        compiler_params=_params("parallel", "parallel"),
        name="ffn_up",
    )(xb, w_gate, w_up)


def _ffn_up_attn_kernel(lam_init, n_pages, ls, n_seq, pt_ref, xb_ref, wg_ref, wu_ref,
                        q_ref, kn_ref, vn_ref, sbias_ref, lam_ref, nw_ref, *rest):
    del pt_ref
    k_pages = rest[:n_pages]
    v_pages = rest[n_pages:2 * n_pages]
    h_ref, o_ref = rest[2 * n_pages:]
    step = pl.program_id(0) * pl.num_programs(1) + pl.program_id(1)

    @pl.when(step >= n_seq)
    def _():
        _ffn_up_math(xb_ref, wg_ref, wu_ref, h_ref)

    @pl.when(step < n_seq)
    def _():
        n_chunks = n_pages // 2
        kc = D_MODEL // n_chunks
        todo = [(name, w_ref, c) for c in range(n_chunks) for name, w_ref in (("gate", wg_ref), ("up", wu_ref))]
        acc = {"gate": None, "up": None}
        reads = [0]

        def after_page():
            reads[0] += 1
            if reads[0] % 2 == 0 and todo:
                name, w_ref, c = todo.pop(0)
                d = _dot(xb_ref[:, c * kc:(c + 1) * kc], w_ref[c * kc:(c + 1) * kc, :])
                acc[name] = d if acc[name] is None else acc[name] + d

        _attn_sample_body(lam_init, ls, q_ref, kn_ref, vn_ref, sbias_ref, lam_ref, nw_ref, k_pages, v_pages, o_ref,
                          after_page)
        assert not todo
        h_ref[...] = (jax.nn.silu(acc["gate"]) * acc["up"]).astype(h_ref.dtype)


def _ffn_up_attn_sample(xb, w_gate, w_up, tm, tf, page_table, q, k_new, v_new, cache_k, cache_v, sbias,
                        lam_params, norm_w, ls, lam_init):
    t = xb.shape[0]
    bs, n_pages = page_table.shape
    n_i, n_j = t // tm, D_FF // tf
    assert t % tm == 0 and D_FF % tf == 0 and n_i * n_j >= bs
    rows_per_page = PAGE_SIZE * H_D
    cache_k = cache_k.reshape(-1, DK_D)
    cache_v = cache_v.reshape(-1, DV_D)

    def seq(i, j):
        return jnp.minimum(i * n_j + j, bs - 1)

    seq_spec = pl.BlockSpec((ls, DIFF_W), lambda i, j, pt: (seq(i, j), 0))
    new_spec = pl.BlockSpec((ls * H_D, DK_D), lambda i, j, pt: (seq(i, j), 0))

    def page_spec(p):
        return pl.BlockSpec((rows_per_page, DK_D), lambda i, j, pt: (pt[seq(i, j), p], 0))

    def const(shape):
        return pl.BlockSpec(shape, lambda i, j, pt: (0,) * len(shape))

    grid_spec = pltpu.PrefetchScalarGridSpec(
        num_scalar_prefetch=1,
        grid=(n_i, n_j),
        in_specs=[pl.BlockSpec((tm, D_MODEL), lambda i, j, pt: (i, 0)),
                  pl.BlockSpec((D_MODEL, tf), lambda i, j, pt: (0, j)),
                  pl.BlockSpec((D_MODEL, tf), lambda i, j, pt: (0, j)),
                  seq_spec, new_spec, new_spec,
                  const(sbias.shape), const(lam_params.shape), const(norm_w.shape)]
        + [page_spec(p) for p in range(n_pages)] * 2,
        out_specs=[pl.BlockSpec((tm, tf), lambda i, j, pt: (i, j)), seq_spec],
    )
    return pl.pallas_call(
        functools.partial(_ffn_up_attn_kernel, lam_init, n_pages, ls, bs),
        grid_spec=grid_spec,
        out_shape=[jax.ShapeDtypeStruct((t, D_FF), BF16), jax.ShapeDtypeStruct((bs * ls, DIFF_W), F32)],
        compiler_params=_params("arbitrary", "arbitrary"),
        name="ffn_up_attn_sample",
    )(page_table, xb, w_gate, w_up, q, k_new, v_new, sbias, lam_params, norm_w,
      *([cache_k] * n_pages), *([cache_v] * n_pages))


def _ffn_down_kernel(x_ref, h_ref, wd_ref, g_ref, b_ref, y_ref):
    y_ref[...] = _layer_norm(ALPHA * x_ref[...] + _dot(h_ref[...], wd_ref[...]), g_ref[...], b_ref[...])


def _ffn_down(x, h, w_down, ln_g, ln_b, tm):
    t = x.shape[0]
    assert t % tm == 0
    row = lambda w: pl.BlockSpec((tm, w), lambda i: (i, 0))
    return pl.pallas_call(
        _ffn_down_kernel,
        grid=(t // tm,),
        in_specs=[row(D_MODEL), row(D_FF), _resident_spec(w_down.shape),
                  _const_spec(ln_g.shape), _const_spec(ln_b.shape)],
        out_specs=row(D_MODEL),
        out_shape=jax.ShapeDtypeStruct((t, D_MODEL), F32),
        compiler_params=_params("parallel"),
        name="ffn_down_ln",
    )(x, h, w_down, ln_g, ln_b)


def _tiles(seq):
    return dict(
        cast_cols=512,
        proj_tm=256,
        attn_blk=min(256, seq),
        attn_group=4,
        mlstm_chunk=min(512, seq),
        sample_group=4,
        mix_tm=512,
        ffn_up_tm=1024,
        ffn_up_tf=256,
        ffn_down_tm=256,
    )


def kernel(x_prompt, x_sample, cache_k, cache_v, state_C, state_n, state_m, page_table, rel_bias, w_in, b_gates, lambda_q1, lambda_k1, lambda_q2, lambda_k2, diff_norm_w, mlstm_norm_w, w_out, ln1_g, ln1_b, w_gate, w_up, w_down, ln2_g, ln2_b):
    B, S, _ = x_prompt.shape
    Bs, Ls, _ = x_sample.shape
    n_pages = page_table.shape[1]
    past = n_pages * cache_k.shape[2]
    assert w_in.shape[0] == DEPTH == 1 and cache_k.shape[2] == PAGE_SIZE
    tl = _tiles(S)
    l = 0
    lam_init = 0.8 - 0.6 * math.exp(-0.3 * l)

    wt = w_in[l].T
    wd = _transpose_cast(wt, 0, DIFF_COLS, tl["cast_cols"])
    wm = _transpose_cast(wt, DIFF_COLS, MLSTM_COLS, tl["cast_cols"])
    wg = _gate_rows(wt, DIFF_COLS + MLSTM_COLS)
    lam_params = jnp.stack([lambda_q1[l], lambda_k1[l], lambda_q2[l], lambda_k2[l]], 0)
    dnw = diff_norm_w[l].reshape(1, DV_D)
    mnw = mlstm_norm_w[l].reshape(1, MLSTM_W)
    g1, b1 = ln1_g[l].reshape(1, D_MODEL), ln1_b[l].reshape(1, D_MODEL)
    g2, b2 = ln2_g[l].reshape(1, D_MODEL), ln2_b[l].reshape(1, D_MODEL)
    bg = b_gates[l]

    pbias, sbias = _bias_tables(rel_bias, tl["attn_blk"], past, Ls)
    sbias = sbias.reshape(H_D * 2 * Ls, past + PAGE_SIZE)

    xp = x_prompt.reshape(B * S, D_MODEL)
    qt, k32, k16, v32, vt, qm, km, vm, om, gates, w_o = _in_proj(xp, wd, wm, wg, tl["proj_tm"], True,
                                                                 riders=(w_out[l],))
    hd, w_g, w_u, w_d = _attn_prompt(qt, k16, vt, pbias, lam_params, dnw.reshape(DV_D, 1), B, S, tl["attn_blk"],
                                     tl["attn_group"], lam_init, riders=(w_gate[l], w_up[l], w_down[l]))
    hm, c_p, n_p, m_p = _mlstm_prompt(bg, qm, km, vm, om, gates, mnw, B, S, tl["mlstm_chunk"])
    x1, x1b = _mix(xp, hm, hd, w_o, g1, b1, tl["mix_tm"], True)

    xs = x_sample.reshape(Bs * Ls, D_MODEL)
    qs, ks, vs, qms, kms, vms, oms, gates_s = _in_proj(xs, wd, wm, wg, tl["proj_tm"], False)
    up_p, hd_s = _ffn_up_attn_sample(x1b, w_g, w_u, min(tl["ffn_up_tm"], B * S), tl["ffn_up_tf"], page_table,
                                     qs, ks, vs, cache_k, cache_v, sbias, lam_params, dnw, Ls, lam_init)
    y_p = _ffn_down(x1, up_p, w_d, g2, b2, tl["ffn_down_tm"])
    m_in = jnp.broadcast_to(state_m[l][:, :, None], (Bs, H_M, LANES))
    hm_s, c_s, n_s, m_s = _mlstm_sample(bg, qms, kms, vms, oms, gates_s, mnw,
                                        state_C[l], state_n[l], m_in, Ls, tl["sample_group"])
    x1s, x1sb = _mix(xs, hm_s, hd_s, w_o, g1, b1, tl["mix_tm"], False)
    up_s = _ffn_up(x1sb, w_g, w_u, min(tl["ffn_up_tm"], Bs * Ls), tl["ffn_up_tf"])
    y_s = _ffn_down(x1s, up_s, w_d, g2, b2, tl["ffn_down_tm"])

    return (y_p.reshape(B, S, D_MODEL), y_s.reshape(Bs, Ls, D_MODEL),
            k32.reshape(1, B, S, H_D, DK_D), v32.reshape(1, B, S, H_D, DV_D),
            c_p[None], n_p[None], m_p[None, :, :, 0],
            ks.reshape(1, Bs, Ls, H_D, DK_D), vs.reshape(1, Bs, Ls, H_D, DV_D),
            c_s[None], n_s[None], m_s[None, :, :, 0])
```

```python
import functools
import math

import numpy as np
import jax
import jax.numpy as jnp
from jax import lax
from jax.experimental import pallas as pl
from jax.experimental.pallas import tpu as pltpu

D_MODEL = 2048
DEPTH = 1
PAGE_SIZE = 128
H_D = 8
DH_HALF = 64
DK_D = 2 * DH_HALF
DV_D = 2 * DH_HALF
DIFF_W = H_D * DV_D
H_M = 4
DH_M = 256
MLSTM_W = H_M * DH_M
D_FF = -(-8 * D_MODEL // (3 * 256)) * 256
N_BUCKETS = 32
MAX_DIST = 128
ALPHA = (2 * DEPTH) ** 0.25
LN_EPS = 1e-5
RMS_EPS = 1e-6
N_GATES = 2 * H_M
LANES = 128
SUBLANES = 8
GATE_PAD = LANES
DIFF_COLS = 3 * DIFF_W
MLSTM_COLS = 4 * MLSTM_W
MASK_VALUE = -1e30
LOG2E = math.log2(math.e)

VMEM_LIMIT_BYTES = 56 * 1024 * 1024

BF16 = jnp.bfloat16
F32 = jnp.float32


def _dot(a, b):
    return jnp.dot(a, b, preferred_element_type=F32)


def _dot_nt(a, b):
    return lax.dot_general(a, b, (((1,), (1,)), ((), ())), preferred_element_type=F32)


def _dot_tn(a, b):
    return lax.dot_general(a, b, (((0,), (0,)), ((), ())), preferred_element_type=F32)


def _params(*semantics):
    return pltpu.CompilerParams(dimension_semantics=semantics, vmem_limit_bytes=VMEM_LIMIT_BYTES)


def _const_spec(shape):
    n = len(shape)
    return pl.BlockSpec(shape, lambda *_: (0,) * n)


def _cast_riders(riders, n_steps, step_of):
    in_specs, out_specs, out_shapes = [], [], []
    for w in riders:
        rows, cols = w.shape
        assert rows % n_steps == 0 and (rows // n_steps) % 16 == 0
        spec = pl.BlockSpec((rows // n_steps, cols), lambda *g: (step_of(*g), 0))
        in_specs.append(spec)
        out_specs.append(spec)
        out_shapes.append(jax.ShapeDtypeStruct(w.shape, BF16))
    return in_specs, out_specs, out_shapes


def _cast_rider_blocks(in_refs, out_refs):
    for src, dst in zip(in_refs, out_refs):
        dst[...] = src[...].astype(BF16)


def _resident_spec(shape):
    n = len(shape)
    return pl.BlockSpec(shape, lambda *_: (0,) * n, pipeline_mode=pl.Buffered(1))


def _bucket_np(dist):
    n = np.maximum(dist, 0)
    max_exact = N_BUCKETS // 2
    nf = np.maximum(n, 1).astype(np.float32)
    large = max_exact + (np.log(nf / np.float32(max_exact)) / np.float32(math.log(MAX_DIST / max_exact))
                         * np.float32(N_BUCKETS - max_exact)).astype(np.int32)
    large = np.minimum(large, N_BUCKETS - 1)
    out = np.where(n < max_exact, n, large).astype(np.int32)
    return np.where(dist < 0, -1, out).astype(np.int32)


def _bias_kernel(rb_ref, pb_ref, sb_ref, pbias_ref, sbias_ref):
    h = pl.program_id(0)

    def lookup(bk, shift, scale):
        acc = jnp.full(bk.shape, MASK_VALUE, F32)
        for b in range(N_BUCKETS):
            acc = jnp.where(bk == b, (rb_ref[b, h] - shift) * scale, acc)
        return acc

    for t in range(pb_ref.shape[0]):
        pbias_ref[0, t] = lookup(pb_ref[t], rb_ref[N_BUCKETS - 1, h], LOG2E)
    sbias_ref[0] = lookup(sb_ref[...], 0.0, 1.0)


def _bias_tables(rel_bias, blk, past, dec_seq):
    k = np.arange(blk)[:, None]
    q = np.arange(blk)[None, :]
    pb = np.stack([_bucket_np(blk + q - k), _bucket_np(q - k)], 0)
    far = N_BUCKETS - 1
    assert (_bucket_np(np.arange(MAX_DIST, 4 * past + 4 * blk)) == far).all()
    assert blk >= MAX_DIST and PAGE_SIZE >= MAX_DIST
    qpos = past + np.arange(dec_seq)[:, None]
    last = _bucket_np(qpos - (past - PAGE_SIZE + np.arange(PAGE_SIZE))[None, :])
    new = _bucket_np(qpos - (past + np.arange(PAGE_SIZE))[None, :])
    new[:, dec_seq:] = -1
    sb = np.concatenate([np.full((dec_seq, past - PAGE_SIZE), far, np.int32), last, new], 1)
    sb = np.concatenate([sb, sb], 0)
    return pl.pallas_call(
        _bias_kernel,
        grid=(H_D,),
        in_specs=[pl.BlockSpec(memory_space=pltpu.SMEM),
                  _const_spec(pb.shape), _const_spec(sb.shape)],
        out_specs=[pl.BlockSpec((1,) + pb.shape, lambda h: (h, 0, 0, 0)),
                   pl.BlockSpec((1,) + sb.shape, lambda h: (h, 0, 0))],
        out_shape=[jax.ShapeDtypeStruct((H_D,) + pb.shape, F32),
                   jax.ShapeDtypeStruct((H_D,) + sb.shape, F32)],
        compiler_params=_params("arbitrary"),
        name="bias_tables",
    )(rel_bias, jnp.asarray(pb), jnp.asarray(sb))


Q_SCALE = DH_HALF ** -0.5
K_SCALE = DH_M ** -0.5


def _head_rows(ref, h, n_tokens):
    return ref[pl.ds(h, n_tokens, stride=H_D), :]


def _store_head_rows(ref, x):
    for h in range(H_D):
        ref[pl.ds(h, x.shape[0], stride=H_D), :] = x[:, h * DK_D:(h + 1) * DK_D]


def _load_head_rows(ref, n_tokens):
    return jnp.concatenate([_head_rows(ref, h, n_tokens) for h in range(H_D)], axis=1)


def _transpose_cast_kernel(wt_ref, w_ref):
    w_ref[...] = wt_ref[...].T.astype(BF16)


def _transpose_cast(wt, first_col, n_cols, blk):
    d = wt.shape[1]
    assert first_col % blk == 0 and n_cols % blk == 0
    return pl.pallas_call(
        _transpose_cast_kernel,
        grid=(n_cols // blk,),
        in_specs=[pl.BlockSpec((blk, d), lambda j: (first_col // blk + j, 0))],
        out_specs=pl.BlockSpec((d, blk), lambda j: (0, j)),
        out_shape=jax.ShapeDtypeStruct((d, n_cols), BF16),
        compiler_params=_params("parallel"),
        name="transpose_cast",
    )(wt)


def _gate_rows_kernel(wt_ref, o_ref):
    zeros = jnp.zeros((GATE_PAD - N_GATES, wt_ref.shape[1]), F32)
    o_ref[...] = jnp.concatenate([wt_ref[...], zeros], axis=0).astype(BF16)


def _gate_rows(wt, first_row):
    d = wt.shape[1]
    assert first_row % N_GATES == 0
    return pl.pallas_call(
        _gate_rows_kernel,
        grid=(1,),
        in_specs=[pl.BlockSpec((N_GATES, d), lambda i: (first_row // N_GATES, 0))],
        out_specs=pl.BlockSpec((GATE_PAD, d), lambda i: (0, 0)),
        out_shape=jax.ShapeDtypeStruct((GATE_PAD, d), BF16),
        compiler_params=_params("arbitrary"),
        name="gate_rows",
    )(wt)


def _in_proj_prompt_kernel(x_ref, wd_ref, wm_ref, wg_ref, rider_ref,
                           qt_ref, k32_ref, k16_ref, v32_ref, vt_ref,
                           qm_ref, km_ref, vm_ref, om_ref, g_ref, rider_out_ref):
    _cast_rider_blocks([rider_ref], [rider_out_ref])
    x = x_ref[...].astype(BF16)
    r = _dot(x, wd_ref[:, 0:DIFF_W]) * (Q_SCALE * LOG2E)
    for h in range(H_D):
        qt_ref[h] = r[:, h * DK_D:(h + 1) * DK_D].T.astype(BF16)
    r = _dot(x, wd_ref[:, DIFF_W:2 * DIFF_W])
    _store_head_rows(k32_ref, r)
    for h in range(H_D):
        k16_ref[h] = r[:, h * DK_D:(h + 1) * DK_D].astype(BF16)
    r = _dot(x, wd_ref[:, 2 * DIFF_W:3 * DIFF_W])
    _store_head_rows(v32_ref, r)
    for h in range(H_D):
        vt_ref[h] = r[:, h * DV_D:(h + 1) * DV_D].T.astype(BF16)
    qm_ref[...] = _dot(x, wm_ref[:, 0:MLSTM_W]).astype(BF16)
    km_ref[...] = (_dot(x, wm_ref[:, MLSTM_W:2 * MLSTM_W]) * K_SCALE).astype(BF16)
    vm_ref[...] = _dot(x, wm_ref[:, 2 * MLSTM_W:3 * MLSTM_W]).astype(BF16)
    om_ref[...] = _dot(x, wm_ref[:, 3 * MLSTM_W:4 * MLSTM_W]).astype(BF16)
    g_ref[...] = _dot_nt(x, wg_ref[...])


def _in_proj_sample_kernel(x_ref, wd_ref, wm_ref, wg_ref,
                           q_ref, k_ref, v_ref, qm_ref, km_ref, vm_ref, om_ref, g_ref):
    x = x_ref[...].astype(BF16)
    q_ref[...] = _dot(x, wd_ref[:, 0:DIFF_W]) * Q_SCALE
    _store_head_rows(k_ref, _dot(x, wd_ref[:, DIFF_W:2 * DIFF_W]))
    _store_head_rows(v_ref, _dot(x, wd_ref[:, 2 * DIFF_W:3 * DIFF_W]))
    qm_ref[...] = _dot(x, wm_ref[:, 0:MLSTM_W])
    km_ref[...] = _dot(x, wm_ref[:, MLSTM_W:2 * MLSTM_W]) * K_SCALE
    vm_ref[...] = _dot(x, wm_ref[:, 2 * MLSTM_W:3 * MLSTM_W])
    om_ref[...] = _dot(x, wm_ref[:, 3 * MLSTM_W:4 * MLSTM_W])
    g_ref[...] = _dot_nt(x, wg_ref[...])


def _in_proj(x, wd, wm, wg, tm, prompt, riders=()):
    t = x.shape[0]
    assert t % tm == 0
    r_in, r_out, r_shapes = _cast_riders(riders, t // tm, lambda i: i)
    row = lambda w: pl.BlockSpec((tm, w), lambda i: (i, 0))
    heads = pl.BlockSpec((H_D, tm, DK_D), lambda i: (0, i, 0))
    cache = pl.BlockSpec((tm * H_D, DK_D), lambda i: (i, 0))
    sds = jax.ShapeDtypeStruct
    if prompt:
        body = _in_proj_prompt_kernel
        heads_t = pl.BlockSpec((H_D, DK_D, tm), lambda i: (0, 0, i))
        out_specs = [heads_t, cache, heads, cache, heads_t] + [row(MLSTM_W)] * 4 + [row(GATE_PAD)]
        out_shape = [sds((H_D, DK_D, t), BF16), sds((t * H_D, DK_D), F32), sds((H_D, t, DK_D), BF16),
                     sds((t * H_D, DV_D), F32), sds((H_D, DV_D, t), BF16)]
        out_shape += [sds((t, MLSTM_W), BF16)] * 4 + [sds((t, GATE_PAD), F32)]
    else:
        body = _in_proj_sample_kernel
        out_specs = [row(DIFF_W), cache, cache] + [row(MLSTM_W)] * 4 + [row(GATE_PAD)]
        out_shape = ([sds((t, DIFF_W), F32)] + [sds((t * H_D, DK_D), F32)] * 2 + [sds((t, MLSTM_W), F32)] * 4
                     + [sds((t, GATE_PAD), F32)])
    return pl.pallas_call(
        body,
        grid=(t // tm,),
        in_specs=[row(D_MODEL), _resident_spec(wd.shape), _resident_spec(wm.shape), _resident_spec(wg.shape)] + r_in,
        out_specs=out_specs + r_out,
        out_shape=out_shape + r_shapes,
        compiler_params=_params("parallel"),
        name="in_proj_prompt" if prompt else "in_proj_sample",
    )(x, wd, wm, wg, *riders)


def _lambda_value(lam_ref, lam_init):
    lp = lam_ref[...]
    e1 = jnp.exp(jnp.sum(lp[0:1] * lp[1:2], axis=1, keepdims=True))
    e2 = jnp.exp(jnp.sum(lp[2:3] * lp[3:4], axis=1, keepdims=True))
    return e1 - e2 + lam_init


def _head_rms(o, w):
    return o * lax.rsqrt(jnp.mean(o * o, axis=-1, keepdims=True) + RMS_EPS) * w


def _layer_norm(x, g, b):
    xc = x - jnp.mean(x, axis=-1, keepdims=True)
    var = jnp.mean(xc * xc, axis=-1, keepdims=True)
    return xc * lax.rsqrt(var + LN_EPS) * g + b


def _attn_prompt_kernel(lam_init, tq, nq, group, n_riders, qt_ref, k_ref, vt_ref, pbias_ref, lam_ref, nw_ref, *rest):
    o_ref = rest[n_riders]
    _cast_rider_blocks(rest[:n_riders], rest[n_riders + 1:])
    i = pl.program_id(1)
    lam = _lambda_value(lam_ref, lam_init)
    chan = lax.broadcasted_iota(jnp.int32, (DK_D, tq), 0)
    first_map = chan < DH_HALF

    def fold8(op, s):
        return op(s.reshape(s.shape[0] // SUBLANES, SUBLANES, s.shape[1]), axis=0)

    def process(ii):
        n = (ii + 1) * tq
        n_far = max(n - 2 * tq, 0)

        def scores(h):
            qt = qt_ref[h]
            zero = jnp.zeros_like(qt)
            q2t = jnp.concatenate([jnp.where(first_map, qt, zero), jnp.where(first_map, zero, qt)], axis=1)
            parts = []
            if n_far:
                parts.append(_dot(k_ref[h, 0:n_far, :], q2t))
            if ii >= 1:
                bs = pbias_ref[h, 0]
                parts.append(_dot(k_ref[h, n - 2 * tq:n - tq, :], q2t) + jnp.concatenate([bs, bs], axis=1))
            bd = pbias_ref[h, 1]
            parts.append(_dot(k_ref[h, n - tq:n, :], q2t) + jnp.concatenate([bd, bd], axis=1))
            return parts

        def softmax(parts):
            m8 = fold8(jnp.max, parts[0])
            for s in parts[1:]:
                m8 = jnp.maximum(m8, fold8(jnp.max, s))
            m = jnp.max(m8, axis=0, keepdims=True)
            probs = [jnp.exp2(s - m) for s in parts]
            l8 = fold8(jnp.sum, probs[0])
            for p in probs[1:]:
                l8 = l8 + fold8(jnp.sum, p)
            l = jnp.sum(l8, axis=0, keepdims=True)
            return jnp.concatenate([p.astype(BF16) for p in probs], axis=0), l

        def values(h, p_all, l):
            o = _dot(vt_ref[h, :, 0:n], p_all) / l
            o = o[:, :tq] - lam * o[:, tq:]
            o = o * lax.rsqrt(jnp.mean(o * o, axis=0, keepdims=True) + RMS_EPS) * (nw_ref[...] * (1.0 - lam_init))
            o_ref[h] = o.T.astype(o_ref.dtype)

        def head_group(g, carry):
            hs = [g * group + u for u in range(group)]
            sc, sm = {}, {}
            for t in range(group + 2):
                if t < group:
                    sc[t] = scores(hs[t])
                if 0 <= t - 1 < group:
                    sm[t - 1] = softmax(sc.pop(t - 1))
                if 0 <= t - 2 < group:
                    values(hs[t - 2], *sm.pop(t - 2))
            return carry

        lax.fori_loop(0, H_D // group, head_group, 0)

    for ii in range(nq):
        pl.when(i == ii)(functools.partial(process, ii))


def _attn_prompt(qt, k, vt, pbias, lam_params, norm_w, batch, seq, blk, group, lam_init, riders):
    nq = seq // blk
    assert H_D % group == 0
    r_in, r_out, r_shapes = _cast_riders(riders, batch * nq, lambda b, i: b * nq + i)
    return pl.pallas_call(
        functools.partial(_attn_prompt_kernel, lam_init, blk, nq, group, len(riders)),
        grid=(batch, nq),
        in_specs=[pl.BlockSpec((H_D, DK_D, blk), lambda b, i: (0, 0, b * nq + i)),
                  pl.BlockSpec((H_D, seq, DK_D), lambda b, i: (0, b, 0)),
                  pl.BlockSpec((H_D, DV_D, seq), lambda b, i: (0, 0, b)),
                  _const_spec(pbias.shape),
                  _const_spec(lam_params.shape),
                  _const_spec(norm_w.shape)] + r_in,
        out_specs=[pl.BlockSpec((H_D, blk, DV_D), lambda b, i: (0, b * nq + i, 0))] + r_out,
        out_shape=[jax.ShapeDtypeStruct((H_D, batch * seq, DV_D), BF16)] + r_shapes,
        compiler_params=_params("parallel", "parallel"),
        name="attn_prompt",
    )(qt, k, vt, pbias, lam_params, norm_w, *riders)


def _attn_sample_body(lam_init, ls, q_ref, kn_ref, vn_ref, sbias_ref, lam_ref, nw_ref, k_pages, v_pages, o_ref,
                      after_page=None):
    n_pages = len(k_pages)
    rows = 2 * ls * H_D
    lam = _lambda_value(lam_ref, lam_init)

    q = q_ref[...]
    qt = jnp.concatenate([q] * (2 * H_D), axis=0)
    rowi = lax.broadcasted_iota(jnp.int32, (rows, DIFF_W), 0)
    coli = lax.broadcasted_iota(jnp.int32, (rows, DIFF_W), 1)
    q_bd = jnp.where(coli // DH_HALF == rowi // ls, qt, 0.0).astype(BF16)

    pad = jnp.zeros((PAGE_SIZE - ls, DIFF_W), F32)

    def page(ref):
        return _load_head_rows(ref, PAGE_SIZE).astype(BF16)

    k_new = jnp.concatenate([_load_head_rows(kn_ref, ls), pad], axis=0).astype(BF16)
    v_new = jnp.concatenate([_load_head_rows(vn_ref, ls), pad], axis=0).astype(BF16)

    def pages(refs):
        out = []
        for r in refs:
            out.append(page(r))
            if after_page is not None:
                after_page()
        return out

    k_all = jnp.concatenate(pages(k_pages) + [k_new], axis=0)
    n_keys = k_all.shape[0]
    s = _dot_nt(q_bd, k_all) + sbias_ref[...]
    m = jnp.max(s, axis=1, keepdims=True)
    p = jnp.exp(s - m)
    l = jnp.sum(p, axis=1, keepdims=True)
    second_map = (lax.broadcasted_iota(jnp.int32, (rows, 1), 0) // ls) % 2 == 1
    fac = jnp.where(second_map, -lam, 1.0) / l
    p3 = (p * fac).reshape(H_D, 2 * ls, n_keys)
    w = (p3[:, :ls] + p3[:, ls:]).reshape(H_D * ls, n_keys).astype(BF16)
    v_all = jnp.concatenate(pages(v_pages) + [v_new], axis=0)
    acc = _dot(w, v_all)
    for h in range(H_D):
        o = acc[h * ls:(h + 1) * ls, h * DV_D:(h + 1) * DV_D]
        o_ref[:, h * DV_D:(h + 1) * DV_D] = _head_rms(o, nw_ref[...]) * (1.0 - lam_init)


def _mlstm_chunk(q, k, v, o_pre, i_pre, f_pre, c_prev, n_prev, m_prev, norm_w):
    L = q.shape[0]
    t_idx = lax.broadcasted_iota(jnp.int32, (L, L), 0)
    s_idx = lax.broadcasted_iota(jnp.int32, (L, L), 1)
    causal = s_idx <= t_idx
    eye = s_idx == t_idx

    def to_row(col):
        return jnp.sum(jnp.where(eye, col, 0.0), axis=0, keepdims=True)

    it_col = i_pre
    lf_col = jax.nn.log_sigmoid(f_pre)
    it_row = to_row(it_col)
    b_row = jnp.sum(jnp.where(t_idx <= s_idx, lf_col, 0.0), axis=0, keepdims=True)
    b_col = jnp.sum(jnp.where(causal, to_row(lf_col), 0.0), axis=1, keepdims=True)

    log_d = jnp.where(causal, b_col - b_row + it_row, -jnp.inf)
    m_t = jnp.maximum(b_col + m_prev, jnp.max(log_d, axis=1, keepdims=True))
    dmat = jnp.exp(log_d - m_t)
    inter = jnp.exp(b_col + m_prev - m_t)

    w = _dot_nt(q, k) * dmat
    kf = k.astype(F32)
    num = inter * _dot_nt(q, c_prev.astype(BF16)) + _dot(w.astype(BF16), v)
    den = inter * jnp.sum(q.astype(F32) * n_prev, axis=1, keepdims=True) + jnp.sum(w, axis=1, keepdims=True)
    h = num / jnp.maximum(jnp.abs(den), jnp.exp(-m_t))
    h = _head_rms(h, norm_w) * jax.nn.sigmoid(o_pre.astype(F32))

    m_new = m_t[L - 1:L]
    b_last = b_col[L - 1:L]
    g = jnp.exp(b_last - b_col + it_col - m_new)
    decay = jnp.exp(b_last + m_prev - m_new)
    gk = g * kf
    c_new = decay * c_prev + _dot_tn(v, gk.astype(BF16))
    n_new = decay * n_prev + jnp.sum(gk, axis=0, keepdims=True)
    return h, c_new, n_new, m_new


def _mlstm_prompt_kernel(bg_ref, q_ref, k_ref, v_ref, o_ref, g_ref, nw_ref, h_ref, c_ref, n_ref, m_ref):
    @pl.when(pl.program_id(1) == 0)
    def _():
        c_ref[...] = jnp.zeros_like(c_ref)
        n_ref[...] = jnp.zeros_like(n_ref)
        m_ref[...] = jnp.zeros_like(m_ref)

    for h in range(H_M):
        cols = slice(h * DH_M, (h + 1) * DH_M)
        out, c_new, n_new, m_new = _mlstm_chunk(
            q_ref[:, cols], k_ref[:, cols], v_ref[:, cols], o_ref[:, cols],
            g_ref[:, h:h + 1] + bg_ref[h], g_ref[:, H_M + h:H_M + h + 1] + bg_ref[H_M + h],
            c_ref[0, h], n_ref[0, h:h + 1, :], m_ref[0, h:h + 1, 0:1], nw_ref[:, cols])
        h_ref[:, cols] = out.astype(h_ref.dtype)
        c_ref[0, h] = c_new
        n_ref[0, h:h + 1, :] = n_new
        m_ref[0, h:h + 1, :] = jnp.broadcast_to(m_new, (1, m_ref.shape[2]))


def _mlstm_prompt(b_gates, qm, km, vm, om, gates, norm_w, batch, seq, chunk):
    nc = seq // chunk
    tok = lambda w: pl.BlockSpec((chunk, w), lambda b, c: (b * nc + c, 0))
    sds = jax.ShapeDtypeStruct
    return pl.pallas_call(
        _mlstm_prompt_kernel,
        grid=(batch, nc),
        in_specs=[pl.BlockSpec(memory_space=pltpu.SMEM),
                  tok(MLSTM_W), tok(MLSTM_W), tok(MLSTM_W), tok(MLSTM_W), tok(GATE_PAD),
                  _const_spec(norm_w.shape)],
        out_specs=[tok(MLSTM_W),
                   pl.BlockSpec((1, H_M, DH_M, DH_M), lambda b, c: (b, 0, 0, 0)),
                   pl.BlockSpec((1, H_M, DH_M), lambda b, c: (b, 0, 0)),
                   pl.BlockSpec((1, H_M, LANES), lambda b, c: (b, 0, 0))],
        out_shape=[sds((batch * seq, MLSTM_W), BF16), sds((batch, H_M, DH_M, DH_M), F32),
                   sds((batch, H_M, DH_M), F32), sds((batch, H_M, LANES), F32)],
        compiler_params=_params("parallel", "arbitrary"),
        name="mlstm_prompt",
    )(b_gates, qm, km, vm, om, gates, norm_w)


def _mlstm_sample_kernel(ls, group, bg_ref, q_ref, k_ref, v_ref, o_ref, g_ref, nw_ref, c_in, n_in, m_in,
                         h_ref, c_ref, n_ref, m_ref):
    pair = 2 if group % 2 == 0 else 1

    def seq_pair(s2, carry):
        for u in range(pair):
            seq(s2 * pair + u)
        return carry

    def seq(s):
        rows = pl.ds(pl.multiple_of(s * ls, ls), ls)
        for h in range(H_M):
            cols = slice(h * DH_M, (h + 1) * DH_M)
            out, c_new, n_new, m_new = _mlstm_chunk(
                q_ref[rows, cols].astype(BF16), k_ref[rows, cols].astype(BF16), v_ref[rows, cols].astype(BF16),
                o_ref[rows, cols],
                g_ref[rows, h:h + 1] + bg_ref[h], g_ref[rows, H_M + h:H_M + h + 1] + bg_ref[H_M + h],
                c_in[s, h], n_in[s, h:h + 1, :], m_in[s, h:h + 1, 0:1], nw_ref[:, cols])
            h_ref[rows, cols] = out
            c_ref[s, h] = c_new
            n_ref[s, h:h + 1, :] = n_new
            m_ref[s, h:h + 1, :] = jnp.broadcast_to(m_new, (1, m_ref.shape[2]))

    lax.fori_loop(0, group // pair, seq_pair, 0)


def _mlstm_sample(b_gates, qm, km, vm, om, gates, norm_w, state_c, state_n, state_m, ls, group):
    bs = state_c.shape[0]
    assert bs % group == 0
    tok = lambda w: pl.BlockSpec((group * ls, w), lambda i: (i, 0))
    c_spec = pl.BlockSpec((group, H_M, DH_M, DH_M), lambda i: (i, 0, 0, 0))
    n_spec = pl.BlockSpec((group, H_M, DH_M), lambda i: (i, 0, 0))
    m_spec = pl.BlockSpec((group, H_M, LANES), lambda i: (i, 0, 0))
    sds = jax.ShapeDtypeStruct
    return pl.pallas_call(
        functools.partial(_mlstm_sample_kernel, ls, group),
        grid=(bs // group,),
        in_specs=[pl.BlockSpec(memory_space=pltpu.SMEM),
                  tok(MLSTM_W), tok(MLSTM_W), tok(MLSTM_W), tok(MLSTM_W), tok(GATE_PAD),
                  _const_spec(norm_w.shape), c_spec, n_spec, m_spec],
        out_specs=[tok(MLSTM_W), c_spec, n_spec, m_spec],
        out_shape=[sds((bs * ls, MLSTM_W), F32), sds(state_c.shape, F32),
                   sds(state_n.shape, F32), sds((bs, H_M, LANES), F32)],
        compiler_params=_params("parallel"),
        name="mlstm_sample",
    )(b_gates, qm, km, vm, om, gates, norm_w, state_c, state_n, state_m)


def _mix_kernel(head_major, x_ref, hm_ref, hd_ref, w_ref, g_ref, b_ref, y_ref, yb_ref):
    half = x_ref.shape[0] // 2
    halves = [slice(0, half), slice(half, 2 * half)]

    def project(rows):
        if head_major:
            hd = jnp.concatenate([hd_ref[h, rows, :] for h in range(H_D)], axis=-1)
        else:
            hd = hd_ref[rows, :]
        return _dot(hm_ref[rows, :].astype(BF16), w_ref[0:MLSTM_W, :]) + _dot(hd.astype(BF16), w_ref[MLSTM_W:, :])

    def finish(rows, mix):
        y = _layer_norm(ALPHA * x_ref[rows, :] + mix, g_ref[...], b_ref[...])
        y_ref[rows, :] = y
        yb_ref[rows, :] = y.astype(BF16)

    mixes = [project(rows) for rows in halves]
    for rows, mix in zip(halves, mixes):
        finish(rows, mix)


def _mix(x, hm, hd, w_out, ln_g, ln_b, tm, head_major):
    t = x.shape[0]
    assert t % tm == 0
    row = lambda w: pl.BlockSpec((tm, w), lambda i: (i, 0))
    hd_spec = pl.BlockSpec((H_D, tm, DV_D), lambda i: (0, i, 0)) if head_major else row(DIFF_W)
    return pl.pallas_call(
        functools.partial(_mix_kernel, head_major),
        grid=(t // tm,),
        in_specs=[row(D_MODEL), row(MLSTM_W), hd_spec, _resident_spec(w_out.shape),
                  _const_spec(ln_g.shape), _const_spec(ln_b.shape)],
        out_specs=[row(D_MODEL), row(D_MODEL)],
        out_shape=[jax.ShapeDtypeStruct((t, D_MODEL), F32), jax.ShapeDtypeStruct((t, D_MODEL), BF16)],
        compiler_params=_params("parallel"),
        name="mix_ln",
    )(x, hm, hd, w_out, ln_g, ln_b)


def _ffn_up_math(xb_ref, wg_ref, wu_ref, h_ref):
    xb = xb_ref[...]
    h_ref[...] = (jax.nn.silu(_dot(xb, wg_ref[...])) * _dot(xb, wu_ref[...])).astype(h_ref.dtype)


def _ffn_up_kernel(xb_ref, wg_ref, wu_ref, h_ref):
    _ffn_up_math(xb_ref, wg_ref, wu_ref, h_ref)


def _ffn_up(xb, w_gate, w_up, tm, tf):
    t = xb.shape[0]
    assert t % tm == 0 and D_FF % tf == 0
    return pl.pallas_call(
        _ffn_up_kernel,
        grid=(t // tm, D_FF // tf),
        in_specs=[pl.BlockSpec((tm, D_MODEL), lambda i, j: (i, 0)),
                  pl.BlockSpec((D_MODEL, tf), lambda i, j: (0, j)),
                  pl.BlockSpec((D_MODEL, tf), lambda i, j: (0, j))],
        out_specs=pl.BlockSpec((tm, tf), lambda i, j: (i, j)),
        out_shape=jax.ShapeDtypeStruct((t, D_FF), BF16),
        compiler_params=_params("parallel", "parallel"),
        name="ffn_up",
    )(xb, w_gate, w_up)


def _ffn_up_attn_kernel(lam_init, n_pages, ls, n_seq, pt_ref, xb_ref, wg_ref, wu_ref,
                        q_ref, kn_ref, vn_ref, sbias_ref, lam_ref, nw_ref, *rest):
    del pt_ref
    k_pages = rest[:n_pages]
    v_pages = rest[n_pages:2 * n_pages]
    h_ref, o_ref = rest[2 * n_pages:]
    step = pl.program_id(0) * pl.num_programs(1) + pl.program_id(1)

    @pl.when(step >= n_seq)
    def _():
        _ffn_up_math(xb_ref, wg_ref, wu_ref, h_ref)

    @pl.when(step < n_seq)
    def _():
        n_chunks = n_pages // 2
        kc = D_MODEL // n_chunks
        todo = [(name, w_ref, c) for c in range(n_chunks) for name, w_ref in (("gate", wg_ref), ("up", wu_ref))]
        acc = {"gate": None, "up": None}
        reads = [0]

        def after_page():
            reads[0] += 1
            if reads[0] % 2 == 0 and todo:
                name, w_ref, c = todo.pop(0)
                d = _dot(xb_ref[:, c * kc:(c + 1) * kc], w_ref[c * kc:(c + 1) * kc, :])
                acc[name] = d if acc[name] is None else acc[name] + d

        _attn_sample_body(lam_init, ls, q_ref, kn_ref, vn_ref, sbias_ref, lam_ref, nw_ref, k_pages, v_pages, o_ref,
                          after_page)
        assert not todo
        h_ref[...] = (jax.nn.silu(acc["gate"]) * acc["up"]).astype(h_ref.dtype)


def _ffn_up_attn_sample(xb, w_gate, w_up, tm, tf, page_table, q, k_new, v_new, cache_k, cache_v, sbias,
                        lam_params, norm_w, ls, lam_init):
    t = xb.shape[0]
    bs, n_pages = page_table.shape
    n_i, n_j = t // tm, D_FF // tf
    assert t % tm == 0 and D_FF % tf == 0 and n_i * n_j >= bs
    rows_per_page = PAGE_SIZE * H_D
    cache_k = cache_k.reshape(-1, DK_D)
    cache_v = cache_v.reshape(-1, DV_D)

    def seq(i, j):
        return jnp.minimum(i * n_j + j, bs - 1)

    seq_spec = pl.BlockSpec((ls, DIFF_W), lambda i, j, pt: (seq(i, j), 0))
    new_spec = pl.BlockSpec((ls * H_D, DK_D), lambda i, j, pt: (seq(i, j), 0))

    def page_spec(p):
        return pl.BlockSpec((rows_per_page, DK_D), lambda i, j, pt: (pt[seq(i, j), p], 0))

    def const(shape):
        return pl.BlockSpec(shape, lambda i, j, pt: (0,) * len(shape))

    grid_spec = pltpu.PrefetchScalarGridSpec(
        num_scalar_prefetch=1,
        grid=(n_i, n_j),
        in_specs=[pl.BlockSpec((tm, D_MODEL), lambda i, j, pt: (i, 0)),
                  pl.BlockSpec((D_MODEL, tf), lambda i, j, pt: (0, j)),
                  pl.BlockSpec((D_MODEL, tf), lambda i, j, pt: (0, j)),
                  seq_spec, new_spec, new_spec,
                  const(sbias.shape), const(lam_params.shape), const(norm_w.shape)]
        + [page_spec(p) for p in range(n_pages)] * 2,
        out_specs=[pl.BlockSpec((tm, tf), lambda i, j, pt: (i, j)), seq_spec],
    )
    return pl.pallas_call(
        functools.partial(_ffn_up_attn_kernel, lam_init, n_pages, ls, bs),
        grid_spec=grid_spec,
        out_shape=[jax.ShapeDtypeStruct((t, D_FF), BF16), jax.ShapeDtypeStruct((bs * ls, DIFF_W), F32)],
        compiler_params=_params("arbitrary", "arbitrary"),
        name="ffn_up_attn_sample",
    )(page_table, xb, w_gate, w_up, q, k_new, v_new, sbias, lam_params, norm_w,
      *([cache_k] * n_pages), *([cache_v] * n_pages))


def _ffn_down_kernel(x_ref, h_ref, wd_ref, g_ref, b_ref, y_ref):
    y_ref[...] = _layer_norm(ALPHA * x_ref[...] + _dot(h_ref[...], wd_ref[...]), g_ref[...], b_ref[...])


def _ffn_down(x, h, w_down, ln_g, ln_b, tm):
    t = x.shape[0]
    assert t % tm == 0
    row = lambda w: pl.BlockSpec((tm, w), lambda i: (i, 0))
    return pl.pallas_call(
        _ffn_down_kernel,
        grid=(t // tm,),
        in_specs=[row(D_MODEL), row(D_FF), _resident_spec(w_down.shape),
                  _const_spec(ln_g.shape), _const_spec(ln_b.shape)],
        out_specs=row(D_MODEL),
        out_shape=jax.ShapeDtypeStruct((t, D_MODEL), F32),
        compiler_params=_params("parallel"),
        name="ffn_down_ln",
    )(x, h, w_down, ln_g, ln_b)


def _tiles(seq):
    return dict(
        cast_cols=1024,
        proj_tm=256,
        attn_blk=min(256, seq),
        attn_group=4,
        mlstm_chunk=min(512, seq),
        sample_group=8,
        mix_tm=512,
        ffn_up_tm=1024,
        ffn_up_tf=256,
        ffn_down_tm=512,
    )


def kernel(x_prompt, x_sample, cache_k, cache_v, state_C, state_n, state_m, page_table, rel_bias, w_in, b_gates, lambda_q1, lambda_k1, lambda_q2, lambda_k2, diff_norm_w, mlstm_norm_w, w_out, ln1_g, ln1_b, w_gate, w_up, w_down, ln2_g, ln2_b):
    B, S, _ = x_prompt.shape
    Bs, Ls, _ = x_sample.shape
    n_pages = page_table.shape[1]
    past = n_pages * cache_k.shape[2]
    assert w_in.shape[0] == DEPTH == 1 and cache_k.shape[2] == PAGE_SIZE
    tl = _tiles(S)
    l = 0
    lam_init = 0.8 - 0.6 * math.exp(-0.3 * l)

    wt = w_in[l].T
    wd = _transpose_cast(wt, 0, DIFF_COLS, tl["cast_cols"])
    wm = _transpose_cast(wt, DIFF_COLS, MLSTM_COLS, tl["cast_cols"])
    wg = _gate_rows(wt, DIFF_COLS + MLSTM_COLS)
    lam_params = jnp.stack([lambda_q1[l], lambda_k1[l], lambda_q2[l], lambda_k2[l]], 0)
    dnw = diff_norm_w[l].reshape(1, DV_D)
    mnw = mlstm_norm_w[l].reshape(1, MLSTM_W)
    g1, b1 = ln1_g[l].reshape(1, D_MODEL), ln1_b[l].reshape(1, D_MODEL)
    g2, b2 = ln2_g[l].reshape(1, D_MODEL), ln2_b[l].reshape(1, D_MODEL)
    bg = b_gates[l]

    pbias, sbias = _bias_tables(rel_bias, tl["attn_blk"], past, Ls)
    sbias = sbias.reshape(H_D * 2 * Ls, past + PAGE_SIZE)

    xp = x_prompt.reshape(B * S, D_MODEL)
    qt, k32, k16, v32, vt, qm, km, vm, om, gates, w_o = _in_proj(xp, wd, wm, wg, tl["proj_tm"], True,
                                                                 riders=(w_out[l],))
    hd, w_g, w_u, w_d = _attn_prompt(qt, k16, vt, pbias, lam_params, dnw.reshape(DV_D, 1), B, S, tl["attn_blk"],
                                     tl["attn_group"], lam_init, riders=(w_gate[l], w_up[l], w_down[l]))
    hm, c_p, n_p, m_p = _mlstm_prompt(bg, qm, km, vm, om, gates, mnw, B, S, tl["mlstm_chunk"])
    x1, x1b = _mix(xp, hm, hd, w_o, g1, b1, tl["mix_tm"], True)

    xs = x_sample.reshape(Bs * Ls, D_MODEL)
    qs, ks, vs, qms, kms, vms, oms, gates_s = _in_proj(xs, wd, wm, wg, tl["proj_tm"], False)
    up_p, hd_s = _ffn_up_attn_sample(x1b, w_g, w_u, min(tl["ffn_up_tm"], B * S), tl["ffn_up_tf"], page_table,
                                     qs, ks, vs, cache_k, cache_v, sbias, lam_params, dnw, Ls, lam_init)
    y_p = _ffn_down(x1, up_p, w_d, g2, b2, tl["ffn_down_tm"])
    m_in = jnp.broadcast_to(state_m[l][:, :, None], (Bs, H_M, LANES))
    hm_s, c_s, n_s, m_s = _mlstm_sample(bg, qms, kms, vms, oms, gates_s, mnw,
                                        state_C[l], state_n[l], m_in, Ls, tl["sample_group"])
    x1s, x1sb = _mix(xs, hm_s, hd_s, w_o, g1, b1, tl["mix_tm"], False)
    up_s = _ffn_up(x1sb, w_g, w_u, min(tl["ffn_up_tm"], Bs * Ls), tl["ffn_up_tf"])
    y_s = _ffn_down(x1s, up_s, w_d, g2, b2, tl["ffn_down_tm"])

    return (y_p.reshape(B, S, D_MODEL), y_s.reshape(Bs, Ls, D_MODEL),
            k32.reshape(1, B, S, H_D, DK_D), v32.reshape(1, B, S, H_D, DV_D),
            c_p[None], n_p[None], m_p[None, :, :, 0],
            ks.reshape(1, Bs, Ls, H_D, DK_D), vs.reshape(1, Bs, Ls, H_D, DV_D),
            c_s[None], n_s[None], m_s[None, :, :, 0])
```

```python
import functools
import math

import numpy as np
import jax
import jax.numpy as jnp
from jax import lax
from jax.experimental import pallas as pl
from jax.experimental.pallas import tpu as pltpu

D_MODEL = 2048
DEPTH = 1
PAGE_SIZE = 128
H_D = 8
DH_HALF = 64
DK_D = 2 * DH_HALF
DV_D = 2 * DH_HALF
DIFF_W = H_D * DV_D
H_M = 4
DH_M = 256
MLSTM_W = H_M * DH_M
D_FF = -(-8 * D_MODEL // (3 * 256)) * 256
N_BUCKETS = 32
MAX_DIST = 128
ALPHA = (2 * DEPTH) ** 0.25
LN_EPS = 1e-5
RMS_EPS = 1e-6
N_GATES = 2 * H_M
LANES = 128
SUBLANES = 8
GATE_PAD = LANES
DIFF_COLS = 3 * DIFF_W
MLSTM_COLS = 4 * MLSTM_W
MASK_VALUE = -1e30
LOG2E = math.log2(math.e)

VMEM_LIMIT_BYTES = 56 * 1024 * 1024

BF16 = jnp.bfloat16
F32 = jnp.float32


def _dot(a, b):
    return jnp.dot(a, b, preferred_element_type=F32)


def _dot_nt(a, b):
    return lax.dot_general(a, b, (((1,), (1,)), ((), ())), preferred_element_type=F32)


def _dot_tn(a, b):
    return lax.dot_general(a, b, (((0,), (0,)), ((), ())), preferred_element_type=F32)


def _params(*semantics):
    return pltpu.CompilerParams(dimension_semantics=semantics, vmem_limit_bytes=VMEM_LIMIT_BYTES)


def _const_spec(shape):
    n = len(shape)
    return pl.BlockSpec(shape, lambda *_: (0,) * n)


def _cast_riders(riders, n_steps, step_of):
    in_specs, out_specs, out_shapes = [], [], []
    for w in riders:
        rows, cols = w.shape
        assert rows % n_steps == 0 and (rows // n_steps) % 16 == 0
        spec = pl.BlockSpec((rows // n_steps, cols), lambda *g: (step_of(*g), 0))
        in_specs.append(spec)
        out_specs.append(spec)
        out_shapes.append(jax.ShapeDtypeStruct(w.shape, BF16))
    return in_specs, out_specs, out_shapes


def _cast_rider_blocks(in_refs, out_refs):
    for src, dst in zip(in_refs, out_refs):
        dst[...] = src[...].astype(BF16)


def _resident_spec(shape):
    n = len(shape)
    return pl.BlockSpec(shape, lambda *_: (0,) * n, pipeline_mode=pl.Buffered(1))


def _bucket_np(dist):
    n = np.maximum(dist, 0)
    max_exact = N_BUCKETS // 2
    nf = np.maximum(n, 1).astype(np.float32)
    large = max_exact + (np.log(nf / np.float32(max_exact)) / np.float32(math.log(MAX_DIST / max_exact))
                         * np.float32(N_BUCKETS - max_exact)).astype(np.int32)
    large = np.minimum(large, N_BUCKETS - 1)
    out = np.where(n < max_exact, n, large).astype(np.int32)
    return np.where(dist < 0, -1, out).astype(np.int32)


def _bias_kernel(rb_ref, pb_ref, sb_ref, pbias_ref, sbias_ref):
    h = pl.program_id(0)

    def lookup(bk, shift, scale):
        acc = jnp.full(bk.shape, MASK_VALUE, F32)
        for b in range(N_BUCKETS):
            acc = jnp.where(bk == b, (rb_ref[b, h] - shift) * scale, acc)
        return acc

    for t in range(pb_ref.shape[0]):
        pbias_ref[0, t] = lookup(pb_ref[t], rb_ref[N_BUCKETS - 1, h], LOG2E)
    sbias_ref[0] = lookup(sb_ref[...], 0.0, 1.0)


def _bias_tables(rel_bias, blk, past, dec_seq):
    k = np.arange(blk)[:, None]
    q = np.arange(blk)[None, :]
    pb = np.stack([_bucket_np(blk + q - k), _bucket_np(q - k)], 0)
    far = N_BUCKETS - 1
    assert (_bucket_np(np.arange(MAX_DIST, 4 * past + 4 * blk)) == far).all()
    assert blk >= MAX_DIST and PAGE_SIZE >= MAX_DIST
    qpos = past + np.arange(dec_seq)[:, None]
    last = _bucket_np(qpos - (past - PAGE_SIZE + np.arange(PAGE_SIZE))[None, :])
    new = _bucket_np(qpos - (past + np.arange(PAGE_SIZE))[None, :])
    new[:, dec_seq:] = -1
    sb = np.concatenate([np.full((dec_seq, past - PAGE_SIZE), far, np.int32), last, new], 1)
    sb = np.concatenate([sb, sb], 0)
    return pl.pallas_call(
        _bias_kernel,
        grid=(H_D,),
        in_specs=[pl.BlockSpec(memory_space=pltpu.SMEM),
                  _const_spec(pb.shape), _const_spec(sb.shape)],
        out_specs=[pl.BlockSpec((1,) + pb.shape, lambda h: (h, 0, 0, 0)),
                   pl.BlockSpec((1,) + sb.shape, lambda h: (h, 0, 0))],
        out_shape=[jax.ShapeDtypeStruct((H_D,) + pb.shape, F32),
                   jax.ShapeDtypeStruct((H_D,) + sb.shape, F32)],
        compiler_params=_params("arbitrary"),
        name="bias_tables",
    )(rel_bias, jnp.asarray(pb), jnp.asarray(sb))


Q_SCALE = DH_HALF ** -0.5
K_SCALE = DH_M ** -0.5


def _head_rows(ref, h, n_tokens):
    return ref[pl.ds(h, n_tokens, stride=H_D), :]


def _store_head_rows(ref, x):
    for h in range(H_D):
        ref[pl.ds(h, x.shape[0], stride=H_D), :] = x[:, h * DK_D:(h + 1) * DK_D]


def _load_head_rows(ref, n_tokens):
    return jnp.concatenate([_head_rows(ref, h, n_tokens) for h in range(H_D)], axis=1)


def _transpose_cast_kernel(wt_ref, w_ref):
    w_ref[...] = wt_ref[...].T.astype(BF16)


def _transpose_cast(wt, first_col, n_cols, blk):
    d = wt.shape[1]
    assert first_col % blk == 0 and n_cols % blk == 0
    return pl.pallas_call(
        _transpose_cast_kernel,
        grid=(n_cols // blk,),
        in_specs=[pl.BlockSpec((blk, d), lambda j: (first_col // blk + j, 0))],
        out_specs=pl.BlockSpec((d, blk), lambda j: (0, j)),
        out_shape=jax.ShapeDtypeStruct((d, n_cols), BF16),
        compiler_params=_params("parallel"),
        name="transpose_cast",
    )(wt)


def _gate_rows_kernel(wt_ref, o_ref):
    zeros = jnp.zeros((GATE_PAD - N_GATES, wt_ref.shape[1]), F32)
    o_ref[...] = jnp.concatenate([wt_ref[...], zeros], axis=0).astype(BF16)


def _gate_rows(wt, first_row):
    d = wt.shape[1]
    assert first_row % N_GATES == 0
    return pl.pallas_call(
        _gate_rows_kernel,
        grid=(1,),
        in_specs=[pl.BlockSpec((N_GATES, d), lambda i: (first_row // N_GATES, 0))],
        out_specs=pl.BlockSpec((GATE_PAD, d), lambda i: (0, 0)),
        out_shape=jax.ShapeDtypeStruct((GATE_PAD, d), BF16),
        compiler_params=_params("arbitrary"),
        name="gate_rows",
    )(wt)


def _in_proj_prompt_kernel(n_riders, x_ref, wd_ref, wm_ref, wg_ref, *rest):
    (qt_ref, k32_ref, k16_ref, v32_ref, vt_ref,
     qm_ref, km_ref, vm_ref, om_ref, g_ref) = rest[n_riders:n_riders + 10]
    _cast_rider_blocks(rest[:n_riders], rest[n_riders + 10:])
    x = x_ref[...].astype(BF16)
    r = _dot(x, wd_ref[:, 0:DIFF_W]) * (Q_SCALE * LOG2E)
    for h in range(H_D):
        qt_ref[h] = r[:, h * DK_D:(h + 1) * DK_D].T.astype(BF16)
    r = _dot(x, wd_ref[:, DIFF_W:2 * DIFF_W])
    _store_head_rows(k32_ref, r)
    for h in range(H_D):
        k16_ref[h] = r[:, h * DK_D:(h + 1) * DK_D].astype(BF16)
    r = _dot(x, wd_ref[:, 2 * DIFF_W:3 * DIFF_W])
    _store_head_rows(v32_ref, r)
    for h in range(H_D):
        vt_ref[h] = r[:, h * DV_D:(h + 1) * DV_D].T.astype(BF16)
    qm_ref[...] = _dot(x, wm_ref[:, 0:MLSTM_W]).astype(BF16)
    km_ref[...] = (_dot(x, wm_ref[:, MLSTM_W:2 * MLSTM_W]) * K_SCALE).astype(BF16)
    vm_ref[...] = _dot(x, wm_ref[:, 2 * MLSTM_W:3 * MLSTM_W]).astype(BF16)
    om_ref[...] = _dot(x, wm_ref[:, 3 * MLSTM_W:4 * MLSTM_W]).astype(BF16)
    g_ref[...] = _dot_nt(x, wg_ref[...])


def _in_proj_sample_kernel(x_ref, wd_ref, wm_ref, wg_ref,
                           q_ref, k_ref, v_ref, qm_ref, km_ref, vm_ref, om_ref, g_ref):
    x = x_ref[...].astype(BF16)
    q_ref[...] = _dot(x, wd_ref[:, 0:DIFF_W]) * Q_SCALE
    _store_head_rows(k_ref, _dot(x, wd_ref[:, DIFF_W:2 * DIFF_W]))
    _store_head_rows(v_ref, _dot(x, wd_ref[:, 2 * DIFF_W:3 * DIFF_W]))
    qm_ref[...] = _dot(x, wm_ref[:, 0:MLSTM_W])
    km_ref[...] = _dot(x, wm_ref[:, MLSTM_W:2 * MLSTM_W]) * K_SCALE
    vm_ref[...] = _dot(x, wm_ref[:, 2 * MLSTM_W:3 * MLSTM_W])
    om_ref[...] = _dot(x, wm_ref[:, 3 * MLSTM_W:4 * MLSTM_W])
    g_ref[...] = _dot_nt(x, wg_ref[...])


def _in_proj(x, wd, wm, wg, tm, prompt, riders=()):
    t = x.shape[0]
    assert t % tm == 0
    r_in, r_out, r_shapes = _cast_riders(riders, t // tm, lambda i: i)
    row = lambda w: pl.BlockSpec((tm, w), lambda i: (i, 0))
    heads = pl.BlockSpec((H_D, tm, DK_D), lambda i: (0, i, 0))
    cache = pl.BlockSpec((tm * H_D, DK_D), lambda i: (i, 0))
    sds = jax.ShapeDtypeStruct
    if prompt:
        body = functools.partial(_in_proj_prompt_kernel, len(riders))
        heads_t = pl.BlockSpec((H_D, DK_D, tm), lambda i: (0, 0, i))
        out_specs = [heads_t, cache, heads, cache, heads_t] + [row(MLSTM_W)] * 4 + [row(GATE_PAD)]
        out_shape = [sds((H_D, DK_D, t), BF16), sds((t * H_D, DK_D), F32), sds((H_D, t, DK_D), BF16),
                     sds((t * H_D, DV_D), F32), sds((H_D, DV_D, t), BF16)]
        out_shape += [sds((t, MLSTM_W), BF16)] * 4 + [sds((t, GATE_PAD), F32)]
    else:
        body = _in_proj_sample_kernel
        out_specs = [row(DIFF_W), cache, cache] + [row(MLSTM_W)] * 4 + [row(GATE_PAD)]
        out_shape = ([sds((t, DIFF_W), F32)] + [sds((t * H_D, DK_D), F32)] * 2 + [sds((t, MLSTM_W), F32)] * 4
                     + [sds((t, GATE_PAD), F32)])
    return pl.pallas_call(
        body,
        grid=(t // tm,),
        in_specs=[row(D_MODEL), _resident_spec(wd.shape), _resident_spec(wm.shape), _resident_spec(wg.shape)] + r_in,
        out_specs=out_specs + r_out,
        out_shape=out_shape + r_shapes,
        compiler_params=_params("parallel"),
        name="in_proj_prompt" if prompt else "in_proj_sample",
    )(x, wd, wm, wg, *riders)


def _lambda_value(lam_ref, lam_init):
    lp = lam_ref[...]
    e1 = jnp.exp(jnp.sum(lp[0:1] * lp[1:2], axis=1, keepdims=True))
    e2 = jnp.exp(jnp.sum(lp[2:3] * lp[3:4], axis=1, keepdims=True))
    return e1 - e2 + lam_init


def _head_rms(o, w):
    return o * lax.rsqrt(jnp.mean(o * o, axis=-1, keepdims=True) + RMS_EPS) * w


def _layer_norm(x, g, b):
    xc = x - jnp.mean(x, axis=-1, keepdims=True)
    var = jnp.mean(xc * xc, axis=-1, keepdims=True)
    return xc * lax.rsqrt(var + LN_EPS) * g + b


def _attn_prompt_kernel(lam_init, tq, nq, group, n_riders, qt_ref, k_ref, vt_ref, pbias_ref, lam_ref, nw_ref, *rest):
    o_ref = rest[n_riders]
    _cast_rider_blocks(rest[:n_riders], rest[n_riders + 1:])
    i = pl.program_id(1)
    lam = _lambda_value(lam_ref, lam_init)
    chan = lax.broadcasted_iota(jnp.int32, (DK_D, tq), 0)
    first_map = chan < DH_HALF

    def fold8(op, s):
        return op(s.reshape(s.shape[0] // SUBLANES, SUBLANES, s.shape[1]), axis=0)

    def process(ii):
        n = (ii + 1) * tq
        n_far = max(n - 2 * tq, 0)

        def scores(h):
            qt = qt_ref[h]
            zero = jnp.zeros_like(qt)
            q2t = jnp.concatenate([jnp.where(first_map, qt, zero), jnp.where(first_map, zero, qt)], axis=1)
            parts = []
            if n_far:
                parts.append(_dot(k_ref[h, 0:n_far, :], q2t))
            if ii >= 1:
                bs = pbias_ref[h, 0]
                parts.append(_dot(k_ref[h, n - 2 * tq:n - tq, :], q2t) + jnp.concatenate([bs, bs], axis=1))
            bd = pbias_ref[h, 1]
            parts.append(_dot(k_ref[h, n - tq:n, :], q2t) + jnp.concatenate([bd, bd], axis=1))
            return parts

        def softmax(parts):
            m8 = fold8(jnp.max, parts[0])
            for s in parts[1:]:
                m8 = jnp.maximum(m8, fold8(jnp.max, s))
            m = jnp.max(m8, axis=0, keepdims=True)
            probs = [jnp.exp2(s - m) for s in parts]
            l8 = fold8(jnp.sum, probs[0])
            for p in probs[1:]:
                l8 = l8 + fold8(jnp.sum, p)
            l = jnp.sum(l8, axis=0, keepdims=True)
            return jnp.concatenate([p.astype(BF16) for p in probs], axis=0), l

        def values(h, p_all, l):
            o = _dot(vt_ref[h, :, 0:n], p_all) / l
            o = o[:, :tq] - lam * o[:, tq:]
            o = o * lax.rsqrt(jnp.mean(o * o, axis=0, keepdims=True) + RMS_EPS) * (nw_ref[...] * (1.0 - lam_init))
            o_ref[h] = o.T.astype(o_ref.dtype)

        def head_group(g, carry):
            hs = [g * group + u for u in range(group)]
            sc, sm = {}, {}
            for t in range(group + 2):
                if t < group:
                    sc[t] = scores(hs[t])
                if 0 <= t - 1 < group:
                    sm[t - 1] = softmax(sc.pop(t - 1))
                if 0 <= t - 2 < group:
                    values(hs[t - 2], *sm.pop(t - 2))
            return carry

        lax.fori_loop(0, H_D // group, head_group, 0)

    for ii in range(nq):
        pl.when(i == ii)(functools.partial(process, ii))


def _attn_prompt(qt, k, vt, pbias, lam_params, norm_w, batch, seq, blk, group, lam_init, riders):
    nq = seq // blk
    assert H_D % group == 0
    r_in, r_out, r_shapes = _cast_riders(riders, batch * nq, lambda b, i: b * nq + i)
    return pl.pallas_call(
        functools.partial(_attn_prompt_kernel, lam_init, blk, nq, group, len(riders)),
        grid=(batch, nq),
        in_specs=[pl.BlockSpec((H_D, DK_D, blk), lambda b, i: (0, 0, b * nq + i)),
                  pl.BlockSpec((H_D, seq, DK_D), lambda b, i: (0, b, 0)),
                  pl.BlockSpec((H_D, DV_D, seq), lambda b, i: (0, 0, b)),
                  _const_spec(pbias.shape),
                  _const_spec(lam_params.shape),
                  _const_spec(norm_w.shape)] + r_in,
        out_specs=[pl.BlockSpec((H_D, blk, DV_D), lambda b, i: (0, b * nq + i, 0))] + r_out,
        out_shape=[jax.ShapeDtypeStruct((H_D, batch * seq, DV_D), BF16)] + r_shapes,
        compiler_params=_params("parallel", "parallel"),
        name="attn_prompt",
    )(qt, k, vt, pbias, lam_params, norm_w, *riders)


def _attn_sample_body(lam_init, ls, q_ref, kn_ref, vn_ref, sbias_ref, lam_ref, nw_ref, k_pages, v_pages, o_ref,
                      after_page=None):
    n_pages = len(k_pages)
    rows = 2 * ls * H_D
    lam = _lambda_value(lam_ref, lam_init)

    q = q_ref[...]
    qt = jnp.concatenate([q] * (2 * H_D), axis=0)
    rowi = lax.broadcasted_iota(jnp.int32, (rows, DIFF_W), 0)
    coli = lax.broadcasted_iota(jnp.int32, (rows, DIFF_W), 1)
    q_bd = jnp.where(coli // DH_HALF == rowi // ls, qt, 0.0).astype(BF16)

    pad = jnp.zeros((PAGE_SIZE - ls, DIFF_W), F32)

    def page(ref):
        return _load_head_rows(ref, PAGE_SIZE).astype(BF16)

    k_new = jnp.concatenate([_load_head_rows(kn_ref, ls), pad], axis=0).astype(BF16)
    v_new = jnp.concatenate([_load_head_rows(vn_ref, ls), pad], axis=0).astype(BF16)

    def pages(refs):
        out = []
        for r in refs:
            out.append(page(r))
            if after_page is not None:
                after_page()
        return out

    k_all = jnp.concatenate(pages(k_pages) + [k_new], axis=0)
    n_keys = k_all.shape[0]
    s = _dot_nt(q_bd, k_all) + sbias_ref[...]
    m = jnp.max(s, axis=1, keepdims=True)
    p = jnp.exp(s - m)
    l = jnp.sum(p, axis=1, keepdims=True)
    second_map = (lax.broadcasted_iota(jnp.int32, (rows, 1), 0) // ls) % 2 == 1
    fac = jnp.where(second_map, -lam, 1.0) / l
    p3 = (p * fac).reshape(H_D, 2 * ls, n_keys)
    w = (p3[:, :ls] + p3[:, ls:]).reshape(H_D * ls, n_keys).astype(BF16)
    v_all = jnp.concatenate(pages(v_pages) + [v_new], axis=0)
    acc = _dot(w, v_all)
    for h in range(H_D):
        o = acc[h * ls:(h + 1) * ls, h * DV_D:(h + 1) * DV_D]
        o_ref[:, h * DV_D:(h + 1) * DV_D] = _head_rms(o, nw_ref[...]) * (1.0 - lam_init)


def _mlstm_chunk(q, k, v, o_pre, i_pre, f_pre, c_prev, n_prev, m_prev, norm_w):
    L = q.shape[0]
    t_idx = lax.broadcasted_iota(jnp.int32, (L, L), 0)
    s_idx = lax.broadcasted_iota(jnp.int32, (L, L), 1)
    causal = s_idx <= t_idx
    eye = s_idx == t_idx

    def to_row(col):
        return jnp.sum(jnp.where(eye, col, 0.0), axis=0, keepdims=True)

    it_col = i_pre
    lf_col = jax.nn.log_sigmoid(f_pre)
    it_row = to_row(it_col)
    b_row = jnp.sum(jnp.where(t_idx <= s_idx, lf_col, 0.0), axis=0, keepdims=True)
    b_col = jnp.sum(jnp.where(causal, to_row(lf_col), 0.0), axis=1, keepdims=True)

    log_d = jnp.where(causal, b_col - b_row + it_row, -jnp.inf)
    m_t = jnp.maximum(b_col + m_prev, jnp.max(log_d, axis=1, keepdims=True))
    dmat = jnp.exp(log_d - m_t)
    inter = jnp.exp(b_col + m_prev - m_t)

    w = _dot_nt(q, k) * dmat
    kf = k.astype(F32)
    num = inter * _dot_nt(q, c_prev.astype(BF16)) + _dot(w.astype(BF16), v)
    den = inter * jnp.sum(q.astype(F32) * n_prev, axis=1, keepdims=True) + jnp.sum(w, axis=1, keepdims=True)
    h = num / jnp.maximum(jnp.abs(den), jnp.exp(-m_t))
    h = _head_rms(h, norm_w) * jax.nn.sigmoid(o_pre.astype(F32))

    m_new = m_t[L - 1:L]
    b_last = b_col[L - 1:L]
    g = jnp.exp(b_last - b_col + it_col - m_new)
    decay = jnp.exp(b_last + m_prev - m_new)
    gk = g * kf
    c_new = decay * c_prev + _dot_tn(v, gk.astype(BF16))
    n_new = decay * n_prev + jnp.sum(gk, axis=0, keepdims=True)
    return h, c_new, n_new, m_new


def _mlstm_prompt_kernel(n_riders, bg_ref, q_ref, k_ref, v_ref, o_ref, g_ref, nw_ref, *rest):
    h_ref, c_ref, n_ref, m_ref = rest[n_riders:n_riders + 4]
    _cast_rider_blocks(rest[:n_riders], rest[n_riders + 4:])
    @pl.when(pl.program_id(1) == 0)
    def _():
        c_ref[...] = jnp.zeros_like(c_ref)
        n_ref[...] = jnp.zeros_like(n_ref)
        m_ref[...] = jnp.zeros_like(m_ref)

    for h in range(H_M):
        cols = slice(h * DH_M, (h + 1) * DH_M)
        out, c_new, n_new, m_new = _mlstm_chunk(
            q_ref[:, cols], k_ref[:, cols], v_ref[:, cols], o_ref[:, cols],
            g_ref[:, h:h + 1] + bg_ref[h], g_ref[:, H_M + h:H_M + h + 1] + bg_ref[H_M + h],
            c_ref[0, h], n_ref[0, h:h + 1, :], m_ref[0, h:h + 1, 0:1], nw_ref[:, cols])
        h_ref[:, cols] = out.astype(h_ref.dtype)
        c_ref[0, h] = c_new
        n_ref[0, h:h + 1, :] = n_new
        m_ref[0, h:h + 1, :] = jnp.broadcast_to(m_new, (1, m_ref.shape[2]))


def _mlstm_prompt(b_gates, qm, km, vm, om, gates, norm_w, batch, seq, chunk, riders):
    nc = seq // chunk
    tok = lambda w: pl.BlockSpec((chunk, w), lambda b, c: (b * nc + c, 0))
    sds = jax.ShapeDtypeStruct
    r_in, r_out, r_shapes = _cast_riders(riders, batch * nc, lambda b, c: b * nc + c)
    return pl.pallas_call(
        functools.partial(_mlstm_prompt_kernel, len(riders)),
        grid=(batch, nc),
        in_specs=[pl.BlockSpec(memory_space=pltpu.SMEM),
                  tok(MLSTM_W), tok(MLSTM_W), tok(MLSTM_W), tok(MLSTM_W), tok(GATE_PAD),
                  _const_spec(norm_w.shape)] + r_in,
        out_specs=[tok(MLSTM_W),
                   pl.BlockSpec((1, H_M, DH_M, DH_M), lambda b, c: (b, 0, 0, 0)),
                   pl.BlockSpec((1, H_M, DH_M), lambda b, c: (b, 0, 0)),
                   pl.BlockSpec((1, H_M, LANES), lambda b, c: (b, 0, 0))] + r_out,
        out_shape=[sds((batch * seq, MLSTM_W), BF16), sds((batch, H_M, DH_M, DH_M), F32),
                   sds((batch, H_M, DH_M), F32), sds((batch, H_M, LANES), F32)] + r_shapes,
        compiler_params=_params("parallel", "arbitrary"),
        name="mlstm_prompt",
    )(b_gates, qm, km, vm, om, gates, norm_w, *riders)


def _mlstm_sample_kernel(ls, group, bg_ref, q_ref, k_ref, v_ref, o_ref, g_ref, nw_ref, c_in, n_in, m_in,
                         h_ref, c_ref, n_ref, m_ref):
    pair = 2 if group % 2 == 0 else 1

    def seq_pair(s2, carry):
        for u in range(pair):
            seq(s2 * pair + u)
        return carry

    def seq(s):
        rows = pl.ds(pl.multiple_of(s * ls, ls), ls)
        for h in range(H_M):
            cols = slice(h * DH_M, (h + 1) * DH_M)
            out, c_new, n_new, m_new = _mlstm_chunk(
                q_ref[rows, cols].astype(BF16), k_ref[rows, cols].astype(BF16), v_ref[rows, cols].astype(BF16),
                o_ref[rows, cols],
                g_ref[rows, h:h + 1] + bg_ref[h], g_ref[rows, H_M + h:H_M + h + 1] + bg_ref[H_M + h],
                c_in[s, h], n_in[s, h:h + 1, :], m_in[s, h:h + 1, 0:1], nw_ref[:, cols])
            h_ref[rows, cols] = out
            c_ref[s, h] = c_new
            n_ref[s, h:h + 1, :] = n_new
            m_ref[s, h:h + 1, :] = jnp.broadcast_to(m_new, (1, m_ref.shape[2]))

    lax.fori_loop(0, group // pair, seq_pair, 0)


def _mlstm_sample(b_gates, qm, km, vm, om, gates, norm_w, state_c, state_n, state_m, ls, group):
    bs = state_c.shape[0]
    assert bs % group == 0
    tok = lambda w: pl.BlockSpec((group * ls, w), lambda i: (i, 0))
    c_spec = pl.BlockSpec((group, H_M, DH_M, DH_M), lambda i: (i, 0, 0, 0))
    n_spec = pl.BlockSpec((group, H_M, DH_M), lambda i: (i, 0, 0))
    m_spec = pl.BlockSpec((group, H_M, LANES), lambda i: (i, 0, 0))
    sds = jax.ShapeDtypeStruct
    return pl.pallas_call(
        functools.partial(_mlstm_sample_kernel, ls, group),
        grid=(bs // group,),
        in_specs=[pl.BlockSpec(memory_space=pltpu.SMEM),
                  tok(MLSTM_W), tok(MLSTM_W), tok(MLSTM_W), tok(MLSTM_W), tok(GATE_PAD),
                  _const_spec(norm_w.shape), c_spec, n_spec, m_spec],
        out_specs=[tok(MLSTM_W), c_spec, n_spec, m_spec],
        out_shape=[sds((bs * ls, MLSTM_W), F32), sds(state_c.shape, F32),
                   sds(state_n.shape, F32), sds((bs, H_M, LANES), F32)],
        compiler_params=_params("parallel"),
        name="mlstm_sample",
    )(b_gates, qm, km, vm, om, gates, norm_w, state_c, state_n, state_m)


def _mix_kernel(head_major, x_ref, hm_ref, hd_ref, w_ref, g_ref, b_ref, y_ref, yb_ref):
    half = x_ref.shape[0] // 2
    halves = [slice(0, half), slice(half, 2 * half)]

    def project(rows):
        if head_major:
            hd = jnp.concatenate([hd_ref[h, rows, :] for h in range(H_D)], axis=-1)
        else:
            hd = hd_ref[rows, :]
        return _dot(hm_ref[rows, :].astype(BF16), w_ref[0:MLSTM_W, :]) + _dot(hd.astype(BF16), w_ref[MLSTM_W:, :])

    def finish(rows, mix):
        y = _layer_norm(ALPHA * x_ref[rows, :] + mix, g_ref[...], b_ref[...])
        y_ref[rows, :] = y
        yb_ref[rows, :] = y.astype(BF16)

    mixes = [project(rows) for rows in halves]
    for rows, mix in zip(halves, mixes):
        finish(rows, mix)


def _mix(x, hm, hd, w_out, ln_g, ln_b, tm, head_major):
    t = x.shape[0]
    assert t % tm == 0
    row = lambda w: pl.BlockSpec((tm, w), lambda i: (i, 0))
    hd_spec = pl.BlockSpec((H_D, tm, DV_D), lambda i: (0, i, 0)) if head_major else row(DIFF_W)
    return pl.pallas_call(
        functools.partial(_mix_kernel, head_major),
        grid=(t // tm,),
        in_specs=[row(D_MODEL), row(MLSTM_W), hd_spec, _resident_spec(w_out.shape),
                  _const_spec(ln_g.shape), _const_spec(ln_b.shape)],
        out_specs=[row(D_MODEL), row(D_MODEL)],
        out_shape=[jax.ShapeDtypeStruct((t, D_MODEL), F32), jax.ShapeDtypeStruct((t, D_MODEL), BF16)],
        compiler_params=_params("parallel"),
        name="mix_ln",
    )(x, hm, hd, w_out, ln_g, ln_b)


def _ffn_up_math(xb_ref, wg_ref, wu_ref, h_ref):
    xb = xb_ref[...]
    h_ref[...] = (jax.nn.silu(_dot(xb, wg_ref[...])) * _dot(xb, wu_ref[...])).astype(h_ref.dtype)


def _ffn_up_kernel(xb_ref, wg_ref, wu_ref, h_ref):
    _ffn_up_math(xb_ref, wg_ref, wu_ref, h_ref)


def _ffn_up(xb, w_gate, w_up, tm, tf):
    t = xb.shape[0]
    assert t % tm == 0 and D_FF % tf == 0
    return pl.pallas_call(
        _ffn_up_kernel,
        grid=(t // tm, D_FF // tf),
        in_specs=[pl.BlockSpec((tm, D_MODEL), lambda i, j: (i, 0)),
                  pl.BlockSpec((D_MODEL, tf), lambda i, j: (0, j)),
                  pl.BlockSpec((D_MODEL, tf), lambda i, j: (0, j))],
        out_specs=pl.BlockSpec((tm, tf), lambda i, j: (i, j)),
        out_shape=jax.ShapeDtypeStruct((t, D_FF), BF16),
        compiler_params=_params("parallel", "parallel"),
        name="ffn_up",
    )(xb, w_gate, w_up)


def _ffn_up_attn_kernel(lam_init, n_pages, ls, n_seq, pt_ref, xb_ref, wg_ref, wu_ref,
                        q_ref, kn_ref, vn_ref, sbias_ref, lam_ref, nw_ref, *rest):
    del pt_ref
    k_pages = rest[:n_pages]
    v_pages = rest[n_pages:2 * n_pages]
    h_ref, o_ref = rest[2 * n_pages:]
    step = pl.program_id(0) * pl.num_programs(1) + pl.program_id(1)

    @pl.when(step >= n_seq)
    def _():
        _ffn_up_math(xb_ref, wg_ref, wu_ref, h_ref)

    @pl.when(step < n_seq)
    def _():
        n_chunks = n_pages // 2
        kc = D_MODEL // n_chunks
        todo = [(name, w_ref, c) for c in range(n_chunks) for name, w_ref in (("gate", wg_ref), ("up", wu_ref))]
        acc = {"gate": None, "up": None}
        reads = [0]

        def after_page():
            reads[0] += 1
            if reads[0] % 2 == 0 and todo:
                name, w_ref, c = todo.pop(0)
                d = _dot(xb_ref[:, c * kc:(c + 1) * kc], w_ref[c * kc:(c + 1) * kc, :])
                acc[name] = d if acc[name] is None else acc[name] + d

        _attn_sample_body(lam_init, ls, q_ref, kn_ref, vn_ref, sbias_ref, lam_ref, nw_ref, k_pages, v_pages, o_ref,
                          after_page)
        assert not todo
        h_ref[...] = (jax.nn.silu(acc["gate"]) * acc["up"]).astype(h_ref.dtype)


def _ffn_up_attn_sample(xb, w_gate, w_up, tm, tf, page_table, q, k_new, v_new, cache_k, cache_v, sbias,
                        lam_params, norm_w, ls, lam_init):
    t = xb.shape[0]
    bs, n_pages = page_table.shape
    n_i, n_j = t // tm, D_FF // tf
    assert t % tm == 0 and D_FF % tf == 0 and n_i * n_j >= bs
    rows_per_page = PAGE_SIZE * H_D
    cache_k = cache_k.reshape(-1, DK_D)
    cache_v = cache_v.reshape(-1, DV_D)

    def seq(i, j):
        return jnp.minimum(i * n_j + j, bs - 1)

    seq_spec = pl.BlockSpec((ls, DIFF_W), lambda i, j, pt: (seq(i, j), 0))
    new_spec = pl.BlockSpec((ls * H_D, DK_D), lambda i, j, pt: (seq(i, j), 0))

    def page_spec(p):
        return pl.BlockSpec((rows_per_page, DK_D), lambda i, j, pt: (pt[seq(i, j), p], 0))

    def const(shape):
        return pl.BlockSpec(shape, lambda i, j, pt: (0,) * len(shape))

    grid_spec = pltpu.PrefetchScalarGridSpec(
        num_scalar_prefetch=1,
        grid=(n_i, n_j),
        in_specs=[pl.BlockSpec((tm, D_MODEL), lambda i, j, pt: (i, 0)),
                  pl.BlockSpec((D_MODEL, tf), lambda i, j, pt: (0, j)),
                  pl.BlockSpec((D_MODEL, tf), lambda i, j, pt: (0, j)),
                  seq_spec, new_spec, new_spec,
                  const(sbias.shape), const(lam_params.shape), const(norm_w.shape)]
        + [page_spec(p) for p in range(n_pages)] * 2,
        out_specs=[pl.BlockSpec((tm, tf), lambda i, j, pt: (i, j)), seq_spec],
    )
    return pl.pallas_call(
        functools.partial(_ffn_up_attn_kernel, lam_init, n_pages, ls, bs),
        grid_spec=grid_spec,
        out_shape=[jax.ShapeDtypeStruct((t, D_FF), BF16), jax.ShapeDtypeStruct((bs * ls, DIFF_W), F32)],
        compiler_params=_params("arbitrary", "arbitrary"),
        name="ffn_up_attn_sample",
    )(page_table, xb, w_gate, w_up, q, k_new, v_new, sbias, lam_params, norm_w,
      *([cache_k] * n_pages), *([cache_v] * n_pages))


def _ffn_down_kernel(x_ref, h_ref, wd_ref, g_ref, b_ref, y_ref):
    y_ref[...] = _layer_norm(ALPHA * x_ref[...] + _dot(h_ref[...], wd_ref[...]), g_ref[...], b_ref[...])


def _ffn_down(x, h, w_down, ln_g, ln_b, tm):
    t = x.shape[0]
    assert t % tm == 0
    row = lambda w: pl.BlockSpec((tm, w), lambda i: (i, 0))
    return pl.pallas_call(
        _ffn_down_kernel,
        grid=(t // tm,),
        in_specs=[row(D_MODEL), row(D_FF), _resident_spec(w_down.shape),
                  _const_spec(ln_g.shape), _const_spec(ln_b.shape)],
        out_specs=row(D_MODEL),
        out_shape=jax.ShapeDtypeStruct((t, D_MODEL), F32),
        compiler_params=_params("parallel"),
        name="ffn_down_ln",
    )(x, h, w_down, ln_g, ln_b)


def _tiles(seq):
    return dict(
        cast_cols=512,
        proj_tm=256,
        attn_blk=min(256, seq),
        attn_group=4,
        mlstm_chunk=min(512, seq),
        sample_group=4,
        mix_tm=512,
        ffn_up_tm=1024,
        ffn_up_tf=256,
        ffn_down_tm=256,
    )


def kernel(x_prompt, x_sample, cache_k, cache_v, state_C, state_n, state_m, page_table, rel_bias, w_in, b_gates, lambda_q1, lambda_k1, lambda_q2, lambda_k2, diff_norm_w, mlstm_norm_w, w_out, ln1_g, ln1_b, w_gate, w_up, w_down, ln2_g, ln2_b):
    B, S, _ = x_prompt.shape
    Bs, Ls, _ = x_sample.shape
    n_pages = page_table.shape[1]
    past = n_pages * cache_k.shape[2]
    assert w_in.shape[0] == DEPTH == 1 and cache_k.shape[2] == PAGE_SIZE
    tl = _tiles(S)
    l = 0
    lam_init = 0.8 - 0.6 * math.exp(-0.3 * l)

    wt = w_in[l].T
    wd = _transpose_cast(wt, 0, DIFF_COLS, tl["cast_cols"])
    wm = _transpose_cast(wt, DIFF_COLS, MLSTM_COLS, tl["cast_cols"])
    wg = _gate_rows(wt, DIFF_COLS + MLSTM_COLS)
    lam_params = jnp.stack([lambda_q1[l], lambda_k1[l], lambda_q2[l], lambda_k2[l]], 0)
    dnw = diff_norm_w[l].reshape(1, DV_D)
    mnw = mlstm_norm_w[l].reshape(1, MLSTM_W)
    g1, b1 = ln1_g[l].reshape(1, D_MODEL), ln1_b[l].reshape(1, D_MODEL)
    g2, b2 = ln2_g[l].reshape(1, D_MODEL), ln2_b[l].reshape(1, D_MODEL)
    bg = b_gates[l]

    pbias, sbias = _bias_tables(rel_bias, tl["attn_blk"], past, Ls)
    sbias = sbias.reshape(H_D * 2 * Ls, past + PAGE_SIZE)

    xp = x_prompt.reshape(B * S, D_MODEL)
    qt, k32, k16, v32, vt, qm, km, vm, om, gates, w_o, w_d = _in_proj(xp, wd, wm, wg, tl["proj_tm"], True,
                                                                      riders=(w_out[l], w_down[l]))
    hd, = _attn_prompt(qt, k16, vt, pbias, lam_params, dnw.reshape(DV_D, 1), B, S, tl["attn_blk"],
                       tl["attn_group"], lam_init, riders=())
    hm, c_p, n_p, m_p, w_g, w_u = _mlstm_prompt(bg, qm, km, vm, om, gates, mnw, B, S, tl["mlstm_chunk"],
                                                riders=(w_gate[l], w_up[l]))
    x1, x1b = _mix(xp, hm, hd, w_o, g1, b1, tl["mix_tm"], True)

    xs = x_sample.reshape(Bs * Ls, D_MODEL)
    qs, ks, vs, qms, kms, vms, oms, gates_s = _in_proj(xs, wd, wm, wg, tl["proj_tm"], False)
    up_p, hd_s = _ffn_up_attn_sample(x1b, w_g, w_u, min(tl["ffn_up_tm"], B * S), tl["ffn_up_tf"], page_table,
                                     qs, ks, vs, cache_k, cache_v, sbias, lam_params, dnw, Ls, lam_init)
    y_p = _ffn_down(x1, up_p, w_d, g2, b2, tl["ffn_down_tm"])
    m_in = jnp.broadcast_to(state_m[l][:, :, None], (Bs, H_M, LANES))
    hm_s, c_s, n_s, m_s = _mlstm_sample(bg, qms, kms, vms, oms, gates_s, mnw,
                                        state_C[l], state_n[l], m_in, Ls, tl["sample_group"])
    x1s, x1sb = _mix(xs, hm_s, hd_s, w_o, g1, b1, tl["mix_tm"], False)
    up_s = _ffn_up(x1sb, w_g, w_u, min(tl["ffn_up_tm"], Bs * Ls), tl["ffn_up_tf"])
    y_s = _ffn_down(x1s, up_s, w_d, g2, b2, tl["ffn_down_tm"])

    return (y_p.reshape(B, S, D_MODEL), y_s.reshape(Bs, Ls, D_MODEL),
            k32.reshape(1, B, S, H_D, DK_D), v32.reshape(1, B, S, H_D, DV_D),
            c_p[None], n_p[None], m_p[None, :, :, 0],
            ks.reshape(1, Bs, Ls, H_D, DK_D), vs.reshape(1, Bs, Ls, H_D, DV_D),
            c_s[None], n_s[None], m_s[None, :, :, 0])
```
